```python
import jax, jax.numpy as jnp
from jax import lax
import numpy as np

D_MODEL = 1024
BATCH = 8
SEQ = 2048
DEPTH = 4

GRID_W = 64
CTX_LEN = 256
HEAD_DIM = 64
NA_HEADS = 4
NA_WIN_ROWS = 8
NA_WIN_COLS = 16
NA_SCALE = HEAD_DIM ** -0.5
POOL_GROUPS = 4
POOL_CH = 64
POOL_WINDOWS = (2, 4, 8, 16)
FFT_GROUPS = 4
FFT_CH = 64
MLA_HEADS = 4
MLA_Q_RANK = 256
MLA_KV_RANK = 128
MLA_NOPE = 64
MLA_ROPE = 32
MLA_V = 64
MLA_SCALE = (MLA_NOPE + MLA_ROPE) ** -0.5
ROPE_BASE = 10000.0
NA_W = NA_HEADS * HEAD_DIM
POOL_W = POOL_GROUPS * POOL_CH
FFT_W = FFT_GROUPS * FFT_CH
MLA_W = MLA_HEADS * MLA_V
BRANCH_W = NA_W
N_BRANCH = 4
IN_SPLITS = tuple(int(s) for s in np.cumsum([NA_W, NA_W, NA_W, POOL_W, FFT_W, MLA_Q_RANK, MLA_KV_RANK, MLA_ROPE]))
IN_COLS = IN_SPLITS[-1] + N_BRANCH * D_MODEL
N_EXPERTS = 16
EXPERT_FF = 2048
EC_CAPACITY = 2
QUERY_BLOCK = 128
DN_ALPHA = (2 * DEPTH) ** 0.25
DN_BETA = (8 * DEPTH) ** -0.25
LN_EPS = 1e-5
NEG = -1e30

kernel_name = "hybrid_na_pool_fourier_mla_ecmoe_dit"


def layer_norm(x, g, b):
    xf = x.astype(jnp.float32)
    mu = xf.mean(-1, keepdims=True)
    var = jnp.square(xf - mu).mean(-1, keepdims=True)
    y = (xf - mu) * lax.rsqrt(var + LN_EPS)
    return (y * g.astype(jnp.float32) + b.astype(jnp.float32)).astype(x.dtype)


def rms_norm(x, g):
    xf = x.astype(jnp.float32)
    y = xf * lax.rsqrt(jnp.mean(xf * xf, -1, keepdims=True) + LN_EPS)
    return (y * g.astype(jnp.float32)).astype(x.dtype)


def axial_rope(L):
    n_freq = MLA_ROPE // 4
    inv = ROPE_BASE ** (-jnp.arange(n_freq, dtype=jnp.float32) / n_freq)
    t = jnp.arange(L)
    row = (t // GRID_W).astype(jnp.float32)
    col = (t % GRID_W).astype(jnp.float32)
    ang = jnp.concatenate([row[:, None] * inv, col[:, None] * inv], axis=-1)
    return jnp.cos(ang), jnp.sin(ang)


def apply_rope(x, cos, sin):
    xf = x.astype(jnp.float32)
    x1, x2 = xf[..., 0::2], xf[..., 1::2]
    c, s = cos[None, :, None, :], sin[None, :, None, :]
    out = jnp.stack([x1 * c - x2 * s, x1 * s + x2 * c], axis=-1).reshape(x.shape)
    return out.astype(x.dtype)


def attend(q, k, v, scale):
    s = jnp.einsum('bqhd,bkhd->bhqk', q, k).astype(jnp.float32) * scale
    p = jax.nn.softmax(s, axis=-1).astype(v.dtype)
    return jnp.einsum('bhqk,bkhd->bqhd', p, v)


def attend_blocked(q, k, v, scale):
    B, L, H, dk = q.shape
    nb = L // QUERY_BLOCK
    qb = q.reshape(B, nb, QUERY_BLOCK, H, dk).transpose(1, 0, 2, 3, 4)
    out = lax.map(lambda qi: attend(qi, k, v, scale), qb)
    return out.transpose(1, 0, 2, 3, 4).reshape(B, L, H, v.shape[-1])


def neighborhood_attention(q, k, v, kc, vc, rpb):
    B, L, H, d = q.shape
    R = L // GRID_W
    kr = min(NA_WIN_ROWS, R)
    rows = jnp.arange(R)
    r0 = jnp.clip(rows - kr // 2, 0, R - kr)
    row_idx = r0[:, None] + jnp.arange(kr)[None, :]
    cols = jnp.arange(GRID_W)
    c0 = jnp.clip(cols - NA_WIN_COLS // 2, 0, GRID_W - NA_WIN_COLS)
    col_in = (cols[None, :] >= c0[:, None]) & (cols[None, :] < c0[:, None] + NA_WIN_COLS)
    row_off = row_idx - rows[:, None] + (NA_WIN_ROWS - 1)
    col_off = jnp.clip(cols[None, :] - cols[:, None], -(NA_WIN_COLS - 1), NA_WIN_COLS - 1) + (NA_WIN_COLS - 1)
    bias = rpb[:, row_off[:, None, :, None], col_off[None, :, None, :]].astype(jnp.float32)
    bias = jnp.where(col_in[None, None, :, None, :], bias, NEG)
    qg = q.reshape(B, R, GRID_W, H, d)
    kg = k.reshape(B, R, GRID_W, H, d)[:, row_idx]
    vg = v.reshape(B, R, GRID_W, H, d)[:, row_idx]
    s_win = jnp.einsum('brqhd,brkchd->bhrqkc', qg, kg).astype(jnp.float32) * NA_SCALE + bias[None]
    s_ctx = jnp.einsum('brqhd,bkhd->bhrqk', qg, kc).astype(jnp.float32) * NA_SCALE
    nwin = kr * GRID_W
    s = jnp.concatenate([s_win.reshape(B, H, R, GRID_W, nwin), s_ctx], axis=-1)
    p = jax.nn.softmax(s, axis=-1).astype(v.dtype)
    p_win = p[..., :nwin].reshape(B, H, R, GRID_W, kr, GRID_W)
    o = jnp.einsum('bhrqkc,brkchd->brqhd', p_win, vg) + jnp.einsum('bhrqk,bkhd->brqhd', p[..., nwin:], vc)
    return o.reshape(B, L, H, d)


def multiscale_pool(u, w_pool, pool_scale):
    B, L, _ = u.shape
    ug = u.reshape(B, L, POOL_GROUPS, POOL_CH).astype(jnp.float32)
    cs = jnp.concatenate([jnp.zeros_like(ug[:, :1]), jnp.cumsum(ug, axis=1)], axis=1)
    t = jnp.arange(L)[:, None]
    half = jnp.array(POOL_WINDOWS, dtype=jnp.int32)[None, :] // 2
    lo = jnp.clip(t - half, 0, L)
    hi = jnp.clip(t + half, 0, L)
    gi = jnp.arange(POOL_GROUPS)[None, :]
    mean = (cs[:, hi, gi] - cs[:, lo, gi]) / (hi - lo).astype(jnp.float32)[None, :, :, None]
    y = (mean - ug).astype(u.dtype)
    y = jnp.einsum('blgc,gcd->blgd', y, w_pool).reshape(B, L, POOL_W)
    return y * pool_scale


def fourier_mix(u):
    B, L, _ = u.shape
    ug = u.reshape(B, L, FFT_GROUPS, FFT_CH).astype(jnp.float32)
    y = jnp.fft.fft2(ug, axes=(1, 3), norm='ortho').real
    return y.reshape(B, L, FFT_W).astype(u.dtype)


def mla_qkv(cq, ckv, krope, q_norm, w_uq, kv_norm, w_ukv, rope):
    B, L, _ = cq.shape
    q = (rms_norm(cq, q_norm) @ w_uq).reshape(B, L, MLA_HEADS, MLA_NOPE + MLA_ROPE)
    kv = (rms_norm(ckv, kv_norm) @ w_ukv).reshape(B, L, MLA_HEADS, MLA_NOPE + MLA_V)
    q_nope, q_rope = q[..., :MLA_NOPE], q[..., MLA_NOPE:]
    k_nope, v = kv[..., :MLA_NOPE], kv[..., MLA_NOPE:]
    k_rope = krope[:, :, None, :]
    if rope is not None:
        q_rope = apply_rope(q_rope, *rope)
        k_rope = apply_rope(k_rope, *rope)
    q = jnp.concatenate([q_nope, q_rope], axis=-1)
    k = jnp.concatenate([k_nope, jnp.broadcast_to(k_rope, (B, L, MLA_HEADS, MLA_ROPE))], axis=-1)
    return q, k, v


def merge_branches(branches, gates, w_branch, w_out):
    B, L, _ = gates.shape
    y = jnp.stack(branches, axis=2)
    proj = jnp.einsum('blnw,nwd->blnd', y, w_branch)
    g = jax.nn.sigmoid(gates.astype(jnp.float32)).astype(proj.dtype).reshape(B, L, N_BRANCH, D_MODEL)
    return jnp.sum(g * proj, axis=2) @ w_out


def _heads(t):
    return t.reshape(t.shape[0], t.shape[1], NA_HEADS, HEAD_DIM)


def token_mixer(ux, uc, w_in, na_rpb, pool_w, pool_scale, q_norm, w_uq, kv_norm, w_ukv, w_branch, w_out, rope, with_ctx):
    B, L, _ = ux.shape
    Lc = uc.shape[1]
    qa_x, ka_x, va_x, up_x, uf_x, cq_x, ckv_x, kr_x, gt_x = jnp.split(ux @ w_in, IN_SPLITS, axis=-1)
    qa_c, ka_c, va_c, up_c, uf_c, cq_c, ckv_c, kr_c, gt_c = jnp.split(uc @ w_in, IN_SPLITS, axis=-1)
    ka_c, va_c = _heads(ka_c), _heads(va_c)
    qm_c, km_c, vm_c = mla_qkv(cq_c, ckv_c, kr_c, q_norm, w_uq, kv_norm, w_ukv, None)
    ya_x = neighborhood_attention(_heads(qa_x), _heads(ka_x), _heads(va_x), ka_c, va_c, na_rpb).reshape(B, L, NA_W)
    yb_x = multiscale_pool(up_x, pool_w, pool_scale)
    yc_x = fourier_mix(uf_x)
    qm_x, km_x, vm_x = mla_qkv(cq_x, ckv_x, kr_x, q_norm, w_uq, kv_norm, w_ukv, rope)
    yd_x = attend_blocked(qm_x, jnp.concatenate([km_x, km_c], axis=1),
                          jnp.concatenate([vm_x, vm_c], axis=1), MLA_SCALE).reshape(B, L, MLA_W)
    out_x = merge_branches((ya_x, yb_x, yc_x, yd_x), gt_x, w_branch, w_out)
    if not with_ctx:
        return out_x, None
    ya_c = attend(_heads(qa_c), ka_c, va_c, NA_SCALE).reshape(B, Lc, NA_W)
    yb_c = multiscale_pool(up_c, pool_w, pool_scale)
    yc_c = fourier_mix(uf_c)
    yd_c = attend(qm_c, km_c, vm_c, MLA_SCALE).reshape(B, Lc, MLA_W)
    out_c = merge_branches((ya_c, yb_c, yc_c, yd_c), gt_c, w_branch, w_out)
    return out_x, out_c


def expert_choice_ffn(h, w_router, w_gate, w_up, w_down):
    B, L, _ = h.shape
    cap = EC_CAPACITY * L // N_EXPERTS
    aff = jax.nn.softmax((h @ w_router).astype(jnp.float32), axis=-1)
    g, idx = lax.top_k(aff.transpose(0, 2, 1), cap)
    bidx = jnp.arange(B)[:, None, None]
    xe = h[bidx, idx]
    a = jnp.einsum('becd,edf->becf', xe, w_gate)
    u = jnp.einsum('becd,edf->becf', xe, w_up)
    ye = jnp.einsum('becf,efd->becd', jax.nn.silu(a) * u, w_down) * g[..., None].astype(h.dtype)
    return jnp.zeros_like(h).at[bidx, idx].add(ye)


def setup_inputs(seed: int = 0) -> dict:
    key = jax.random.key(seed)
    ks = jax.random.split(key, 26)
    f32 = jnp.float32

    def nrm(k, shape, s):
        return jax.random.normal(k, shape, f32) * s

    D = D_MODEL
    return {
        'x': nrm(ks[0], (BATCH, SEQ, D), 1.0),
        'c': nrm(ks[1], (BATCH, D), 1.0),
        'ctx': nrm(ks[2], (BATCH, CTX_LEN, D), 1.0),
        'c_ctx': nrm(ks[3], (D,), 1.0),
        'w_mod': nrm(ks[4], (DEPTH, D, 6 * D), 0.3 * D ** -0.5),
        'b_mod': nrm(ks[5], (DEPTH, 6 * D), 0.02),
        'w_in': nrm(ks[6], (DEPTH, D, IN_COLS), D ** -0.5),
        'na_rpb': nrm(ks[7], (DEPTH, NA_HEADS, 2 * NA_WIN_ROWS - 1, 2 * NA_WIN_COLS - 1), 0.1),
        'pool_w': nrm(ks[8], (DEPTH, POOL_GROUPS, POOL_CH, POOL_CH), POOL_CH ** -0.5),
        'pool_scale': 1.0 + nrm(ks[9], (DEPTH, POOL_W), 0.1),
        'mla_q_norm': 1.0 + nrm(ks[10], (DEPTH, MLA_Q_RANK), 0.01),
        'mla_w_uq': nrm(ks[11], (DEPTH, MLA_Q_RANK, MLA_HEADS * (MLA_NOPE + MLA_ROPE)), MLA_Q_RANK ** -0.5),
        'mla_kv_norm': 1.0 + nrm(ks[12], (DEPTH, MLA_KV_RANK), 0.01),
        'mla_w_ukv': nrm(ks[13], (DEPTH, MLA_KV_RANK, MLA_HEADS * (MLA_NOPE + MLA_V)), MLA_KV_RANK ** -0.5),
        'w_branch': nrm(ks[14], (DEPTH, N_BRANCH, BRANCH_W, D), BRANCH_W ** -0.5),
        'w_out': nrm(ks[15], (DEPTH, D, D), DN_BETA * D ** -0.5),
        'ln1_g': 1.0 + nrm(ks[16], (DEPTH, D), 0.01),
        'ln1_b': nrm(ks[17], (DEPTH, D), 0.01),
        'w_router': nrm(ks[18], (DEPTH, D, N_EXPERTS), D ** -0.5),
        'w_gate': nrm(ks[19], (DEPTH, N_EXPERTS, D, EXPERT_FF), D ** -0.5),
        'w_up': nrm(ks[20], (DEPTH, N_EXPERTS, D, EXPERT_FF), D ** -0.5),
        'w_down': nrm(ks[21], (DEPTH, N_EXPERTS, EXPERT_FF, D), DN_BETA * EXPERT_FF ** -0.5),
        'ln2_g': 1.0 + nrm(ks[22], (DEPTH, D), 0.01),
        'ln2_b': nrm(ks[23], (DEPTH, D), 0.01),
    }


def reference(x, c, ctx, c_ctx, w_mod, b_mod, w_in, na_rpb, pool_w, pool_scale, mla_q_norm, mla_w_uq,
              mla_kv_norm, mla_w_ukv, w_branch, w_out, ln1_g, ln1_b, w_router, w_gate, w_up, w_down,
              ln2_g, ln2_b):
    rope = axial_rope(x.shape[1])
    h, hc = x, ctx
    silu_c = jax.nn.silu(c)
    silu_cc = jax.nn.silu(c_ctx)
    for l in range(DEPTH):
        with_ctx = l < DEPTH - 1
        mod_x = (silu_c @ w_mod[l] + b_mod[l])[:, None, :]
        mod_c = silu_cc @ w_mod[l] + b_mod[l]
        sh1, sc1, g1, sh2, sc2, g2 = jnp.split(mod_x, 6, axis=-1)
        sh1c, sc1c, g1c, sh2c, sc2c, g2c = jnp.split(mod_c, 6, axis=-1)
        mx, mc = token_mixer(h * (1.0 + sc1) + sh1, hc * (1.0 + sc1c) + sh1c, w_in[l], na_rpb[l],
                             pool_w[l], pool_scale[l], mla_q_norm[l], mla_w_uq[l], mla_kv_norm[l],
                             mla_w_ukv[l], w_branch[l], w_out[l], rope, with_ctx)
        h = layer_norm(DN_ALPHA * h + g1 * mx, ln1_g[l], ln1_b[l])
        fx = expert_choice_ffn(h * (1.0 + sc2) + sh2, w_router[l], w_gate[l], w_up[l], w_down[l])
        h = layer_norm(DN_ALPHA * h + g2 * fx, ln2_g[l], ln2_b[l])
        if with_ctx:
            hc = layer_norm(DN_ALPHA * hc + g1c * mc, ln1_g[l], ln1_b[l])
            fc = expert_choice_ffn(hc * (1.0 + sc2c) + sh2c, w_router[l], w_gate[l], w_up[l], w_down[l])
            hc = layer_norm(DN_ALPHA * hc + g2c * fc, ln2_g[l], ln2_b[l])
    return h
```

```python
import functools

import numpy as np
import jax
import jax.numpy as jnp
from jax import lax
from jax.experimental import pallas as pl
from jax.experimental.pallas import tpu as pltpu

F32 = jnp.float32
BF16 = jnp.bfloat16

D_MODEL = 1024
DEPTH = 4
GRID_W = 64
SEQ = 2048
CTX_LEN = 256
T_ALL = SEQ + CTX_LEN
HEAD_DIM = 64
NA_HEADS = 4
NA_WIN_ROWS = 8
NA_WIN_COLS = 16
NA_SCALE = HEAD_DIM ** -0.5
POOL_GROUPS = 4
POOL_CH = 64
POOL_WINDOWS = (2, 4, 8, 16)
FFT_GROUPS = 4
FFT_CH = 64
MLA_HEADS = 4
MLA_Q_RANK = 256
MLA_KV_RANK = 128
MLA_NOPE = 64
MLA_ROPE = 32
MLA_V = 64
MLA_SCALE = (MLA_NOPE + MLA_ROPE) ** -0.5
ROPE_BASE = 10000.0
BRANCH_W = 256
N_BRANCH = 4
N_EXPERTS = 16
EXPERT_FF = 2048
EC_CAPACITY = 2
DN_ALPHA = (2 * DEPTH) ** 0.25
LN_EPS = 1e-5
NEG = -1e30

LANE = 128
TM = 256
N_TILES = T_ALL // TM
N_LAT_TILES = SEQ // TM
MLA_HEAD_PAD = 128
NA_QROWS = TM // GRID_W
NA_KROWS = 12
NA_KWIN = NA_KROWS * GRID_W
NA_KALL = NA_KWIN + CTX_LEN
CAP_LAT = EC_CAPACITY * SEQ // N_EXPERTS
CAP_CTX = EC_CAPACITY * CTX_LEN // N_EXPERTS
CAP_ALL = CAP_LAT + CAP_CTX
ROUTER_PAD = 128
FF_TILE = 512
XROW_CHUNK = 576
VMEM_LIMIT = 56 * 1024 * 1024

C_QA, C_KA, C_VA, C_UP, C_UF, C_CQ, C_CKV, C_KR, C_KRS, C_END = 0, 256, 512, 768, 1024, 1280, 1536, 1664, 1792, 1920


def _cparams(sem):
    return pltpu.CompilerParams(dimension_semantics=sem, vmem_limit_bytes=VMEM_LIMIT)


def _dot(a, b):
    return jnp.dot(a, b, preferred_element_type=F32)


def _dot_t(a, b):
    return lax.dot_general(a, b, (((1,), (1,)), ((), ())), preferred_element_type=F32)


def _layer_norm(x, g, b):
    mu = jnp.mean(x, axis=-1, keepdims=True)
    xc = x - mu
    var = jnp.mean(xc * xc, axis=-1, keepdims=True)
    return xc * lax.rsqrt(var + LN_EPS) * g + b


def _mod_kernel(c_ref, w_ref, b_ref, o_ref):
    c = c_ref[...]
    s = c * jax.nn.sigmoid(c)
    w = w_ref[0]
    s_hi = s.astype(BF16)
    s_lo = (s - s_hi.astype(F32)).astype(BF16)
    w_hi = w.astype(BF16)
    w_lo = (w - w_hi.astype(F32)).astype(BF16)
    o_ref[0] = _dot(s_hi, w_hi) + _dot(s_hi, w_lo) + _dot(s_lo, w_hi) + b_ref[0]


def _modulation(cc, w_mod, b_mod):
    rows = cc.shape[0]
    n6 = 6 * D_MODEL
    tn = 1536
    return pl.pallas_call(
        _mod_kernel,
        grid=(DEPTH, n6 // tn),
        in_specs=[
            pl.BlockSpec((rows, D_MODEL), lambda l, j: (0, 0)),
            pl.BlockSpec((1, D_MODEL, tn), lambda l, j: (l, 0, j)),
            pl.BlockSpec((1, 1, tn), lambda l, j: (l, 0, j)),
        ],
        out_specs=pl.BlockSpec((1, rows, tn), lambda l, j: (l, 0, j)),
        out_shape=jax.ShapeDtypeStruct((DEPTH, rows, n6), F32),
        compiler_params=_cparams(("arbitrary", "arbitrary")),
    )(cc, w_mod, b_mod.reshape(DEPTH, 1, n6))


def _inproj_kernel(h_ref, mod_ref, wsm_ref, qn_ref, kvn_ref, wuq_ref, wukv_ref, cos_ref, sin_ref, bd_ref,
                   qa_ref, ka_ref, va_ref, up_ref, xcs_ref, qm_ref, km_ref, vm_ref):
    h = h_ref[0]
    sh1 = mod_ref[0, 0, 0:1, :]
    sc1 = mod_ref[0, 0, 1:2, :]
    u = (h * (1.0 + sc1) + sh1).astype(BF16)
    z = _dot(u, wsm_ref[...])
    qa_ref[0] = (z[:, C_QA:C_KA] * NA_SCALE).astype(BF16)
    ka_ref[0] = z[:, C_KA:C_VA].astype(BF16)
    va_ref[0] = z[:, C_VA:C_UP].astype(BF16)
    up_ref[0] = z[:, C_UP:C_UF]
    xcs_ref[0] = _dot(z[:, C_UF:C_CQ].astype(BF16), bd_ref[...]).astype(BF16)

    cos = cos_ref[...]
    sin = sin_ref[...]
    cos4 = jnp.concatenate([cos] * MLA_HEADS, axis=-1)
    sin4 = jnp.concatenate([sin] * MLA_HEADS, axis=-1)

    cq = z[:, C_CQ:C_CKV]
    nq = cq * lax.rsqrt(jnp.mean(cq * cq, axis=-1, keepdims=True) + LN_EPS) * qn_ref[...]
    q2 = _dot(nq.astype(BF16), wuq_ref[...])
    hw = MLA_HEADS * MLA_HEAD_PAD
    qm_ref[0] = (q2[:, :hw] * cos4 + q2[:, hw:] * sin4).astype(BF16)

    ckv = z[:, C_CKV:C_KR]
    nkv = ckv * lax.rsqrt(jnp.mean(ckv * ckv, axis=-1, keepdims=True) + LN_EPS) * kvn_ref[...]
    kv2 = _dot(nkv.astype(BF16), wukv_ref[...])
    kr = z[:, C_KR:C_KRS] * cos + z[:, C_KRS:C_END] * sin
    km_ref[0] = (kv2[:, :hw] + jnp.concatenate([kr] * MLA_HEADS, axis=-1)).astype(BF16)
    vm_ref[0] = kv2[:, hw:].astype(BF16)


def _inproj(hall, mod_all, wsm, qn, kvn, wuq2, wukv2, cos_t, sin_t, bd):
    B = hall.shape[0]
    hw = MLA_HEADS * MLA_HEAD_PAD
    const = lambda b, t: (0, 0)
    row = lambda b, t: (b, t, 0)
    outs = [
        (256, BF16), (256, BF16), (256, BF16), (256, F32), (512, BF16), (hw, BF16), (hw, BF16), (MLA_HEADS * MLA_V, BF16),
    ]
    return pl.pallas_call(
        _inproj_kernel,
        grid=(B, N_TILES),
        in_specs=[
            pl.BlockSpec((1, TM, D_MODEL), row),
            pl.BlockSpec((1, 1, 6, D_MODEL), lambda b, t: (b, t // N_LAT_TILES, 0, 0)),
            pl.BlockSpec(wsm.shape, const),
            pl.BlockSpec(qn.shape, const),
            pl.BlockSpec(kvn.shape, const),
            pl.BlockSpec(wuq2.shape, const),
            pl.BlockSpec(wukv2.shape, const),
            pl.BlockSpec((TM, LANE), lambda b, t: (t, 0)),
            pl.BlockSpec((TM, LANE), lambda b, t: (t, 0)),
            pl.BlockSpec(bd.shape, const),
        ],
        out_specs=[pl.BlockSpec((1, TM, w), row) for w, _ in outs],
        out_shape=[jax.ShapeDtypeStruct((B, T_ALL, w), dt) for w, dt in outs],
        compiler_params=_cparams(("arbitrary", "arbitrary")),
    )(hall, mod_all, wsm, qn, kvn, wuq2, wukv2, cos_t, sin_t, bd)


def _softmax_pv(s, v):
    m = jnp.max(s, axis=-1, keepdims=True)
    p = jnp.exp(s - m)
    l = jnp.sum(p, axis=-1, keepdims=True)
    return _dot(p.astype(BF16), v) / l


def _na_kernel(q_ref, k_ref, v_ref, bias_ref, o_ref):
    t = pl.program_id(0)

    @pl.when(t < N_LAT_TILES)
    def _():
        ks = pl.multiple_of(jnp.clip(t * NA_QROWS - NA_WIN_ROWS // 2, 0, SEQ // GRID_W - NA_KROWS) * GRID_W, GRID_W)
        kk = jnp.concatenate([k_ref[0, pl.ds(ks, NA_KWIN), :], k_ref[0, SEQ:T_ALL, :]], axis=0)
        vv = jnp.concatenate([v_ref[0, pl.ds(ks, NA_KWIN), :], v_ref[0, SEQ:T_ALL, :]], axis=0)
        q = q_ref[0]
        for h in range(NA_HEADS):
            sl = slice(h * HEAD_DIM, (h + 1) * HEAD_DIM)
            s = _dot_t(q[:, sl], kk[:, sl]) + bias_ref[0, h]
            o_ref[0, :, sl] = _softmax_pv(s, vv[:, sl]).astype(o_ref.dtype)

    @pl.when(t == N_LAT_TILES)
    def _():
        kk = k_ref[0, SEQ:T_ALL, :]
        vv = v_ref[0, SEQ:T_ALL, :]
        q = q_ref[0]
        for h in range(NA_HEADS):
            sl = slice(h * HEAD_DIM, (h + 1) * HEAD_DIM)
            o_ref[0, :, sl] = _softmax_pv(_dot_t(q[:, sl], kk[:, sl]), vv[:, sl]).astype(o_ref.dtype)


def _na_attention(qa, ka, va, bias):
    B = qa.shape[0]

    def bias_idx(t, b):
        return (jnp.where(t == 0, 0, jnp.where(t >= N_LAT_TILES - 1, 2, 1)), 0, 0, 0)

    return pl.pallas_call(
        _na_kernel,
        grid=(N_TILES, B),
        in_specs=[
            pl.BlockSpec((1, TM, 256), lambda t, b: (b, t, 0)),
            pl.BlockSpec((1, T_ALL, 256), lambda t, b: (b, 0, 0)),
            pl.BlockSpec((1, T_ALL, 256), lambda t, b: (b, 0, 0)),
            pl.BlockSpec((1, NA_HEADS, TM, NA_KALL), bias_idx),
        ],
        out_specs=pl.BlockSpec((1, TM, 256), lambda t, b: (b, t, 0)),
        out_shape=jax.ShapeDtypeStruct((B, T_ALL, 256), BF16),
        compiler_params=_cparams(("arbitrary", "arbitrary")),
    )(qa, ka, va, bias)


def _mla_kernel(q_ref, k_ref, v_ref, o_ref):
    t = pl.program_id(1)

    def run(k_all, v_all):
        q = q_ref[0]
        for h in range(MLA_HEADS):
            ksl = slice(h * MLA_HEAD_PAD, (h + 1) * MLA_HEAD_PAD)
            vsl = slice(h * MLA_V, (h + 1) * MLA_V)
            s = _dot_t(q[:, ksl], k_all[:, ksl]) * MLA_SCALE
            o_ref[0, :, vsl] = _softmax_pv(s, v_all[:, vsl]).astype(o_ref.dtype)

    @pl.when(t < N_LAT_TILES)
    def _():
        run(k_ref[0], v_ref[0])

    @pl.when(t == N_LAT_TILES)
    def _():
        run(k_ref[0, SEQ:T_ALL, :], v_ref[0, SEQ:T_ALL, :])


def _mla_attention(qm, km, vm):
    B = qm.shape[0]
    hw = MLA_HEADS * MLA_HEAD_PAD
    vw = MLA_HEADS * MLA_V
    return pl.pallas_call(
        _mla_kernel,
        grid=(B, N_TILES),
        in_specs=[
            pl.BlockSpec((1, TM, hw), lambda b, t: (b, t, 0)),
            pl.BlockSpec((1, T_ALL, hw), lambda b, t: (b, 0, 0)),
            pl.BlockSpec((1, T_ALL, vw), lambda b, t: (b, 0, 0)),
        ],
        out_specs=pl.BlockSpec((1, TM, vw), lambda b, t: (b, t, 0)),
        out_shape=jax.ShapeDtypeStruct((B, T_ALL, vw), BF16),
        compiler_params=_cparams(("arbitrary", "arbitrary")),
    )(qm, km, vm)


def _pool_kernel(u_ref, w_ref, scale_ref, o_ref):
    x = u_ref[0]
    t = lax.broadcasted_iota(jnp.int32, (T_ALL, 1), 0)
    lane = lax.broadcasted_iota(jnp.int32, (1, POOL_GROUPS * POOL_CH), 1)
    half = jnp.left_shift(1, lane // POOL_CH)
    seg_lo = jnp.where(t < SEQ, 0, SEQ)
    seg_hi = jnp.where(t < SEQ, SEQ, T_ALL)
    max_half = POOL_WINDOWS[-1] // 2
    acc = jnp.zeros_like(x)
    for d in range(-max_half, max_half):
        xs = x if d == 0 else pltpu.roll(x, (-d) % T_ALL, 0)
        td = t + d
        ok = (td >= seg_lo) & (td < seg_hi) & (half >= -d) & (half > d)
        acc = acc + jnp.where(ok, xs, 0.0)
    cnt = jnp.minimum(t + half, seg_hi) - jnp.maximum(t - half, seg_lo)
    y = (acc / cnt.astype(F32) - x).astype(BF16)
    o_ref[0] = (_dot(y, w_ref[...]) * scale_ref[...]).astype(o_ref.dtype)


def _pool(up, w_bd, scale):
    B = up.shape[0]
    return pl.pallas_call(
        _pool_kernel,
        grid=(B,),
        in_specs=[
            pl.BlockSpec((1, T_ALL, 256), lambda b: (b, 0, 0)),
            pl.BlockSpec((256, 256), lambda b: (0, 0)),
            pl.BlockSpec((1, 256), lambda b: (0, 0)),
        ],
        out_specs=pl.BlockSpec((1, T_ALL, 256), lambda b: (b, 0, 0)),
        out_shape=jax.ShapeDtypeStruct((B, T_ALL, 256), BF16),
        compiler_params=_cparams(("arbitrary",)),
    )(up, w_bd, scale)


DFT_ROWS = 768


def _dft_kernel(c_ref, s_ref, x_ref, o_ref):
    x = x_ref[0]
    w = FFT_GROUPS * FFT_CH
    o_ref[0] = (_dot(c_ref[...], x[:, :w]) + _dot(s_ref[...], x[:, w:])).astype(o_ref.dtype)


def _dft_positions(cmat, smat, xcs):
    B = xcs.shape[0]
    return pl.pallas_call(
        _dft_kernel,
        grid=(T_ALL // DFT_ROWS, B),
        in_specs=[
            pl.BlockSpec((DFT_ROWS, T_ALL), lambda i, b: (i, 0)),
            pl.BlockSpec((DFT_ROWS, T_ALL), lambda i, b: (i, 0)),
            pl.BlockSpec((1, T_ALL, 512), lambda i, b: (b, 0, 0)),
        ],
        out_specs=pl.BlockSpec((1, DFT_ROWS, 256), lambda i, b: (b, i, 0)),
        out_shape=jax.ShapeDtypeStruct((B, T_ALL, 256), BF16),
        compiler_params=_cparams(("arbitrary", "arbitrary")),
    )(cmat, smat, xcs)


def _merge_kernel(h_ref, mod_ref, ya_ref, yb_ref, yc_ref, yd_ref, wg_ref, wbr_ref, wout_ref, g_ref, b_ref, wr_ref,
                  h1_ref, u2_ref, aff_ref):
    h = h_ref[0]
    sh1 = mod_ref[0, 0, 0:1, :]
    sc1 = mod_ref[0, 0, 1:2, :]
    g1 = mod_ref[0, 0, 2:3, :]
    sh2 = mod_ref[0, 0, 3:4, :]
    sc2 = mod_ref[0, 0, 4:5, :]
    u = (h * (1.0 + sc1) + sh1).astype(BF16)
    ys = (ya_ref, yb_ref, yc_ref, yd_ref)
    acc = jnp.zeros((TM, D_MODEL), F32)
    for n in range(N_BRANCH):
        gate = jax.nn.sigmoid(_dot(u, wg_ref[:, n * D_MODEL:(n + 1) * D_MODEL]))
        acc = acc + gate * _dot(ys[n][0], wbr_ref[n])
    mix = _dot(acc.astype(BF16), wout_ref[...])
    h1 = _layer_norm(DN_ALPHA * h + g1 * mix, g_ref[...], b_ref[...])
    h1_ref[0] = h1
    u2 = h1 * (1.0 + sc2) + sh2
    u2_hi = u2.astype(BF16)
    u2_ref[0] = u2_hi
    u2_lo = (u2 - u2_hi.astype(F32)).astype(BF16)
    logits = _dot(u2_hi, wr_ref[0]) + _dot(u2_hi, wr_ref[1]) + _dot(u2_lo, wr_ref[0])
    lane = lax.broadcasted_iota(jnp.int32, (1, ROUTER_PAD), 1)
    logits = jnp.where(lane < N_EXPERTS, logits, NEG)
    e = jnp.exp(logits - jnp.max(logits, axis=-1, keepdims=True))
    aff_ref[0] = e / jnp.sum(e, axis=-1, keepdims=True)


def _merge(hall, mod_all, ya, yb, yc, yd, wg, wbr, wout, ln_g, ln_b, wr2):
    B = hall.shape[0]
    row = lambda b, t: (b, t, 0)
    return pl.pallas_call(
        _merge_kernel,
        grid=(B, N_TILES),
        in_specs=[
            pl.BlockSpec((1, TM, D_MODEL), row),
            pl.BlockSpec((1, 1, 6, D_MODEL), lambda b, t: (b, t // N_LAT_TILES, 0, 0)),
            pl.BlockSpec((1, TM, 256), row),
            pl.BlockSpec((1, TM, 256), row),
            pl.BlockSpec((1, TM, 256), row),
            pl.BlockSpec((1, TM, 256), row),
            pl.BlockSpec(wg.shape, lambda b, t: (0, 0)),
            pl.BlockSpec(wbr.shape, lambda b, t: (0, 0, 0)),
            pl.BlockSpec(wout.shape, lambda b, t: (0, 0)),
            pl.BlockSpec((1, D_MODEL), lambda b, t: (0, 0)),
            pl.BlockSpec((1, D_MODEL), lambda b, t: (0, 0)),
            pl.BlockSpec(wr2.shape, lambda b, t: (0, 0, 0)),
        ],
        out_specs=[
            pl.BlockSpec((1, TM, D_MODEL), row),
            pl.BlockSpec((1, TM, D_MODEL), row),
            pl.BlockSpec((1, TM, ROUTER_PAD), row),
        ],
        out_shape=[
            jax.ShapeDtypeStruct((B, T_ALL, D_MODEL), F32),
            jax.ShapeDtypeStruct((B, T_ALL, D_MODEL), BF16),
            jax.ShapeDtypeStruct((B, T_ALL, ROUTER_PAD), F32),
        ],
        compiler_params=_cparams(("arbitrary", "arbitrary")),
    )(hall, mod_all, ya, yb, yc, yd, wg, wbr, wout, ln_g, ln_b, wr2)


def _gather_kernel(idx_ref, u_ref, o_ref):
    idx = idx_ref[0, 0]
    tok = lax.broadcasted_iota(jnp.int32, (CAP_ALL, T_ALL), 1)
    onehot = jnp.where(tok == idx, 1.0, 0.0).astype(BF16)
    o_ref[0] = _dot(onehot, u_ref[0]).astype(o_ref.dtype)


def _gather_tokens(idx_col, u2):
    B = u2.shape[0]
    return pl.pallas_call(
        _gather_kernel,
        grid=(B, N_EXPERTS),
        in_specs=[
            pl.BlockSpec((1, 1, CAP_ALL, 1), lambda b, e: (b, e, 0, 0)),
            pl.BlockSpec((1, T_ALL, D_MODEL), lambda b, e: (b, 0, 0)),
        ],
        out_specs=pl.BlockSpec((1, CAP_ALL, D_MODEL), lambda b, e: (e, b, 0)),
        out_shape=jax.ShapeDtypeStruct((N_EXPERTS, B * CAP_ALL, D_MODEL), BF16),
        compiler_params=_cparams(("arbitrary", "arbitrary")),
    )(idx_col, u2)


def _expert_kernel(x_ref, g_ref, wg_ref, wu_ref, wd_ref, o_ref, acc_ref, *, chunk):
    f = pl.program_id(1)
    wg = wg_ref[0, 0].astype(BF16)
    wu = wu_ref[0, 0].astype(BF16)
    wd = wd_ref[0, 0].astype(BF16)
    for c in range(x_ref.shape[1] // chunk):
        rows = pl.ds(c * chunk, chunk)
        x = x_ref[0, rows, :]
        a = _dot(x, wg)
        u = _dot(x, wu)
        y = _dot((a * jax.nn.sigmoid(a) * u).astype(BF16), wd)

        @pl.when(f == 0)
        def _():
            acc_ref[rows, :] = y

        @pl.when(f > 0)
        def _():
            acc_ref[rows, :] += y

    @pl.when(f == pl.num_programs(1) - 1)
    def _():
        o_ref[0] = (acc_ref[...] * g_ref[0]).astype(o_ref.dtype)


def _experts(xe, gcol, w_gate, w_up, w_down, layer):
    rows = xe.shape[1]
    chunk = XROW_CHUNK if rows % XROW_CHUNK == 0 else CAP_ALL
    return pl.pallas_call(
        functools.partial(_expert_kernel, chunk=chunk),
        grid=(N_EXPERTS, EXPERT_FF // FF_TILE),
        in_specs=[
            pl.BlockSpec((1, rows, D_MODEL), lambda e, f: (e, 0, 0)),
            pl.BlockSpec((1, rows, 1), lambda e, f: (e, 0, 0)),
            pl.BlockSpec((1, 1, D_MODEL, FF_TILE), lambda e, f: (layer, e, 0, f)),
            pl.BlockSpec((1, 1, D_MODEL, FF_TILE), lambda e, f: (layer, e, 0, f)),
            pl.BlockSpec((1, 1, FF_TILE, D_MODEL), lambda e, f: (layer, e, f, 0)),
        ],
        out_specs=pl.BlockSpec((1, rows, D_MODEL), lambda e, f: (e, 0, 0)),
        out_shape=jax.ShapeDtypeStruct((N_EXPERTS, rows, D_MODEL), BF16),
        scratch_shapes=[pltpu.VMEM((rows, D_MODEL), F32)],
        compiler_params=_cparams(("arbitrary", "arbitrary")),
    )(xe, gcol, w_gate, w_up, w_down)


def _scatter_kernel(h_ref, mod_ref, idx_ref, y_ref, g_ref, b_ref, o_ref):
    t = pl.program_id(1)
    idx = idx_ref[0]
    tok = lax.broadcasted_iota(jnp.int32, (TM, N_EXPERTS * CAP_ALL), 0) + t * TM
    onehot = jnp.where(tok == idx, 1.0, 0.0).astype(BF16)
    y = y_ref[...].reshape(N_EXPERTS * CAP_ALL, D_MODEL)
    fx = _dot(onehot, y)
    g2 = mod_ref[0, 0, 5:6, :]
    o_ref[0] = _layer_norm(DN_ALPHA * h_ref[0] + g2 * fx, g_ref[...], b_ref[...])


def _scatter_ln(h1, mod_all, idx_row, ye, ln_g, ln_b):
    B = h1.shape[0]
    return pl.pallas_call(
        _scatter_kernel,
        grid=(B, N_TILES),
        in_specs=[
            pl.BlockSpec((1, TM, D_MODEL), lambda b, t: (b, t, 0)),
            pl.BlockSpec((1, 1, 6, D_MODEL), lambda b, t: (b, t // N_LAT_TILES, 0, 0)),
            pl.BlockSpec((1, 1, N_EXPERTS * CAP_ALL), lambda b, t: (b, 0, 0)),
            pl.BlockSpec((N_EXPERTS, 1, CAP_ALL, D_MODEL), lambda b, t: (0, b, 0, 0)),
            pl.BlockSpec((1, D_MODEL), lambda b, t: (0, 0)),
            pl.BlockSpec((1, D_MODEL), lambda b, t: (0, 0)),
        ],
        out_specs=pl.BlockSpec((1, TM, D_MODEL), lambda b, t: (b, t, 0)),
        out_shape=jax.ShapeDtypeStruct((B, T_ALL, D_MODEL), F32),
        compiler_params=_cparams(("arbitrary", "arbitrary")),
    )(h1, mod_all, idx_row, ye, ln_g, ln_b)


def _rope_tables():
    n_freq = MLA_ROPE // 4
    inv = ROPE_BASE ** (-jnp.arange(n_freq, dtype=F32) / n_freq)
    t = jnp.arange(SEQ)
    row = (t // GRID_W).astype(F32)
    col = (t % GRID_W).astype(F32)
    ang = jnp.concatenate([row[:, None] * inv, col[:, None] * inv], axis=-1)
    cos, sin = jnp.cos(ang), jnp.sin(ang)
    ones = jnp.ones((SEQ, MLA_NOPE), F32)
    pad1 = jnp.ones((SEQ, MLA_HEAD_PAD - MLA_NOPE - MLA_ROPE), F32)
    cos_t = jnp.concatenate([ones, cos, cos, pad1], axis=-1)
    sin_t = jnp.concatenate([0 * ones, -sin, sin, 0 * pad1], axis=-1)
    cos_t = jnp.concatenate([cos_t, jnp.ones((CTX_LEN, MLA_HEAD_PAD), F32)], axis=0)
    sin_t = jnp.concatenate([sin_t, jnp.zeros((CTX_LEN, MLA_HEAD_PAD), F32)], axis=0)
    return cos_t, sin_t


def _dft_tables():
    def block(n):
        j = jnp.arange(n, dtype=jnp.int32)
        ang = ((j[:, None] * j[None, :]) % n).astype(F32) * (2.0 * np.pi / n)
        sc = 1.0 / np.sqrt(n * FFT_CH)
        return jnp.cos(ang) * sc, -jnp.sin(ang) * sc

    cl, sl = block(SEQ)
    cc, sc_ = block(CTX_LEN)

    def diag(a, b):
        top = jnp.concatenate([a, jnp.zeros((SEQ, CTX_LEN), F32)], axis=1)
        bot = jnp.concatenate([jnp.zeros((CTX_LEN, SEQ), F32), b], axis=1)
        return jnp.concatenate([top, bot], axis=0).astype(BF16)

    j = jnp.arange(FFT_CH, dtype=jnp.int32)
    ang = ((j[:, None] * j[None, :]) % FFT_CH).astype(F32) * (2.0 * np.pi / FFT_CH)
    eye = jnp.eye(FFT_GROUPS, dtype=F32)
    bd = jnp.concatenate([jnp.kron(eye, jnp.cos(ang)), jnp.kron(eye, jnp.sin(ang))], axis=1).astype(BF16)
    return diag(cl, cc), diag(sl, sc_), bd


def _na_bias_index():
    R = SEQ // GRID_W
    pats = [(0, 0), (NA_QROWS, 0), (R - NA_QROWS, R - NA_KROWS)]
    q = np.arange(TM)
    k = np.arange(NA_KWIN)
    qi, qc = q // GRID_W, q % GRID_W
    kj, kc = k // GRID_W, k % GRID_W
    c0 = np.clip(qc - NA_WIN_COLS // 2, 0, GRID_W - NA_WIN_COLS)
    col_in = (kc[None, :] >= c0[:, None]) & (kc[None, :] < c0[:, None] + NA_WIN_COLS)
    col_off = np.clip(kc[None, :] - qc[:, None], -(NA_WIN_COLS - 1), NA_WIN_COLS - 1) + (NA_WIN_COLS - 1)
    row_offs, valids = [], []
    for rb, ks in pats:
        qr = rb + qi
        kr = ks + kj
        r0 = np.clip(qr - NA_WIN_ROWS // 2, 0, R - NA_WIN_ROWS)
        row_in = (kr[None, :] >= r0[:, None]) & (kr[None, :] < r0[:, None] + NA_WIN_ROWS)
        row_offs.append(np.clip(kr[None, :] - qr[:, None] + (NA_WIN_ROWS - 1), 0, 2 * NA_WIN_ROWS - 2))
        valids.append(row_in & col_in)
    flat = np.stack(row_offs) * (2 * NA_WIN_COLS - 1) + col_off[None]
    return flat.astype(np.int32), np.stack(valids)


def _na_bias(rpb):
    flat, valid = _na_bias_index()
    tab = rpb.reshape(NA_HEADS, -1)
    b = jnp.take(tab, jnp.asarray(flat.reshape(-1)), axis=1).reshape(NA_HEADS, 3, TM, NA_KWIN)
    b = jnp.where(jnp.asarray(valid)[None], b, NEG).transpose(1, 0, 2, 3)
    return jnp.concatenate([b, jnp.zeros((3, NA_HEADS, TM, CTX_LEN), F32)], axis=-1)


def _rearranged_in_weights(w_in):
    z = lambda n: jnp.zeros((D_MODEL, n), F32)
    kr = w_in[:, 1664:1696]
    kr_e, kr_o = kr[:, 0::2], kr[:, 1::2]
    pad = MLA_HEAD_PAD - MLA_NOPE - MLA_ROPE
    krb = jnp.concatenate([z(MLA_NOPE), kr_e, kr_o, z(pad)], axis=1)
    krs = jnp.concatenate([z(MLA_NOPE), kr_o, kr_e, z(pad)], axis=1)
    wsm = jnp.concatenate([w_in[:, :1664], krb, krs], axis=1).astype(BF16)
    return wsm, w_in[:, 1696:].astype(BF16)


def _rearranged_mla_weights(w_uq, w_ukv):
    pad = MLA_HEAD_PAD - MLA_NOPE - MLA_ROPE
    wq = w_uq.reshape(MLA_Q_RANK, MLA_HEADS, MLA_NOPE + MLA_ROPE)
    nope, rope = wq[..., :MLA_NOPE], wq[..., MLA_NOPE:]
    r_e, r_o = rope[..., 0::2], rope[..., 1::2]
    zq = jnp.zeros((MLA_Q_RANK, MLA_HEADS, pad), F32)
    q_main = jnp.concatenate([nope, r_e, r_o, zq], axis=-1).reshape(MLA_Q_RANK, -1)
    q_swap = jnp.concatenate([nope, r_o, r_e, zq], axis=-1).reshape(MLA_Q_RANK, -1)
    wuq2 = jnp.concatenate([q_main, q_swap], axis=1).astype(BF16)
    wkv = w_ukv.reshape(MLA_KV_RANK, MLA_HEADS, MLA_NOPE + MLA_V)
    k_nope, v = wkv[..., :MLA_NOPE], wkv[..., MLA_NOPE:]
    zk = jnp.zeros((MLA_KV_RANK, MLA_HEADS, MLA_HEAD_PAD - MLA_NOPE), F32)
    k_main = jnp.concatenate([k_nope, zk], axis=-1).reshape(MLA_KV_RANK, -1)
    wukv2 = jnp.concatenate([k_main, v.reshape(MLA_KV_RANK, -1)], axis=1).astype(BF16)
    return wuq2, wukv2


def _route(aff):
    a = aff.transpose(0, 2, 1)
    g_l, i_l = lax.top_k(a[:, :, :SEQ], CAP_LAT)
    g_c, i_c = lax.top_k(a[:, :, SEQ:], CAP_CTX)
    return jnp.concatenate([i_l, i_c + SEQ], axis=-1), jnp.concatenate([g_l, g_c], axis=-1)


def kernel(x, c, ctx, c_ctx, w_mod, b_mod, w_in, na_rpb, pool_w, pool_scale, mla_q_norm, mla_w_uq, mla_kv_norm,
           mla_w_ukv, w_branch, w_out, ln1_g, ln1_b, w_router, w_gate, w_up, w_down, ln2_g, ln2_b):
    B = x.shape[0]
    assert x.shape == (B, SEQ, D_MODEL) and ctx.shape == (B, CTX_LEN, D_MODEL) and B + 1 <= 16

    cc = jnp.concatenate([c, c_ctx[None], jnp.zeros((16 - B - 1, D_MODEL), F32)], axis=0)
    mod = _modulation(cc, w_mod, b_mod).reshape(DEPTH, 16, 6, D_MODEL)
    cos_t, sin_t = _rope_tables()
    cmat, smat, bd = _dft_tables()
    hall = jnp.concatenate([x, ctx], axis=1)

    for l in range(DEPTH):
        mod_all = jnp.stack([mod[l, :B], jnp.broadcast_to(mod[l, B], (B, 6, D_MODEL))], axis=1)
        wsm, wgates = _rearranged_in_weights(w_in[l])
        wuq2, wukv2 = _rearranged_mla_weights(mla_w_uq[l], mla_w_ukv[l])
        qa, ka, va, up, xcs, qm, km, vm = _inproj(
            hall, mod_all, wsm, mla_q_norm[l][None], mla_kv_norm[l][None], wuq2, wukv2, cos_t, sin_t, bd)
        ya = _na_attention(qa, ka, va, _na_bias(na_rpb[l]))
        pool_bd = jax.scipy.linalg.block_diag(*[pool_w[l, g] for g in range(POOL_GROUPS)]).astype(BF16)
        yb = _pool(up, pool_bd, pool_scale[l][None])
        yc = _dft_positions(cmat, smat, xcs)
        yd = _mla_attention(qm, km, vm)
        wr = jnp.pad(w_router[l], ((0, 0), (0, ROUTER_PAD - N_EXPERTS)))
        wr_hi = wr.astype(BF16)
        wr2 = jnp.stack([wr_hi, (wr - wr_hi.astype(F32)).astype(BF16)])
        h1, u2, aff = _merge(hall, mod_all, ya, yb, yc, yd, wgates, w_branch[l].astype(BF16), w_out[l].astype(BF16),
                             ln1_g[l][None], ln1_b[l][None], wr2)
        idx, gate = _route(aff[:, :, :N_EXPERTS])
        xe = _gather_tokens(idx[..., None], u2)
        gcol = gate.transpose(1, 0, 2).reshape(N_EXPERTS, B * CAP_ALL, 1)
        ye = _experts(xe, gcol, w_gate, w_up, w_down, l)
        hall = _scatter_ln(h1, mod_all, idx.reshape(B, 1, N_EXPERTS * CAP_ALL),
                           ye.reshape(N_EXPERTS, B, CAP_ALL, D_MODEL), ln2_g[l][None], ln2_b[l][None])
    return hall[:, :SEQ]
```

```python
import functools

import numpy as np
import jax
import jax.numpy as jnp
from jax import lax
from jax.experimental import pallas as pl
from jax.experimental.pallas import tpu as pltpu

F32 = jnp.float32
BF16 = jnp.bfloat16

D_MODEL = 1024
DEPTH = 4
GRID_W = 64
SEQ = 2048
CTX_LEN = 256
T_ALL = SEQ + CTX_LEN
HEAD_DIM = 64
NA_HEADS = 4
NA_WIN_ROWS = 8
NA_WIN_COLS = 16
NA_SCALE = HEAD_DIM ** -0.5
POOL_GROUPS = 4
POOL_CH = 64
POOL_WINDOWS = (2, 4, 8, 16)
FFT_GROUPS = 4
FFT_CH = 64
MLA_HEADS = 4
MLA_Q_RANK = 256
MLA_KV_RANK = 128
MLA_NOPE = 64
MLA_ROPE = 32
MLA_V = 64
MLA_SCALE = (MLA_NOPE + MLA_ROPE) ** -0.5
ROPE_BASE = 10000.0
BRANCH_W = 256
N_BRANCH = 4
N_EXPERTS = 16
EXPERT_FF = 2048
EC_CAPACITY = 2
DN_ALPHA = (2 * DEPTH) ** 0.25
LN_EPS = 1e-5
NEG = -1e30

LANE = 128
TM = 256
N_TILES = T_ALL // TM
N_LAT_TILES = SEQ // TM
MLA_HEAD_PAD = 128
NA_QROWS = TM // GRID_W
NA_KROWS = 12
NA_KWIN = NA_KROWS * GRID_W
NA_KALL = NA_KWIN + CTX_LEN
CAP_LAT = EC_CAPACITY * SEQ // N_EXPERTS
CAP_CTX = EC_CAPACITY * CTX_LEN // N_EXPERTS
CAP_ALL = CAP_LAT + CAP_CTX
ROUTER_PAD = 128
FF_TILE = 512
XROW_CHUNK = 576
MLA_KCHUNK = 256
VMEM_LIMIT = 56 * 1024 * 1024

C_QA, C_KA, C_VA, C_UP, C_UF, C_CQ, C_CKV, C_KR, C_KRS, C_END = 0, 256, 512, 768, 1024, 1280, 1536, 1664, 1792, 1920


def _cparams(sem):
    return pltpu.CompilerParams(dimension_semantics=sem, vmem_limit_bytes=VMEM_LIMIT)


def _dot(a, b):
    return jnp.dot(a, b, preferred_element_type=F32)


def _dot_t(a, b):
    return lax.dot_general(a, b, (((1,), (1,)), ((), ())), preferred_element_type=F32)


def _layer_norm(x, g, b):
    mu = jnp.mean(x, axis=-1, keepdims=True)
    xc = x - mu
    var = jnp.mean(xc * xc, axis=-1, keepdims=True)
    return xc * lax.rsqrt(var + LN_EPS) * g + b


def _mod_kernel(c_ref, w_ref, b_ref, o_ref):
    c = c_ref[...]
    s = c * jax.nn.sigmoid(c)
    w = w_ref[0]
    s_hi = s.astype(BF16)
    s_lo = (s - s_hi.astype(F32)).astype(BF16)
    w_hi = w.astype(BF16)
    w_lo = (w - w_hi.astype(F32)).astype(BF16)
    o_ref[0] = _dot(s_hi, w_hi) + _dot(s_hi, w_lo) + _dot(s_lo, w_hi) + b_ref[0]


def _modulation(cc, w_mod, b_mod):
    rows = cc.shape[0]
    n6 = 6 * D_MODEL
    tn = 1536
    return pl.pallas_call(
        _mod_kernel,
        grid=(DEPTH, n6 // tn),
        in_specs=[
            pl.BlockSpec((rows, D_MODEL), lambda l, j: (0, 0)),
            pl.BlockSpec((1, D_MODEL, tn), lambda l, j: (l, 0, j)),
            pl.BlockSpec((1, 1, tn), lambda l, j: (l, 0, j)),
        ],
        out_specs=pl.BlockSpec((1, rows, tn), lambda l, j: (l, 0, j)),
        out_shape=jax.ShapeDtypeStruct((DEPTH, rows, n6), F32),
        compiler_params=_cparams(("arbitrary", "arbitrary")),
    )(cc, w_mod, b_mod.reshape(DEPTH, 1, n6))


def _inproj_kernel(h_ref, mod_ref, wsm_ref, qn_ref, kvn_ref, wuq_ref, wukv_ref, cos_ref, sin_ref, bd_ref,
                   qa_ref, ka_ref, va_ref, up_ref, xcs_ref, qm_ref, km_ref, vm_ref):
    h = h_ref[0]
    sh1 = mod_ref[0, 0, 0:1, :]
    sc1 = mod_ref[0, 0, 1:2, :]
    u = (h * (1.0 + sc1) + sh1).astype(BF16)
    z = _dot(u, wsm_ref[...])
    qa_ref[0] = (z[:, C_QA:C_KA] * NA_SCALE).astype(BF16)
    ka_ref[0] = z[:, C_KA:C_VA].astype(BF16)
    va_ref[0] = z[:, C_VA:C_UP].astype(BF16)
    up_ref[0] = z[:, C_UP:C_UF]
    xcs_ref[0] = _dot(z[:, C_UF:C_CQ].astype(BF16), bd_ref[...]).astype(BF16)

    cos = cos_ref[...]
    sin = sin_ref[...]
    cos4 = jnp.concatenate([cos] * MLA_HEADS, axis=-1)
    sin4 = jnp.concatenate([sin] * MLA_HEADS, axis=-1)

    cq = z[:, C_CQ:C_CKV]
    nq = cq * lax.rsqrt(jnp.mean(cq * cq, axis=-1, keepdims=True) + LN_EPS) * qn_ref[...]
    q2 = _dot(nq.astype(BF16), wuq_ref[...])
    hw = MLA_HEADS * MLA_HEAD_PAD
    qm_ref[0] = ((q2[:, :hw] * cos4 + q2[:, hw:] * sin4) * MLA_SCALE).astype(BF16)

    ckv = z[:, C_CKV:C_KR]
    nkv = ckv * lax.rsqrt(jnp.mean(ckv * ckv, axis=-1, keepdims=True) + LN_EPS) * kvn_ref[...]
    kv2 = _dot(nkv.astype(BF16), wukv_ref[...])
    kr = z[:, C_KR:C_KRS] * cos + z[:, C_KRS:C_END] * sin
    km_ref[0] = (kv2[:, :hw] + jnp.concatenate([kr] * MLA_HEADS, axis=-1)).astype(BF16)
    vm_ref[0] = kv2[:, hw:].astype(BF16)


def _inproj(hall, mod_all, wsm, qn, kvn, wuq2, wukv2, cos_t, sin_t, bd):
    B = hall.shape[0]
    hw = MLA_HEADS * MLA_HEAD_PAD
    const = lambda b, t: (0, 0)
    row = lambda b, t: (b, t, 0)
    outs = [
        (256, BF16), (256, BF16), (256, BF16), (256, F32), (512, BF16), (hw, BF16), (hw, BF16), (MLA_HEADS * MLA_V, BF16),
    ]
    return pl.pallas_call(
        _inproj_kernel,
        grid=(B, N_TILES),
        in_specs=[
            pl.BlockSpec((1, TM, D_MODEL), row),
            pl.BlockSpec((1, 1, 6, D_MODEL), lambda b, t: (b, t // N_LAT_TILES, 0, 0)),
            pl.BlockSpec(wsm.shape, const),
            pl.BlockSpec(qn.shape, const),
            pl.BlockSpec(kvn.shape, const),
            pl.BlockSpec(wuq2.shape, const),
            pl.BlockSpec(wukv2.shape, const),
            pl.BlockSpec((TM, LANE), lambda b, t: (t, 0)),
            pl.BlockSpec((TM, LANE), lambda b, t: (t, 0)),
            pl.BlockSpec(bd.shape, const),
        ],
        out_specs=[pl.BlockSpec((1, TM, w), row) for w, _ in outs],
        out_shape=[jax.ShapeDtypeStruct((B, T_ALL, w), dt) for w, dt in outs],
        compiler_params=_cparams(("arbitrary", "arbitrary")),
    )(hall, mod_all, wsm, qn, kvn, wuq2, wukv2, cos_t, sin_t, bd)


def _softmax_pv(s, v):
    m = jnp.max(s, axis=-1, keepdims=True)
    p = jnp.exp(s - m)
    l = jnp.sum(p, axis=-1, keepdims=True)
    return _dot(p.astype(BF16), v) / l


def _na_kernel(q_ref, k_ref, v_ref, bias_ref, o_ref):
    t = pl.program_id(0)

    @pl.when(t < N_LAT_TILES)
    def _():
        ks = pl.multiple_of(jnp.clip(t * NA_QROWS - NA_WIN_ROWS // 2, 0, SEQ // GRID_W - NA_KROWS) * GRID_W, GRID_W)
        kk = jnp.concatenate([k_ref[0, pl.ds(ks, NA_KWIN), :], k_ref[0, SEQ:T_ALL, :]], axis=0)
        vv = jnp.concatenate([v_ref[0, pl.ds(ks, NA_KWIN), :], v_ref[0, SEQ:T_ALL, :]], axis=0)
        q = q_ref[0]
        for h in range(NA_HEADS):
            sl = slice(h * HEAD_DIM, (h + 1) * HEAD_DIM)
            s = _dot_t(q[:, sl], kk[:, sl]) + bias_ref[0, 0, h]
            o_ref[0, :, sl] = _softmax_pv(s, vv[:, sl]).astype(o_ref.dtype)

    @pl.when(t == N_LAT_TILES)
    def _():
        kk = k_ref[0, SEQ:T_ALL, :]
        vv = v_ref[0, SEQ:T_ALL, :]
        q = q_ref[0]
        for h in range(NA_HEADS):
            sl = slice(h * HEAD_DIM, (h + 1) * HEAD_DIM)
            o_ref[0, :, sl] = _softmax_pv(_dot_t(q[:, sl], kk[:, sl]), vv[:, sl]).astype(o_ref.dtype)


def _na_attention(qa, ka, va, bias, layer):
    B = qa.shape[0]

    def bias_idx(t, b):
        return (layer, jnp.where(t == 0, 0, jnp.where(t >= N_LAT_TILES - 1, 2, 1)), 0, 0, 0)

    return pl.pallas_call(
        _na_kernel,
        grid=(N_TILES, B),
        in_specs=[
            pl.BlockSpec((1, TM, 256), lambda t, b: (b, t, 0)),
            pl.BlockSpec((1, T_ALL, 256), lambda t, b: (b, 0, 0)),
            pl.BlockSpec((1, T_ALL, 256), lambda t, b: (b, 0, 0)),
            pl.BlockSpec((1, 1, NA_HEADS, TM, NA_KALL), bias_idx),
        ],
        out_specs=pl.BlockSpec((1, TM, 256), lambda t, b: (b, t, 0)),
        out_shape=jax.ShapeDtypeStruct((B, T_ALL, 256), BF16),
        compiler_params=_cparams(("arbitrary", "arbitrary")),
    )(qa, ka, va, bias)


def _mla_kernel(q_ref, k_ref, v_ref, o_ref):
    t = pl.program_id(1)

    def run(key_lo, n_keys):
        n_chunks = n_keys // MLA_KCHUNK
        for h in range(MLA_HEADS):
            ksl = slice(h * MLA_HEAD_PAD, (h + 1) * MLA_HEAD_PAD)
            vsl = slice(h * MLA_V, (h + 1) * MLA_V)
            q = q_ref[0, :, ksl]
            s_chunks = []
            m = None
            for c in range(n_chunks):
                rows = slice(key_lo + c * MLA_KCHUNK, key_lo + (c + 1) * MLA_KCHUNK)
                s = _dot_t(q, k_ref[0, rows, ksl])
                s_chunks.append(s)
                cm = jnp.max(s, axis=-1, keepdims=True)
                m = cm if m is None else jnp.maximum(m, cm)
            l = jnp.zeros((TM, 1), F32)
            o = jnp.zeros((TM, MLA_V), F32)
            for c in range(n_chunks):
                rows = slice(key_lo + c * MLA_KCHUNK, key_lo + (c + 1) * MLA_KCHUNK)
                p = jnp.exp(s_chunks[c] - m)
                l = l + jnp.sum(p, axis=-1, keepdims=True)
                o = o + _dot(p.astype(BF16), v_ref[0, rows, vsl])
            o_ref[0, :, vsl] = (o / l).astype(o_ref.dtype)

    @pl.when(t < N_LAT_TILES)
    def _():
        run(0, T_ALL)

    @pl.when(t == N_LAT_TILES)
    def _():
        run(SEQ, CTX_LEN)


def _mla_attention(qm, km, vm):
    B = qm.shape[0]
    hw = MLA_HEADS * MLA_HEAD_PAD
    vw = MLA_HEADS * MLA_V
    return pl.pallas_call(
        _mla_kernel,
        grid=(B, N_TILES),
        in_specs=[
            pl.BlockSpec((1, TM, hw), lambda b, t: (b, t, 0)),
            pl.BlockSpec((1, T_ALL, hw), lambda b, t: (b, 0, 0)),
            pl.BlockSpec((1, T_ALL, vw), lambda b, t: (b, 0, 0)),
        ],
        out_specs=pl.BlockSpec((1, TM, vw), lambda b, t: (b, t, 0)),
        out_shape=jax.ShapeDtypeStruct((B, T_ALL, vw), BF16),
        compiler_params=_cparams(("arbitrary", "arbitrary")),
    )(qm, km, vm)


def _pool_kernel(u_ref, w_ref, scale_ref, o_ref):
    x = u_ref[0]
    t = lax.broadcasted_iota(jnp.int32, (T_ALL, 1), 0)
    lane = lax.broadcasted_iota(jnp.int32, (1, POOL_GROUPS * POOL_CH), 1)
    half = jnp.left_shift(1, lane // POOL_CH)
    seg_lo = jnp.where(t < SEQ, 0, SEQ)
    seg_hi = jnp.where(t < SEQ, SEQ, T_ALL)
    max_half = POOL_WINDOWS[-1] // 2
    acc = jnp.zeros_like(x)
    for d in range(-max_half, max_half):
        xs = x if d == 0 else pltpu.roll(x, (-d) % T_ALL, 0)
        td = t + d
        ok = (td >= seg_lo) & (td < seg_hi) & (half >= -d) & (half > d)
        acc = acc + jnp.where(ok, xs, 0.0)
    cnt = jnp.minimum(t + half, seg_hi) - jnp.maximum(t - half, seg_lo)
    y = (acc / cnt.astype(F32) - x).astype(BF16)
    o_ref[0] = (_dot(y, w_ref[...]) * scale_ref[...]).astype(o_ref.dtype)


def _pool(up, w_bd, scale):
    B = up.shape[0]
    return pl.pallas_call(
        _pool_kernel,
        grid=(B,),
        in_specs=[
            pl.BlockSpec((1, T_ALL, 256), lambda b: (b, 0, 0)),
            pl.BlockSpec((256, 256), lambda b: (0, 0)),
            pl.BlockSpec((1, 256), lambda b: (0, 0)),
        ],
        out_specs=pl.BlockSpec((1, T_ALL, 256), lambda b: (b, 0, 0)),
        out_shape=jax.ShapeDtypeStruct((B, T_ALL, 256), BF16),
        compiler_params=_cparams(("arbitrary",)),
    )(up, w_bd, scale)


DFT_ROWS = 768


def _dft_kernel(c_ref, s_ref, x_ref, o_ref):
    x = x_ref[0]
    w = FFT_GROUPS * FFT_CH
    o_ref[0] = (_dot(c_ref[...], x[:, :w]) + _dot(s_ref[...], x[:, w:])).astype(o_ref.dtype)


def _dft_positions(cmat, smat, xcs):
    B = xcs.shape[0]
    return pl.pallas_call(
        _dft_kernel,
        grid=(T_ALL // DFT_ROWS, B),
        in_specs=[
            pl.BlockSpec((DFT_ROWS, T_ALL), lambda i, b: (i, 0)),
            pl.BlockSpec((DFT_ROWS, T_ALL), lambda i, b: (i, 0)),
            pl.BlockSpec((1, T_ALL, 512), lambda i, b: (b, 0, 0)),
        ],
        out_specs=pl.BlockSpec((1, DFT_ROWS, 256), lambda i, b: (b, i, 0)),
        out_shape=jax.ShapeDtypeStruct((B, T_ALL, 256), BF16),
        compiler_params=_cparams(("arbitrary", "arbitrary")),
    )(cmat, smat, xcs)


def _merge_kernel(h_ref, mod_ref, ya_ref, yb_ref, yc_ref, yd_ref, wg_ref, wbr_ref, wout_ref, g_ref, b_ref, wr_ref,
                  h1_ref, u2_ref, aff_ref):
    h = h_ref[0]
    sh1 = mod_ref[0, 0, 0:1, :]
    sc1 = mod_ref[0, 0, 1:2, :]
    g1 = mod_ref[0, 0, 2:3, :]
    sh2 = mod_ref[0, 0, 3:4, :]
    sc2 = mod_ref[0, 0, 4:5, :]
    u = (h * (1.0 + sc1) + sh1).astype(BF16)
    ys = (ya_ref, yb_ref, yc_ref, yd_ref)
    acc = jnp.zeros((TM, D_MODEL), F32)
    for n in range(N_BRANCH):
        gate = jax.nn.sigmoid(_dot(u, wg_ref[:, n * D_MODEL:(n + 1) * D_MODEL]))
        acc = acc + gate * _dot(ys[n][0], wbr_ref[n])
    mix = _dot(acc.astype(BF16), wout_ref[...])
    h1 = _layer_norm(DN_ALPHA * h + g1 * mix, g_ref[...], b_ref[...])
    h1_ref[0] = h1
    u2 = h1 * (1.0 + sc2) + sh2
    u2_hi = u2.astype(BF16)
    u2_ref[0] = u2_hi
    u2_lo = (u2 - u2_hi.astype(F32)).astype(BF16)
    logits = _dot(u2_hi, wr_ref[0]) + _dot(u2_hi, wr_ref[1]) + _dot(u2_lo, wr_ref[0])
    lane = lax.broadcasted_iota(jnp.int32, (1, ROUTER_PAD), 1)
    logits = jnp.where(lane < N_EXPERTS, logits, NEG)
    e = jnp.exp(logits - jnp.max(logits, axis=-1, keepdims=True))
    aff_ref[0] = e / jnp.sum(e, axis=-1, keepdims=True)


def _merge(hall, mod_all, ya, yb, yc, yd, wg, wbr, wout, ln_g, ln_b, wr2):
    B = hall.shape[0]
    row = lambda b, t: (b, t, 0)
    return pl.pallas_call(
        _merge_kernel,
        grid=(B, N_TILES),
        in_specs=[
            pl.BlockSpec((1, TM, D_MODEL), row),
            pl.BlockSpec((1, 1, 6, D_MODEL), lambda b, t: (b, t // N_LAT_TILES, 0, 0)),
            pl.BlockSpec((1, TM, 256), row),
            pl.BlockSpec((1, TM, 256), row),
            pl.BlockSpec((1, TM, 256), row),
            pl.BlockSpec((1, TM, 256), row),
            pl.BlockSpec(wg.shape, lambda b, t: (0, 0)),
            pl.BlockSpec(wbr.shape, lambda b, t: (0, 0, 0)),
            pl.BlockSpec(wout.shape, lambda b, t: (0, 0)),
            pl.BlockSpec((1, D_MODEL), lambda b, t: (0, 0)),
            pl.BlockSpec((1, D_MODEL), lambda b, t: (0, 0)),
            pl.BlockSpec(wr2.shape, lambda b, t: (0, 0, 0)),
        ],
        out_specs=[
            pl.BlockSpec((1, TM, D_MODEL), row),
            pl.BlockSpec((1, TM, D_MODEL), row),
            pl.BlockSpec((1, TM, ROUTER_PAD), row),
        ],
        out_shape=[
            jax.ShapeDtypeStruct((B, T_ALL, D_MODEL), F32),
            jax.ShapeDtypeStruct((B, T_ALL, D_MODEL), BF16),
            jax.ShapeDtypeStruct((B, T_ALL, ROUTER_PAD), F32),
        ],
        compiler_params=_cparams(("arbitrary", "arbitrary")),
    )(hall, mod_all, ya, yb, yc, yd, wg, wbr, wout, ln_g, ln_b, wr2)


def _gather_kernel(idx_ref, u_ref, o_ref):
    idx = idx_ref[0, 0]
    tok = lax.broadcasted_iota(jnp.int32, (CAP_ALL, T_ALL), 1)
    onehot = jnp.where(tok == idx, 1.0, 0.0).astype(BF16)
    o_ref[0] = _dot(onehot, u_ref[0]).astype(o_ref.dtype)


def _gather_tokens(idx_col, u2):
    B = u2.shape[0]
    return pl.pallas_call(
        _gather_kernel,
        grid=(B, N_EXPERTS),
        in_specs=[
            pl.BlockSpec((1, 1, CAP_ALL, 1), lambda b, e: (b, e, 0, 0)),
            pl.BlockSpec((1, T_ALL, D_MODEL), lambda b, e: (b, 0, 0)),
        ],
        out_specs=pl.BlockSpec((1, CAP_ALL, D_MODEL), lambda b, e: (e, b, 0)),
        out_shape=jax.ShapeDtypeStruct((N_EXPERTS, B * CAP_ALL, D_MODEL), BF16),
        compiler_params=_cparams(("arbitrary", "arbitrary")),
    )(idx_col, u2)


def _expert_kernel(x_ref, g_ref, wg_ref, wu_ref, wd_ref, o_ref, acc_ref, *, chunk):
    f = pl.program_id(1)
    wg = wg_ref[0, 0].astype(BF16)
    wu = wu_ref[0, 0].astype(BF16)
    wd = wd_ref[0, 0].astype(BF16)
    for c in range(x_ref.shape[1] // chunk):
        rows = pl.ds(c * chunk, chunk)
        x = x_ref[0, rows, :]
        a = _dot(x, wg)
        u = _dot(x, wu)
        y = _dot((a * jax.nn.sigmoid(a) * u).astype(BF16), wd)

        @pl.when(f == 0)
        def _():
            acc_ref[rows, :] = y

        @pl.when(f > 0)
        def _():
            acc_ref[rows, :] += y

    @pl.when(f == pl.num_programs(1) - 1)
    def _():
        o_ref[0] = (acc_ref[...] * g_ref[0]).astype(o_ref.dtype)


def _experts(xe, gcol, w_gate, w_up, w_down, layer):
    rows = xe.shape[1]
    chunk = XROW_CHUNK if rows % XROW_CHUNK == 0 else CAP_ALL
    return pl.pallas_call(
        functools.partial(_expert_kernel, chunk=chunk),
        grid=(N_EXPERTS, EXPERT_FF // FF_TILE),
        in_specs=[
            pl.BlockSpec((1, rows, D_MODEL), lambda e, f: (e, 0, 0)),
            pl.BlockSpec((1, rows, 1), lambda e, f: (e, 0, 0)),
            pl.BlockSpec((1, 1, D_MODEL, FF_TILE), lambda e, f: (layer, e, 0, f)),
            pl.BlockSpec((1, 1, D_MODEL, FF_TILE), lambda e, f: (layer, e, 0, f)),
            pl.BlockSpec((1, 1, FF_TILE, D_MODEL), lambda e, f: (layer, e, f, 0)),
        ],
        out_specs=pl.BlockSpec((1, rows, D_MODEL), lambda e, f: (e, 0, 0)),
        out_shape=jax.ShapeDtypeStruct((N_EXPERTS, rows, D_MODEL), BF16),
        scratch_shapes=[pltpu.VMEM((rows, D_MODEL), F32)],
        compiler_params=_cparams(("arbitrary", "arbitrary")),
    )(xe, gcol, w_gate, w_up, w_down)


def _scatter_kernel(h_ref, mod_ref, idx_ref, y_ref, g_ref, b_ref, o_ref):
    t = pl.program_id(1)
    idx = idx_ref[0]
    tok = lax.broadcasted_iota(jnp.int32, (TM, N_EXPERTS * CAP_ALL), 0) + t * TM
    onehot = jnp.where(tok == idx, 1.0, 0.0).astype(BF16)
    y = y_ref[...].reshape(N_EXPERTS * CAP_ALL, D_MODEL)
    fx = _dot(onehot, y)
    g2 = mod_ref[0, 0, 5:6, :]
    o_ref[0] = _layer_norm(DN_ALPHA * h_ref[0] + g2 * fx, g_ref[...], b_ref[...])


def _scatter_ln(h1, mod_all, idx_row, ye, ln_g, ln_b):
    B = h1.shape[0]
    return pl.pallas_call(
        _scatter_kernel,
        grid=(B, N_TILES),
        in_specs=[
            pl.BlockSpec((1, TM, D_MODEL), lambda b, t: (b, t, 0)),
            pl.BlockSpec((1, 1, 6, D_MODEL), lambda b, t: (b, t // N_LAT_TILES, 0, 0)),
            pl.BlockSpec((1, 1, N_EXPERTS * CAP_ALL), lambda b, t: (b, 0, 0)),
            pl.BlockSpec((N_EXPERTS, 1, CAP_ALL, D_MODEL), lambda b, t: (0, b, 0, 0)),
            pl.BlockSpec((1, D_MODEL), lambda b, t: (0, 0)),
            pl.BlockSpec((1, D_MODEL), lambda b, t: (0, 0)),
        ],
        out_specs=pl.BlockSpec((1, TM, D_MODEL), lambda b, t: (b, t, 0)),
        out_shape=jax.ShapeDtypeStruct((B, T_ALL, D_MODEL), F32),
        compiler_params=_cparams(("arbitrary", "arbitrary")),
    )(h1, mod_all, idx_row, ye, ln_g, ln_b)


def _rope_tables():
    n_freq = MLA_ROPE // 4
    inv = ROPE_BASE ** (-jnp.arange(n_freq, dtype=F32) / n_freq)
    t = jnp.arange(SEQ)
    row = (t // GRID_W).astype(F32)
    col = (t % GRID_W).astype(F32)
    ang = jnp.concatenate([row[:, None] * inv, col[:, None] * inv], axis=-1)
    cos, sin = jnp.cos(ang), jnp.sin(ang)
    ones = jnp.ones((SEQ, MLA_NOPE), F32)
    pad1 = jnp.ones((SEQ, MLA_HEAD_PAD - MLA_NOPE - MLA_ROPE), F32)
    cos_t = jnp.concatenate([ones, cos, cos, pad1], axis=-1)
    sin_t = jnp.concatenate([0 * ones, -sin, sin, 0 * pad1], axis=-1)
    cos_t = jnp.concatenate([cos_t, jnp.ones((CTX_LEN, MLA_HEAD_PAD), F32)], axis=0)
    sin_t = jnp.concatenate([sin_t, jnp.zeros((CTX_LEN, MLA_HEAD_PAD), F32)], axis=0)
    return cos_t, sin_t


def _dft_tables():
    def block(n):
        j = jnp.arange(n, dtype=jnp.int32)
        ang = ((j[:, None] * j[None, :]) % n).astype(F32) * (2.0 * np.pi / n)
        sc = 1.0 / np.sqrt(n * FFT_CH)
        return jnp.cos(ang) * sc, -jnp.sin(ang) * sc

    cl, sl = block(SEQ)
    cc, sc_ = block(CTX_LEN)

    def diag(a, b):
        top = jnp.concatenate([a, jnp.zeros((SEQ, CTX_LEN), F32)], axis=1)
        bot = jnp.concatenate([jnp.zeros((CTX_LEN, SEQ), F32), b], axis=1)
        return jnp.concatenate([top, bot], axis=0).astype(BF16)

    j = jnp.arange(FFT_CH, dtype=jnp.int32)
    ang = ((j[:, None] * j[None, :]) % FFT_CH).astype(F32) * (2.0 * np.pi / FFT_CH)
    eye = jnp.eye(FFT_GROUPS, dtype=F32)
    bd = jnp.concatenate([jnp.kron(eye, jnp.cos(ang)), jnp.kron(eye, jnp.sin(ang))], axis=1).astype(BF16)
    return diag(cl, cc), diag(sl, sc_), bd


def _na_bias_index():
    R = SEQ // GRID_W
    pats = [(0, 0), (NA_QROWS, 0), (R - NA_QROWS, R - NA_KROWS)]
    q = np.arange(TM)
    k = np.arange(NA_KWIN)
    qi, qc = q // GRID_W, q % GRID_W
    kj, kc = k // GRID_W, k % GRID_W
    c0 = np.clip(qc - NA_WIN_COLS // 2, 0, GRID_W - NA_WIN_COLS)
    col_in = (kc[None, :] >= c0[:, None]) & (kc[None, :] < c0[:, None] + NA_WIN_COLS)
    col_off = np.clip(kc[None, :] - qc[:, None], -(NA_WIN_COLS - 1), NA_WIN_COLS - 1) + (NA_WIN_COLS - 1)
    del col_off
    row_offs, valids = [], []
    for rb, ks in pats:
        qr = rb + np.arange(NA_QROWS)
        kr = ks + np.arange(NA_KROWS)
        r0 = np.clip(qr - NA_WIN_ROWS // 2, 0, R - NA_WIN_ROWS)
        row_in = (kr[None, :] >= r0[:, None]) & (kr[None, :] < r0[:, None] + NA_WIN_ROWS)
        row_offs.append(np.where(row_in, kr[None, :] - qr[:, None] + (NA_WIN_ROWS - 1), -1))
        valids.append(row_in[qi][:, kj] & col_in)
    return np.stack(row_offs), np.stack(valids)


def _na_bias(rpb):
    row_off, valid = _na_bias_index()
    reach = GRID_W - NA_WIN_COLS
    padded = jnp.pad(rpb, ((0, 0), (0, 0), (0, 0), (reach, reach)))
    band = jnp.stack([padded[..., GRID_W - 1 - qc:2 * GRID_W - 1 - qc] for qc in range(GRID_W)], axis=-2)
    masked = jnp.zeros(band.shape[:2] + band.shape[3:], F32)
    pats = []
    for p in range(row_off.shape[0]):
        qrows = []
        for qi in range(NA_QROWS):
            blocks = [masked if row_off[p, qi, kj] < 0 else band[:, :, row_off[p, qi, kj]] for kj in range(NA_KROWS)]
            qrows.append(jnp.concatenate(blocks, axis=-1))
        pats.append(jnp.concatenate(qrows, axis=-2))
    b = jnp.stack(pats, axis=1)
    b = jnp.where(jnp.asarray(valid)[None, :, None], b, NEG)
    return jnp.concatenate([b, jnp.zeros(b.shape[:-1] + (CTX_LEN,), F32)], axis=-1)


def _rearranged_in_weights(w_in):
    z = lambda n: jnp.zeros((D_MODEL, n), F32)
    kr = w_in[:, 1664:1696]
    kr_e, kr_o = kr[:, 0::2], kr[:, 1::2]
    pad = MLA_HEAD_PAD - MLA_NOPE - MLA_ROPE
    krb = jnp.concatenate([z(MLA_NOPE), kr_e, kr_o, z(pad)], axis=1)
    krs = jnp.concatenate([z(MLA_NOPE), kr_o, kr_e, z(pad)], axis=1)
    wsm = jnp.concatenate([w_in[:, :1664], krb, krs], axis=1).astype(BF16)
    return wsm, w_in[:, 1696:].astype(BF16)


def _rearranged_mla_weights(w_uq, w_ukv):
    pad = MLA_HEAD_PAD - MLA_NOPE - MLA_ROPE
    wq = w_uq.reshape(MLA_Q_RANK, MLA_HEADS, MLA_NOPE + MLA_ROPE)
    nope, rope = wq[..., :MLA_NOPE], wq[..., MLA_NOPE:]
    r_e, r_o = rope[..., 0::2], rope[..., 1::2]
    zq = jnp.zeros((MLA_Q_RANK, MLA_HEADS, pad), F32)
    q_main = jnp.concatenate([nope, r_e, r_o, zq], axis=-1).reshape(MLA_Q_RANK, -1)
    q_swap = jnp.concatenate([nope, r_o, r_e, zq], axis=-1).reshape(MLA_Q_RANK, -1)
    wuq2 = jnp.concatenate([q_main, q_swap], axis=1).astype(BF16)
    wkv = w_ukv.reshape(MLA_KV_RANK, MLA_HEADS, MLA_NOPE + MLA_V)
    k_nope, v = wkv[..., :MLA_NOPE], wkv[..., MLA_NOPE:]
    zk = jnp.zeros((MLA_KV_RANK, MLA_HEADS, MLA_HEAD_PAD - MLA_NOPE), F32)
    k_main = jnp.concatenate([k_nope, zk], axis=-1).reshape(MLA_KV_RANK, -1)
    wukv2 = jnp.concatenate([k_main, v.reshape(MLA_KV_RANK, -1)], axis=1).astype(BF16)
    return wuq2, wukv2


def _route(aff):
    a = aff.transpose(0, 2, 1)
    g_l, i_l = lax.top_k(a[:, :, :SEQ], CAP_LAT)
    g_c, i_c = lax.top_k(a[:, :, SEQ:], CAP_CTX)
    return jnp.concatenate([i_l, i_c + SEQ], axis=-1), jnp.concatenate([g_l, g_c], axis=-1)


def kernel(x, c, ctx, c_ctx, w_mod, b_mod, w_in, na_rpb, pool_w, pool_scale, mla_q_norm, mla_w_uq, mla_kv_norm,
           mla_w_ukv, w_branch, w_out, ln1_g, ln1_b, w_router, w_gate, w_up, w_down, ln2_g, ln2_b):
    B = x.shape[0]
    assert x.shape == (B, SEQ, D_MODEL) and ctx.shape == (B, CTX_LEN, D_MODEL) and B + 1 <= 16

    cc = jnp.concatenate([c, c_ctx[None], jnp.zeros((16 - B - 1, D_MODEL), F32)], axis=0)
    mod = _modulation(cc, w_mod, b_mod).reshape(DEPTH, 16, 6, D_MODEL)
    cos_t, sin_t = _rope_tables()
    cmat, smat, bd = _dft_tables()
    na_bias = _na_bias(na_rpb)
    hall = jnp.concatenate([x, ctx], axis=1)

    for l in range(DEPTH):
        mod_all = jnp.stack([mod[l, :B], jnp.broadcast_to(mod[l, B], (B, 6, D_MODEL))], axis=1)
        wsm, wgates = _rearranged_in_weights(w_in[l])
        wuq2, wukv2 = _rearranged_mla_weights(mla_w_uq[l], mla_w_ukv[l])
        qa, ka, va, up, xcs, qm, km, vm = _inproj(
            hall, mod_all, wsm, mla_q_norm[l][None], mla_kv_norm[l][None], wuq2, wukv2, cos_t, sin_t, bd)
        ya = _na_attention(qa, ka, va, na_bias, l)
        pool_bd = jax.scipy.linalg.block_diag(*[pool_w[l, g] for g in range(POOL_GROUPS)]).astype(BF16)
        yb = _pool(up, pool_bd, pool_scale[l][None])
        yc = _dft_positions(cmat, smat, xcs)
        yd = _mla_attention(qm, km, vm)
        wr = jnp.pad(w_router[l], ((0, 0), (0, ROUTER_PAD - N_EXPERTS)))
        wr_hi = wr.astype(BF16)
        wr2 = jnp.stack([wr_hi, (wr - wr_hi.astype(F32)).astype(BF16)])
        h1, u2, aff = _merge(hall, mod_all, ya, yb, yc, yd, wgates, w_branch[l].astype(BF16), w_out[l].astype(BF16),
                             ln1_g[l][None], ln1_b[l][None], wr2)
        idx, gate = _route(aff[:, :, :N_EXPERTS])
        xe = _gather_tokens(idx[..., None], u2)
        gcol = gate.transpose(1, 0, 2).reshape(N_EXPERTS, B * CAP_ALL, 1)
        ye = _experts(xe, gcol, w_gate, w_up, w_down, l)
        hall = _scatter_ln(h1, mod_all, idx.reshape(B, 1, N_EXPERTS * CAP_ALL),
                           ye.reshape(N_EXPERTS, B, CAP_ALL, D_MODEL), ln2_g[l][None], ln2_b[l][None])
    return hall[:, :SEQ]
```

```python
import functools

import numpy as np
import jax
import jax.numpy as jnp
from jax import lax
from jax.experimental import pallas as pl
from jax.experimental.pallas import tpu as pltpu

F32 = jnp.float32
BF16 = jnp.bfloat16

D_MODEL = 1024
DEPTH = 4
GRID_W = 64
SEQ = 2048
CTX_LEN = 256
T_ALL = SEQ + CTX_LEN
HEAD_DIM = 64
NA_HEADS = 4
NA_WIN_ROWS = 8
NA_WIN_COLS = 16
NA_SCALE = HEAD_DIM ** -0.5
POOL_GROUPS = 4
POOL_CH = 64
POOL_WINDOWS = (2, 4, 8, 16)
FFT_GROUPS = 4
FFT_CH = 64
MLA_HEADS = 4
MLA_Q_RANK = 256
MLA_KV_RANK = 128
MLA_NOPE = 64
MLA_ROPE = 32
MLA_V = 64
MLA_SCALE = (MLA_NOPE + MLA_ROPE) ** -0.5
ROPE_BASE = 10000.0
BRANCH_W = 256
N_BRANCH = 4
N_EXPERTS = 16
EXPERT_FF = 2048
EC_CAPACITY = 2
DN_ALPHA = (2 * DEPTH) ** 0.25
LN_EPS = 1e-5
NEG = -1e30

LANE = 128
TM = 256
N_TILES = T_ALL // TM
N_LAT_TILES = SEQ // TM
MLA_HEAD_PAD = 128
NA_QROWS = TM // GRID_W
NA_KROWS = 12
NA_KWIN = NA_KROWS * GRID_W
NA_KALL = NA_KWIN + CTX_LEN
CAP_LAT = EC_CAPACITY * SEQ // N_EXPERTS
CAP_CTX = EC_CAPACITY * CTX_LEN // N_EXPERTS
CAP_ALL = CAP_LAT + CAP_CTX
ROUTER_PAD = 128
FF_TILE = 512
XROW_CHUNK = 576
VMEM_LIMIT = 56 * 1024 * 1024

C_QA, C_KA, C_VA, C_UP, C_UF, C_CQ, C_CKV, C_KR, C_KRS, C_END = 0, 256, 512, 768, 1024, 1280, 1536, 1664, 1792, 1920


def _cparams(sem):
    return pltpu.CompilerParams(dimension_semantics=sem, vmem_limit_bytes=VMEM_LIMIT)


def _dot(a, b):
    return jnp.dot(a, b, preferred_element_type=F32)


def _dot_t(a, b):
    return lax.dot_general(a, b, (((1,), (1,)), ((), ())), preferred_element_type=F32)


def _layer_norm(x, g, b):
    mu = jnp.mean(x, axis=-1, keepdims=True)
    xc = x - mu
    var = jnp.mean(xc * xc, axis=-1, keepdims=True)
    return xc * lax.rsqrt(var + LN_EPS) * g + b


def _mod_kernel(c_ref, w_ref, b_ref, o_ref):
    c = c_ref[...]
    s = c * jax.nn.sigmoid(c)
    w = w_ref[0]
    s_hi = s.astype(BF16)
    s_lo = (s - s_hi.astype(F32)).astype(BF16)
    w_hi = w.astype(BF16)
    w_lo = (w - w_hi.astype(F32)).astype(BF16)
    o_ref[0] = _dot(s_hi, w_hi) + _dot(s_hi, w_lo) + _dot(s_lo, w_hi) + b_ref[0]


def _modulation(cc, w_mod, b_mod):
    rows = cc.shape[0]
    n6 = 6 * D_MODEL
    tn = 1536
    return pl.pallas_call(
        _mod_kernel,
        grid=(DEPTH, n6 // tn),
        in_specs=[
            pl.BlockSpec((rows, D_MODEL), lambda l, j: (0, 0)),
            pl.BlockSpec((1, D_MODEL, tn), lambda l, j: (l, 0, j)),
            pl.BlockSpec((1, 1, tn), lambda l, j: (l, 0, j)),
        ],
        out_specs=pl.BlockSpec((1, rows, tn), lambda l, j: (l, 0, j)),
        out_shape=jax.ShapeDtypeStruct((DEPTH, rows, n6), F32),
        compiler_params=_cparams(("arbitrary", "arbitrary")),
    )(cc, w_mod, b_mod.reshape(DEPTH, 1, n6))


def _inproj_kernel(h_ref, mod_ref, wsm_ref, qn_ref, kvn_ref, wuq_ref, wukv_ref, cos_ref, sin_ref, bd_ref,
                   qa_ref, ka_ref, va_ref, up_ref, xcs_ref, qm_ref, km_ref, vm_ref):
    pb = h_ref.shape[0]

    def put(ref, val):
        ref[...] = val.astype(ref.dtype).reshape(ref.shape)

    sh1 = mod_ref[:, 0, 0:1, :]
    sc1 = mod_ref[:, 0, 1:2, :]
    u = (h_ref[...] * (1.0 + sc1) + sh1).astype(BF16).reshape(pb * TM, D_MODEL)
    z = _dot(u, wsm_ref[...])
    put(qa_ref, z[:, C_QA:C_KA] * NA_SCALE)
    put(ka_ref, z[:, C_KA:C_VA])
    put(va_ref, z[:, C_VA:C_UP])
    put(up_ref, z[:, C_UP:C_UF])
    put(xcs_ref, _dot(z[:, C_UF:C_CQ].astype(BF16), bd_ref[...]))

    cos = jnp.concatenate([cos_ref[...]] * pb, axis=0)
    sin = jnp.concatenate([sin_ref[...]] * pb, axis=0)
    cos4 = jnp.concatenate([cos] * MLA_HEADS, axis=-1)
    sin4 = jnp.concatenate([sin] * MLA_HEADS, axis=-1)

    cq = z[:, C_CQ:C_CKV]
    nq = cq * lax.rsqrt(jnp.mean(cq * cq, axis=-1, keepdims=True) + LN_EPS) * qn_ref[...]
    q2 = _dot(nq.astype(BF16), wuq_ref[...])
    hw = MLA_HEADS * MLA_HEAD_PAD
    put(qm_ref, (q2[:, :hw] * cos4 + q2[:, hw:] * sin4) * MLA_SCALE)

    ckv = z[:, C_CKV:C_KR]
    nkv = ckv * lax.rsqrt(jnp.mean(ckv * ckv, axis=-1, keepdims=True) + LN_EPS) * kvn_ref[...]
    kv2 = _dot(nkv.astype(BF16), wukv_ref[...])
    kr = z[:, C_KR:C_KRS] * cos + z[:, C_KRS:C_END] * sin
    put(km_ref, kv2[:, :hw] + jnp.concatenate([kr] * MLA_HEADS, axis=-1))
    put(vm_ref, kv2[:, hw:])


def _samples_per_step(B):
    return 2 if B % 2 == 0 else 1


def _resident(shape):
    return pl.BlockSpec(shape, lambda *_: (0,) * len(shape), pipeline_mode=pl.Buffered(1))


def _inproj(hall, mod_all, wsm, qn, kvn, wuq2, wukv2, cos_t, sin_t, bd):
    B = hall.shape[0]
    pb = _samples_per_step(B)
    hw = MLA_HEADS * MLA_HEAD_PAD
    row = lambda b, t: (b, t, 0)
    outs = [
        (256, BF16), (256, BF16), (256, BF16), (256, F32), (512, BF16), (hw, BF16), (hw, BF16), (MLA_HEADS * MLA_V, BF16),
    ]
    return pl.pallas_call(
        _inproj_kernel,
        grid=(B // pb, N_TILES),
        in_specs=[
            pl.BlockSpec((pb, TM, D_MODEL), row),
            pl.BlockSpec((pb, 1, 6, D_MODEL), lambda b, t: (b, t // N_LAT_TILES, 0, 0)),
            _resident(wsm.shape),
            _resident(qn.shape),
            _resident(kvn.shape),
            _resident(wuq2.shape),
            _resident(wukv2.shape),
            pl.BlockSpec((TM, LANE), lambda b, t: (t, 0)),
            pl.BlockSpec((TM, LANE), lambda b, t: (t, 0)),
            _resident(bd.shape),
        ],
        out_specs=[pl.BlockSpec((pb, TM, w), row) for w, _ in outs],
        out_shape=[jax.ShapeDtypeStruct((B, T_ALL, w), dt) for w, dt in outs],
        compiler_params=_cparams(("arbitrary", "arbitrary")),
    )(hall, mod_all, wsm, qn, kvn, wuq2, wukv2, cos_t, sin_t, bd)


def _softmax_pv(s, v):
    m = jnp.max(s, axis=-1, keepdims=True)
    p = jnp.exp(s - m)
    l = jnp.sum(p, axis=-1, keepdims=True)
    return _dot(p.astype(BF16), v) / l


def _na_kernel(q_ref, k_ref, v_ref, bias_ref, o_ref):
    t = pl.program_id(0)

    @pl.when(t < N_LAT_TILES)
    def _():
        ks = pl.multiple_of(jnp.clip(t * NA_QROWS - NA_WIN_ROWS // 2, 0, SEQ // GRID_W - NA_KROWS) * GRID_W, GRID_W)
        kk = jnp.concatenate([k_ref[0, pl.ds(ks, NA_KWIN), :], k_ref[0, SEQ:T_ALL, :]], axis=0)
        vv = jnp.concatenate([v_ref[0, pl.ds(ks, NA_KWIN), :], v_ref[0, SEQ:T_ALL, :]], axis=0)
        q = q_ref[0]
        for h in range(NA_HEADS):
            sl = slice(h * HEAD_DIM, (h + 1) * HEAD_DIM)
            s = _dot_t(q[:, sl], kk[:, sl]) + bias_ref[0, 0, h]
            o_ref[0, :, sl] = _softmax_pv(s, vv[:, sl]).astype(o_ref.dtype)

    @pl.when(t == N_LAT_TILES)
    def _():
        kk = k_ref[0, SEQ:T_ALL, :]
        vv = v_ref[0, SEQ:T_ALL, :]
        q = q_ref[0]
        for h in range(NA_HEADS):
            sl = slice(h * HEAD_DIM, (h + 1) * HEAD_DIM)
            o_ref[0, :, sl] = _softmax_pv(_dot_t(q[:, sl], kk[:, sl]), vv[:, sl]).astype(o_ref.dtype)


def _na_attention(qa, ka, va, bias, layer, n_tiles):
    B = qa.shape[0]

    def bias_idx(t, b):
        return (layer, jnp.where(t == 0, 0, jnp.where(t >= N_LAT_TILES - 1, 2, 1)), 0, 0, 0)

    return pl.pallas_call(
        _na_kernel,
        grid=(n_tiles, B),
        in_specs=[
            pl.BlockSpec((1, TM, 256), lambda t, b: (b, t, 0)),
            pl.BlockSpec((1, T_ALL, 256), lambda t, b: (b, 0, 0)),
            pl.BlockSpec((1, T_ALL, 256), lambda t, b: (b, 0, 0)),
            pl.BlockSpec((1, 1, NA_HEADS, TM, NA_KALL), bias_idx),
        ],
        out_specs=pl.BlockSpec((1, TM, 256), lambda t, b: (b, t, 0)),
        out_shape=jax.ShapeDtypeStruct((B, n_tiles * TM, 256), BF16),
        compiler_params=_cparams(("arbitrary", "arbitrary")),
    )(qa, ka, va, bias)


def _mla_kernel(q_ref, k_ref, v_ref, o_ref):
    t = pl.program_id(1)

    def run(key_lo, n_keys):
        keys = slice(key_lo, key_lo + n_keys)
        for h in range(MLA_HEADS):
            ksl = slice(h * MLA_HEAD_PAD, (h + 1) * MLA_HEAD_PAD)
            vsl = slice(h * MLA_V, (h + 1) * MLA_V)
            s = _dot_t(q_ref[0, :, ksl], k_ref[0, keys, ksl])
            o_ref[0, :, vsl] = _softmax_pv(s, v_ref[0, keys, vsl]).astype(o_ref.dtype)

    @pl.when(t < N_LAT_TILES)
    def _():
        run(0, T_ALL)

    @pl.when(t == N_LAT_TILES)
    def _():
        run(SEQ, CTX_LEN)


def _mla_attention(qm, km, vm, n_tiles):
    B = qm.shape[0]
    hw = MLA_HEADS * MLA_HEAD_PAD
    vw = MLA_HEADS * MLA_V
    return pl.pallas_call(
        _mla_kernel,
        grid=(B, n_tiles),
        in_specs=[
            pl.BlockSpec((1, TM, hw), lambda b, t: (b, t, 0)),
            pl.BlockSpec((1, T_ALL, hw), lambda b, t: (b, 0, 0)),
            pl.BlockSpec((1, T_ALL, vw), lambda b, t: (b, 0, 0)),
        ],
        out_specs=pl.BlockSpec((1, TM, vw), lambda b, t: (b, t, 0)),
        out_shape=jax.ShapeDtypeStruct((B, n_tiles * TM, vw), BF16),
        compiler_params=_cparams(("arbitrary", "arbitrary")),
    )(qm, km, vm)


def _pool_kernel(u_ref, w_ref, scale_ref, o_ref):
    x = u_ref[0]
    t = lax.broadcasted_iota(jnp.int32, (T_ALL, 1), 0)
    lane = lax.broadcasted_iota(jnp.int32, (1, POOL_GROUPS * POOL_CH), 1)
    half = jnp.left_shift(1, lane // POOL_CH)
    seg_lo = jnp.where(t < SEQ, 0, SEQ)
    seg_hi = jnp.where(t < SEQ, SEQ, T_ALL)
    max_half = POOL_WINDOWS[-1] // 2
    acc = jnp.zeros_like(x)
    for d in range(-max_half, max_half):
        xs = x if d == 0 else pltpu.roll(x, (-d) % T_ALL, 0)
        td = t + d
        ok = (td >= seg_lo) & (td < seg_hi) & (half >= -d) & (half > d)
        acc = acc + jnp.where(ok, xs, 0.0)
    cnt = jnp.minimum(t + half, seg_hi) - jnp.maximum(t - half, seg_lo)
    y = (acc / cnt.astype(F32) - x).astype(BF16)
    o_ref[0] = (_dot(y, w_ref[...]) * scale_ref[...]).astype(o_ref.dtype)


def _pool(up, w_bd, scale):
    B = up.shape[0]
    return pl.pallas_call(
        _pool_kernel,
        grid=(B,),
        in_specs=[
            pl.BlockSpec((1, T_ALL, 256), lambda b: (b, 0, 0)),
            pl.BlockSpec((256, 256), lambda b: (0, 0)),
            pl.BlockSpec((1, 256), lambda b: (0, 0)),
        ],
        out_specs=pl.BlockSpec((1, T_ALL, 256), lambda b: (b, 0, 0)),
        out_shape=jax.ShapeDtypeStruct((B, T_ALL, 256), BF16),
        compiler_params=_cparams(("arbitrary",)),
    )(up, w_bd, scale)


DFT_ROWS = 768


def _dft_kernel(c_ref, s_ref, x_ref, o_ref):
    x = x_ref[0]
    w = FFT_GROUPS * FFT_CH
    o_ref[0] = (_dot(c_ref[...], x[:, :w]) + _dot(s_ref[...], x[:, w:])).astype(o_ref.dtype)


def _dft_positions(cmat, smat, xcs):
    B = xcs.shape[0]
    return pl.pallas_call(
        _dft_kernel,
        grid=(T_ALL // DFT_ROWS, B),
        in_specs=[
            pl.BlockSpec((DFT_ROWS, T_ALL), lambda i, b: (i, 0)),
            pl.BlockSpec((DFT_ROWS, T_ALL), lambda i, b: (i, 0)),
            pl.BlockSpec((1, T_ALL, 512), lambda i, b: (b, 0, 0)),
        ],
        out_specs=pl.BlockSpec((1, DFT_ROWS, 256), lambda i, b: (b, i, 0)),
        out_shape=jax.ShapeDtypeStruct((B, T_ALL, 256), BF16),
        compiler_params=_cparams(("arbitrary", "arbitrary")),
    )(cmat, smat, xcs)


def _merge_kernel(h_ref, mod_ref, ya_ref, yb_ref, yc_ref, yd_ref, wg_ref, wbr_ref, wout_ref, g_ref, b_ref, wr_ref,
                  h1_ref, u2_ref, aff_ref):
    pb = h_ref.shape[0]
    rows = pb * TM
    h = h_ref[...]
    sh1 = mod_ref[:, 0, 0:1, :]
    sc1 = mod_ref[:, 0, 1:2, :]
    g1 = mod_ref[:, 0, 2:3, :]
    sh2 = mod_ref[:, 0, 3:4, :]
    sc2 = mod_ref[:, 0, 4:5, :]
    u = (h * (1.0 + sc1) + sh1).astype(BF16).reshape(rows, D_MODEL)
    ys = (ya_ref, yb_ref, yc_ref, yd_ref)
    acc = jnp.zeros((rows, D_MODEL), F32)
    for n in range(N_BRANCH):
        gate = jax.nn.sigmoid(_dot(u, wg_ref[:, n * D_MODEL:(n + 1) * D_MODEL]))
        acc = acc + gate * _dot(ys[n][...].reshape(rows, BRANCH_W), wbr_ref[n])
    mix = _dot(acc.astype(BF16), wout_ref[...]).reshape(pb, TM, D_MODEL)
    h1 = _layer_norm(DN_ALPHA * h + g1 * mix, g_ref[...], b_ref[...])
    h1_ref[...] = h1
    u2 = h1 * (1.0 + sc2) + sh2
    u2_hi = u2.astype(BF16)
    u2_ref[...] = u2_hi
    u2_lo = (u2 - u2_hi.astype(F32)).astype(BF16).reshape(rows, D_MODEL)
    hi_terms = _dot(u2_hi.reshape(rows, D_MODEL), wr_ref[...])
    logits = hi_terms[:, :ROUTER_PAD] + hi_terms[:, ROUTER_PAD:] + _dot(u2_lo, wr_ref[:, :ROUTER_PAD])
    lane = lax.broadcasted_iota(jnp.int32, (1, ROUTER_PAD), 1)
    logits = jnp.where(lane < N_EXPERTS, logits, NEG)
    e = jnp.exp(logits - jnp.max(logits, axis=-1, keepdims=True))
    aff_ref[...] = (e / jnp.sum(e, axis=-1, keepdims=True)).reshape(pb, TM, ROUTER_PAD)


def _merge(hall, mod_all, ya, yb, yc, yd, wg, wbr, wout, ln_g, ln_b, wr_cat, n_tiles):
    B = hall.shape[0]
    pb = _samples_per_step(B)
    row = lambda b, t: (b, t, 0)
    t_rows = n_tiles * TM
    return pl.pallas_call(
        _merge_kernel,
        grid=(B // pb, n_tiles),
        in_specs=[
            pl.BlockSpec((pb, TM, D_MODEL), row),
            pl.BlockSpec((pb, 1, 6, D_MODEL), lambda b, t: (b, t // N_LAT_TILES, 0, 0)),
            pl.BlockSpec((pb, TM, BRANCH_W), row),
            pl.BlockSpec((pb, TM, BRANCH_W), row),
            pl.BlockSpec((pb, TM, BRANCH_W), row),
            pl.BlockSpec((pb, TM, BRANCH_W), row),
            _resident(wg.shape),
            _resident(wbr.shape),
            _resident(wout.shape),
            _resident((1, D_MODEL)),
            _resident((1, D_MODEL)),
            _resident(wr_cat.shape),
        ],
        out_specs=[
            pl.BlockSpec((pb, TM, D_MODEL), row),
            pl.BlockSpec((pb, TM, D_MODEL), row),
            pl.BlockSpec((pb, TM, ROUTER_PAD), row),
        ],
        out_shape=[
            jax.ShapeDtypeStruct((B, t_rows, D_MODEL), F32),
            jax.ShapeDtypeStruct((B, t_rows, D_MODEL), BF16),
            jax.ShapeDtypeStruct((B, t_rows, ROUTER_PAD), F32),
        ],
        compiler_params=_cparams(("arbitrary", "arbitrary")),
    )(hall, mod_all, ya, yb, yc, yd, wg, wbr, wout, ln_g, ln_b, wr_cat)


def _onehot(match):
    return jnp.where(match, 1.0, 0.0).astype(BF16)


def _gather_kernel(idx_ref, u_ref, o_ref):
    idx = idx_ref[0, 0]
    tok = lax.broadcasted_iota(jnp.int32, (CAP_LAT, SEQ), 1)
    o_ref[0, :CAP_LAT, :] = _dot(_onehot(tok == idx[:CAP_LAT]), u_ref[0, :SEQ, :]).astype(o_ref.dtype)
    if idx.shape[0] > CAP_LAT:
        tok = lax.broadcasted_iota(jnp.int32, (CAP_CTX, CTX_LEN), 1) + SEQ
        o_ref[0, CAP_LAT:, :] = _dot(_onehot(tok == idx[CAP_LAT:]), u_ref[0, SEQ:, :]).astype(o_ref.dtype)


def _gather_tokens(idx_col, u2):
    B, t_rows = u2.shape[:2]
    cap = idx_col.shape[2]
    return pl.pallas_call(
        _gather_kernel,
        grid=(B, N_EXPERTS),
        in_specs=[
            pl.BlockSpec((1, 1, cap, 1), lambda b, e: (b, e, 0, 0)),
            pl.BlockSpec((1, t_rows, D_MODEL), lambda b, e: (b, 0, 0)),
        ],
        out_specs=pl.BlockSpec((1, cap, D_MODEL), lambda b, e: (e, b, 0)),
        out_shape=jax.ShapeDtypeStruct((N_EXPERTS, B * cap, D_MODEL), BF16),
        compiler_params=_cparams(("arbitrary", "arbitrary")),
    )(idx_col, u2)


def _expert_kernel(x_ref, g_ref, wg_ref, wu_ref, wd_ref, o_ref, acc_ref, *, chunk):
    f = pl.program_id(1)

    @pl.when(f == 0)
    def _():
        acc_ref[...] = jnp.zeros_like(acc_ref)

    wg = wg_ref[0, 0].astype(BF16)
    wu = wu_ref[0, 0].astype(BF16)
    wd = wd_ref[0, 0].astype(BF16)
    for c in range(x_ref.shape[1] // chunk):
        rows = pl.ds(c * chunk, chunk)
        x = x_ref[0, rows, :]
        a = _dot(x, wg)
        u = _dot(x, wu)
        acc_ref[rows, :] += _dot((a * jax.nn.sigmoid(a) * u).astype(BF16), wd)

    @pl.when(f == pl.num_programs(1) - 1)
    def _():
        o_ref[0] = (acc_ref[...] * g_ref[0]).astype(o_ref.dtype)


def _experts(xe, gcol, w_gate, w_up, w_down, layer):
    rows = xe.shape[1]
    chunk = next(c for c in (XROW_CHUNK, 512, CAP_ALL, CAP_LAT) if rows % c == 0)
    return pl.pallas_call(
        functools.partial(_expert_kernel, chunk=chunk),
        grid=(N_EXPERTS, EXPERT_FF // FF_TILE),
        in_specs=[
            pl.BlockSpec((1, rows, D_MODEL), lambda e, f: (e, 0, 0)),
            pl.BlockSpec((1, rows, 1), lambda e, f: (e, 0, 0)),
            pl.BlockSpec((1, 1, D_MODEL, FF_TILE), lambda e, f: (layer, e, 0, f)),
            pl.BlockSpec((1, 1, D_MODEL, FF_TILE), lambda e, f: (layer, e, 0, f)),
            pl.BlockSpec((1, 1, FF_TILE, D_MODEL), lambda e, f: (layer, e, f, 0)),
        ],
        out_specs=pl.BlockSpec((1, rows, D_MODEL), lambda e, f: (e, 0, 0)),
        out_shape=jax.ShapeDtypeStruct((N_EXPERTS, rows, D_MODEL), BF16),
        scratch_shapes=[pltpu.VMEM((rows, D_MODEL), F32)],
        compiler_params=_cparams(("arbitrary", "arbitrary")),
    )(xe, gcol, w_gate, w_up, w_down)


def _scatter_kernel(h_ref, mod_ref, y_ref, g_ref, b_ref, *rest):
    idx_refs, o_ref = rest[:-1], rest[-1]
    t = pl.program_id(1)

    def scatter(idx_ref, slot_lo, slot_hi):
        n = N_EXPERTS * (slot_hi - slot_lo)
        tok = lax.broadcasted_iota(jnp.int32, (TM, n), 0) + t * TM
        y = y_ref[:, 0, slot_lo:slot_hi, :].reshape(n, D_MODEL)
        fx = _dot(_onehot(tok == idx_ref[0]), y)
        g2 = mod_ref[0, 0, 5:6, :]
        o_ref[0] = _layer_norm(DN_ALPHA * h_ref[0] + g2 * fx, g_ref[...], b_ref[...])

    if len(idx_refs) == 1:
        scatter(idx_refs[0], 0, CAP_LAT)
    else:
        pl.when(t < N_LAT_TILES)(lambda: scatter(idx_refs[0], 0, CAP_LAT))
        pl.when(t == N_LAT_TILES)(lambda: scatter(idx_refs[1], CAP_LAT, CAP_ALL))


def _scatter_ln(h1, mod_all, ye, ln_g, ln_b, idx_rows):
    B = h1.shape[0]
    cap = ye.shape[2]
    n_tiles = N_LAT_TILES if len(idx_rows) == 1 else N_TILES
    return pl.pallas_call(
        _scatter_kernel,
        grid=(B, n_tiles),
        in_specs=[
            pl.BlockSpec((1, TM, D_MODEL), lambda b, t: (b, t, 0)),
            pl.BlockSpec((1, 1, 6, D_MODEL), lambda b, t: (b, t // N_LAT_TILES, 0, 0)),
            pl.BlockSpec((N_EXPERTS, 1, cap, D_MODEL), lambda b, t: (0, b, 0, 0)),
            pl.BlockSpec((1, D_MODEL), lambda b, t: (0, 0)),
            pl.BlockSpec((1, D_MODEL), lambda b, t: (0, 0)),
        ] + [pl.BlockSpec((1, 1, r.shape[2]), lambda b, t: (b, 0, 0)) for r in idx_rows],
        out_specs=pl.BlockSpec((1, TM, D_MODEL), lambda b, t: (b, t, 0)),
        out_shape=jax.ShapeDtypeStruct((B, n_tiles * TM, D_MODEL), F32),
        compiler_params=_cparams(("arbitrary", "arbitrary")),
    )(h1, mod_all, ye, ln_g, ln_b, *idx_rows)


def _rope_tables():
    n_freq = MLA_ROPE // 4
    inv = ROPE_BASE ** (-jnp.arange(n_freq, dtype=F32) / n_freq)
    t = jnp.arange(SEQ)
    row = (t // GRID_W).astype(F32)
    col = (t % GRID_W).astype(F32)
    ang = jnp.concatenate([row[:, None] * inv, col[:, None] * inv], axis=-1)
    cos, sin = jnp.cos(ang), jnp.sin(ang)
    ones = jnp.ones((SEQ, MLA_NOPE), F32)
    pad1 = jnp.ones((SEQ, MLA_HEAD_PAD - MLA_NOPE - MLA_ROPE), F32)
    cos_t = jnp.concatenate([ones, cos, cos, pad1], axis=-1)
    sin_t = jnp.concatenate([0 * ones, -sin, sin, 0 * pad1], axis=-1)
    cos_t = jnp.concatenate([cos_t, jnp.ones((CTX_LEN, MLA_HEAD_PAD), F32)], axis=0)
    sin_t = jnp.concatenate([sin_t, jnp.zeros((CTX_LEN, MLA_HEAD_PAD), F32)], axis=0)
    return cos_t, sin_t


def _dft_tables():
    def block(n):
        j = jnp.arange(n, dtype=jnp.int32)
        ang = ((j[:, None] * j[None, :]) % n).astype(F32) * (2.0 * np.pi / n)
        sc = 1.0 / np.sqrt(n * FFT_CH)
        return jnp.cos(ang) * sc, -jnp.sin(ang) * sc

    cl, sl = block(SEQ)
    cc, sc_ = block(CTX_LEN)

    def diag(a, b):
        top = jnp.concatenate([a, jnp.zeros((SEQ, CTX_LEN), F32)], axis=1)
        bot = jnp.concatenate([jnp.zeros((CTX_LEN, SEQ), F32), b], axis=1)
        return jnp.concatenate([top, bot], axis=0).astype(BF16)

    j = jnp.arange(FFT_CH, dtype=jnp.int32)
    ang = ((j[:, None] * j[None, :]) % FFT_CH).astype(F32) * (2.0 * np.pi / FFT_CH)
    eye = jnp.eye(FFT_GROUPS, dtype=F32)
    bd = jnp.concatenate([jnp.kron(eye, jnp.cos(ang)), jnp.kron(eye, jnp.sin(ang))], axis=1).astype(BF16)
    return diag(cl, cc), diag(sl, sc_), bd


def _na_bias_index():
    R = SEQ // GRID_W
    pats = [(0, 0), (NA_QROWS, 0), (R - NA_QROWS, R - NA_KROWS)]
    q = np.arange(TM)
    k = np.arange(NA_KWIN)
    qi, qc = q // GRID_W, q % GRID_W
    kj, kc = k // GRID_W, k % GRID_W
    c0 = np.clip(qc - NA_WIN_COLS // 2, 0, GRID_W - NA_WIN_COLS)
    col_in = (kc[None, :] >= c0[:, None]) & (kc[None, :] < c0[:, None] + NA_WIN_COLS)
    col_off = np.clip(kc[None, :] - qc[:, None], -(NA_WIN_COLS - 1), NA_WIN_COLS - 1) + (NA_WIN_COLS - 1)
    del col_off
    row_offs, valids = [], []
    for rb, ks in pats:
        qr = rb + np.arange(NA_QROWS)
        kr = ks + np.arange(NA_KROWS)
        r0 = np.clip(qr - NA_WIN_ROWS // 2, 0, R - NA_WIN_ROWS)
        row_in = (kr[None, :] >= r0[:, None]) & (kr[None, :] < r0[:, None] + NA_WIN_ROWS)
        row_offs.append(np.where(row_in, kr[None, :] - qr[:, None] + (NA_WIN_ROWS - 1), -1))
        valids.append(row_in[qi][:, kj] & col_in)
    return np.stack(row_offs), np.stack(valids)


def _na_bias(rpb):
    row_off, valid = _na_bias_index()
    reach = GRID_W - NA_WIN_COLS
    padded = jnp.pad(rpb, ((0, 0), (0, 0), (0, 0), (reach, reach)))
    band = jnp.stack([padded[..., GRID_W - 1 - qc:2 * GRID_W - 1 - qc] for qc in range(GRID_W)], axis=-2)
    masked = jnp.zeros(band.shape[:2] + band.shape[3:], F32)
    pats = []
    for p in range(row_off.shape[0]):
        qrows = []
        for qi in range(NA_QROWS):
            blocks = [masked if row_off[p, qi, kj] < 0 else band[:, :, row_off[p, qi, kj]] for kj in range(NA_KROWS)]
            qrows.append(jnp.concatenate(blocks, axis=-1))
        pats.append(jnp.concatenate(qrows, axis=-2))
    b = jnp.stack(pats, axis=1)
    b = jnp.where(jnp.asarray(valid)[None, :, None], b, NEG)
    return jnp.concatenate([b, jnp.zeros(b.shape[:-1] + (CTX_LEN,), F32)], axis=-1)


def _rearranged_in_weights(w_in):
    z = lambda n: jnp.zeros((D_MODEL, n), F32)
    kr = w_in[:, 1664:1696]
    kr_e, kr_o = kr[:, 0::2], kr[:, 1::2]
    pad = MLA_HEAD_PAD - MLA_NOPE - MLA_ROPE
    krb = jnp.concatenate([z(MLA_NOPE), kr_e, kr_o, z(pad)], axis=1)
    krs = jnp.concatenate([z(MLA_NOPE), kr_o, kr_e, z(pad)], axis=1)
    wsm = jnp.concatenate([w_in[:, :1664], krb, krs], axis=1).astype(BF16)
    return wsm, w_in[:, 1696:].astype(BF16)


def _rearranged_mla_weights(w_uq, w_ukv):
    pad = MLA_HEAD_PAD - MLA_NOPE - MLA_ROPE
    wq = w_uq.reshape(MLA_Q_RANK, MLA_HEADS, MLA_NOPE + MLA_ROPE)
    nope, rope = wq[..., :MLA_NOPE], wq[..., MLA_NOPE:]
    r_e, r_o = rope[..., 0::2], rope[..., 1::2]
    zq = jnp.zeros((MLA_Q_RANK, MLA_HEADS, pad), F32)
    q_main = jnp.concatenate([nope, r_e, r_o, zq], axis=-1).reshape(MLA_Q_RANK, -1)
    q_swap = jnp.concatenate([nope, r_o, r_e, zq], axis=-1).reshape(MLA_Q_RANK, -1)
    wuq2 = jnp.concatenate([q_main, q_swap], axis=1).astype(BF16)
    wkv = w_ukv.reshape(MLA_KV_RANK, MLA_HEADS, MLA_NOPE + MLA_V)
    k_nope, v = wkv[..., :MLA_NOPE], wkv[..., MLA_NOPE:]
    zk = jnp.zeros((MLA_KV_RANK, MLA_HEADS, MLA_HEAD_PAD - MLA_NOPE), F32)
    k_main = jnp.concatenate([k_nope, zk], axis=-1).reshape(MLA_KV_RANK, -1)
    wukv2 = jnp.concatenate([k_main, v.reshape(MLA_KV_RANK, -1)], axis=1).astype(BF16)
    return wuq2, wukv2


def _route(aff, with_ctx):
    a = aff.transpose(0, 2, 1)
    g_l, i_l = lax.top_k(a[:, :, :SEQ], CAP_LAT)
    if not with_ctx:
        return [i_l], [g_l]
    g_c, i_c = lax.top_k(a[:, :, SEQ:], CAP_CTX)
    return [i_l, i_c + SEQ], [g_l, g_c]


def kernel(x, c, ctx, c_ctx, w_mod, b_mod, w_in, na_rpb, pool_w, pool_scale, mla_q_norm, mla_w_uq, mla_kv_norm,
           mla_w_ukv, w_branch, w_out, ln1_g, ln1_b, w_router, w_gate, w_up, w_down, ln2_g, ln2_b):
    B = x.shape[0]
    assert x.shape == (B, SEQ, D_MODEL) and ctx.shape == (B, CTX_LEN, D_MODEL) and B + 1 <= 16

    cc = jnp.concatenate([c, c_ctx[None], jnp.zeros((16 - B - 1, D_MODEL), F32)], axis=0)
    mod = _modulation(cc, w_mod, b_mod).reshape(DEPTH, 16, 6, D_MODEL)
    cos_t, sin_t = _rope_tables()
    cmat, smat, bd = _dft_tables()
    na_bias = _na_bias(na_rpb)
    hall = jnp.concatenate([x, ctx], axis=1)

    for l in range(DEPTH):
        with_ctx = l < DEPTH - 1
        n_tiles = N_TILES if with_ctx else N_LAT_TILES
        mod_all = jnp.stack([mod[l, :B], jnp.broadcast_to(mod[l, B], (B, 6, D_MODEL))], axis=1)
        wsm, wgates = _rearranged_in_weights(w_in[l])
        wuq2, wukv2 = _rearranged_mla_weights(mla_w_uq[l], mla_w_ukv[l])
        qa, ka, va, up, xcs, qm, km, vm = _inproj(
            hall, mod_all, wsm, mla_q_norm[l][None], mla_kv_norm[l][None], wuq2, wukv2, cos_t, sin_t, bd)
        ya = _na_attention(qa, ka, va, na_bias, l, n_tiles)
        pool_bd = jax.scipy.linalg.block_diag(*[pool_w[l, g] for g in range(POOL_GROUPS)]).astype(BF16)
        yb = _pool(up, pool_bd, pool_scale[l][None])
        yc = _dft_positions(cmat, smat, xcs)
        yd = _mla_attention(qm, km, vm, n_tiles)
        wr = jnp.pad(w_router[l], ((0, 0), (0, ROUTER_PAD - N_EXPERTS)))
        wr_hi = wr.astype(BF16)
        wr_cat = jnp.concatenate([wr_hi, (wr - wr_hi.astype(F32)).astype(BF16)], axis=1)
        h1, u2, aff = _merge(hall, mod_all, ya, yb, yc, yd, wgates, w_branch[l].astype(BF16), w_out[l].astype(BF16),
                             ln1_g[l][None], ln1_b[l][None], wr_cat, n_tiles)
        idx_sets, gate_sets = _route(aff[:, :, :N_EXPERTS], with_ctx)
        idx = jnp.concatenate(idx_sets, axis=-1)
        cap = idx.shape[-1]
        xe = _gather_tokens(idx[..., None], u2)
        gcol = jnp.concatenate(gate_sets, axis=-1).transpose(1, 0, 2).reshape(N_EXPERTS, B * cap, 1)
        ye = _experts(xe, gcol, w_gate, w_up, w_down, l)
        hall = _scatter_ln(h1, mod_all, ye.reshape(N_EXPERTS, B, cap, D_MODEL), ln2_g[l][None], ln2_b[l][None],
                           [i.reshape(B, 1, -1) for i in idx_sets])
    return hall
```

```python
import functools

import numpy as np
import jax
import jax.numpy as jnp
from jax import lax
from jax.experimental import pallas as pl
from jax.experimental.pallas import tpu as pltpu

F32 = jnp.float32
BF16 = jnp.bfloat16

D_MODEL = 1024
DEPTH = 4
GRID_W = 64
SEQ = 2048
CTX_LEN = 256
T_ALL = SEQ + CTX_LEN
HEAD_DIM = 64
NA_HEADS = 4
NA_WIN_ROWS = 8
NA_WIN_COLS = 16
NA_SCALE = HEAD_DIM ** -0.5
POOL_GROUPS = 4
POOL_CH = 64
POOL_WINDOWS = (2, 4, 8, 16)
FFT_GROUPS = 4
FFT_CH = 64
MLA_HEADS = 4
MLA_Q_RANK = 256
MLA_KV_RANK = 128
MLA_NOPE = 64
MLA_ROPE = 32
MLA_V = 64
MLA_SCALE = (MLA_NOPE + MLA_ROPE) ** -0.5
ROPE_BASE = 10000.0
BRANCH_W = 256
N_BRANCH = 4
N_EXPERTS = 16
EXPERT_FF = 2048
EC_CAPACITY = 2
DN_ALPHA = (2 * DEPTH) ** 0.25
LN_EPS = 1e-5
NEG = -1e30

LANE = 128
TM = 256
N_TILES = T_ALL // TM
N_LAT_TILES = SEQ // TM
MLA_HEAD_PAD = 128
NA_QROWS = TM // GRID_W
NA_KROWS = 12
NA_KWIN = NA_KROWS * GRID_W
NA_KALL = NA_KWIN + CTX_LEN
CAP_LAT = EC_CAPACITY * SEQ // N_EXPERTS
CAP_CTX = EC_CAPACITY * CTX_LEN // N_EXPERTS
CAP_ALL = CAP_LAT + CAP_CTX
ROUTER_PAD = 128
FF_TILE = 512
XROW_CHUNK = 576
VMEM_LIMIT = 56 * 1024 * 1024

C_QA, C_KA, C_VA, C_UP, C_UF, C_CQ, C_CKV, C_KR, C_KRS, C_END = 0, 256, 512, 768, 1024, 1280, 1536, 1664, 1792, 1920


def _cparams(sem):
    return pltpu.CompilerParams(dimension_semantics=sem, vmem_limit_bytes=VMEM_LIMIT)


def _dot(a, b):
    return jnp.dot(a, b, preferred_element_type=F32)


def _dot_t(a, b):
    return lax.dot_general(a, b, (((1,), (1,)), ((), ())), preferred_element_type=F32)


def _layer_norm(x, g, b):
    mu = jnp.mean(x, axis=-1, keepdims=True)
    xc = x - mu
    var = jnp.mean(xc * xc, axis=-1, keepdims=True)
    return xc * lax.rsqrt(var + LN_EPS) * g + b


def _mod_kernel(c_ref, w_ref, b_ref, o_ref):
    c = c_ref[...]
    s = c * jax.nn.sigmoid(c)
    w = w_ref[0]
    s_hi = s.astype(BF16)
    s_lo = (s - s_hi.astype(F32)).astype(BF16)
    w_hi = w.astype(BF16)
    w_lo = (w - w_hi.astype(F32)).astype(BF16)
    o_ref[0] = _dot(s_hi, w_hi) + _dot(s_hi, w_lo) + _dot(s_lo, w_hi) + b_ref[0]


def _modulation(cc, w_mod, b_mod):
    rows = cc.shape[0]
    n6 = 6 * D_MODEL
    tn = 1536
    return pl.pallas_call(
        _mod_kernel,
        grid=(DEPTH, n6 // tn),
        in_specs=[
            pl.BlockSpec((rows, D_MODEL), lambda l, j: (0, 0)),
            pl.BlockSpec((1, D_MODEL, tn), lambda l, j: (l, 0, j)),
            pl.BlockSpec((1, 1, tn), lambda l, j: (l, 0, j)),
        ],
        out_specs=pl.BlockSpec((1, rows, tn), lambda l, j: (l, 0, j)),
        out_shape=jax.ShapeDtypeStruct((DEPTH, rows, n6), F32),
        compiler_params=_cparams(("arbitrary", "arbitrary")),
    )(cc, w_mod, b_mod.reshape(DEPTH, 1, n6))


def _inproj_kernel(h_ref, mod_ref, wsm_ref, qn_ref, kvn_ref, wuq_ref, wukv_ref, cos_ref, sin_ref, bd_ref,
                   qa_ref, ka_ref, va_ref, up_ref, xcs_ref, qm_ref, km_ref, vm_ref):
    pb = h_ref.shape[0]

    def put(ref, val):
        ref[...] = val.astype(ref.dtype).reshape(ref.shape)

    sh1 = mod_ref[:, 0, 0:1, :]
    sc1 = mod_ref[:, 0, 1:2, :]
    u = (h_ref[...] * (1.0 + sc1) + sh1).astype(BF16).reshape(pb * TM, D_MODEL)
    z = _dot(u, wsm_ref[...])
    put(qa_ref, z[:, C_QA:C_KA] * NA_SCALE)
    put(ka_ref, z[:, C_KA:C_VA])
    put(va_ref, z[:, C_VA:C_UP])
    put(up_ref, z[:, C_UP:C_UF])
    put(xcs_ref, _dot(z[:, C_UF:C_CQ].astype(BF16), bd_ref[...]))

    cos = jnp.concatenate([cos_ref[...]] * pb, axis=0)
    sin = jnp.concatenate([sin_ref[...]] * pb, axis=0)
    cos4 = jnp.concatenate([cos] * MLA_HEADS, axis=-1)
    sin4 = jnp.concatenate([sin] * MLA_HEADS, axis=-1)

    cq = z[:, C_CQ:C_CKV]
    nq = cq * lax.rsqrt(jnp.mean(cq * cq, axis=-1, keepdims=True) + LN_EPS) * qn_ref[...]
    q2 = _dot(nq.astype(BF16), wuq_ref[...])
    hw = MLA_HEADS * MLA_HEAD_PAD
    put(qm_ref, (q2[:, :hw] * cos4 + q2[:, hw:] * sin4) * MLA_SCALE)

    ckv = z[:, C_CKV:C_KR]
    nkv = ckv * lax.rsqrt(jnp.mean(ckv * ckv, axis=-1, keepdims=True) + LN_EPS) * kvn_ref[...]
    kv2 = _dot(nkv.astype(BF16), wukv_ref[...])
    kr = z[:, C_KR:C_KRS] * cos + z[:, C_KRS:C_END] * sin
    put(km_ref, kv2[:, :hw] + jnp.concatenate([kr] * MLA_HEADS, axis=-1))
    put(vm_ref, kv2[:, hw:])


def _samples_per_step(B):
    return 2 if B % 2 == 0 else 1


def _resident(shape):
    return pl.BlockSpec(shape, lambda *_: (0,) * len(shape), pipeline_mode=pl.Buffered(1))


def _inproj(hall, mod_all, wsm, qn, kvn, wuq2, wukv2, cos_t, sin_t, bd):
    B = hall.shape[0]
    pb = _samples_per_step(B)
    hw = MLA_HEADS * MLA_HEAD_PAD
    row = lambda b, t: (b, t, 0)
    outs = [
        (256, BF16), (256, BF16), (256, BF16), (256, F32), (512, BF16), (hw, BF16), (hw, BF16), (MLA_HEADS * MLA_V, BF16),
    ]
    return pl.pallas_call(
        _inproj_kernel,
        grid=(B // pb, N_TILES),
        in_specs=[
            pl.BlockSpec((pb, TM, D_MODEL), row),
            pl.BlockSpec((pb, 1, 6, D_MODEL), lambda b, t: (b, t // N_LAT_TILES, 0, 0)),
            _resident(wsm.shape),
            _resident(qn.shape),
            _resident(kvn.shape),
            _resident(wuq2.shape),
            _resident(wukv2.shape),
            pl.BlockSpec((TM, LANE), lambda b, t: (t, 0)),
            pl.BlockSpec((TM, LANE), lambda b, t: (t, 0)),
            _resident(bd.shape),
        ],
        out_specs=[pl.BlockSpec((pb, TM, w), row) for w, _ in outs],
        out_shape=[jax.ShapeDtypeStruct((B, T_ALL, w), dt) for w, dt in outs],
        compiler_params=_cparams(("arbitrary", "arbitrary")),
    )(hall, mod_all, wsm, qn, kvn, wuq2, wukv2, cos_t, sin_t, bd)


def _softmax_pv(s, v):
    m = jnp.max(s, axis=-1, keepdims=True)
    p = jnp.exp(s - m)
    l = jnp.sum(p, axis=-1, keepdims=True)
    return _dot(p.astype(BF16), v) / l


def _na_kernel(q_ref, k_ref, v_ref, bias_ref, o_ref):
    t = pl.program_id(0)

    @pl.when(t < N_LAT_TILES)
    def _():
        ks = pl.multiple_of(jnp.clip(t * NA_QROWS - NA_WIN_ROWS // 2, 0, SEQ // GRID_W - NA_KROWS) * GRID_W, GRID_W)
        kk = jnp.concatenate([k_ref[0, pl.ds(ks, NA_KWIN), :], k_ref[0, SEQ:T_ALL, :]], axis=0)
        vv = jnp.concatenate([v_ref[0, pl.ds(ks, NA_KWIN), :], v_ref[0, SEQ:T_ALL, :]], axis=0)
        q = q_ref[0]
        sls = [slice(h * HEAD_DIM, (h + 1) * HEAD_DIM) for h in range(NA_HEADS)]
        ss = [_dot_t(q[:, sl], kk[:, sl]) + bias_ref[0, 0, h] for h, sl in enumerate(sls)]
        ms = [jnp.max(s, axis=-1, keepdims=True) for s in ss]
        ps = [jnp.exp(s - m).astype(BF16) for s, m in zip(ss, ms)]
        ones = _onehot(lax.broadcasted_iota(jnp.int32, (NA_KALL, HEAD_DIM), 1) == 0)
        oa = [_dot(p, jnp.concatenate([vv[:, sl], ones], axis=-1)) for p, sl in zip(ps, sls)]
        os_ = [o[:, :HEAD_DIM] / o[:, HEAD_DIM:HEAD_DIM + 1] for o in oa]
        o_ref[0] = jnp.concatenate(os_, axis=-1).astype(o_ref.dtype)

    @pl.when(t == N_LAT_TILES)
    def _():
        kk = k_ref[0, SEQ:T_ALL, :]
        vv = v_ref[0, SEQ:T_ALL, :]
        q = q_ref[0]
        for h in range(NA_HEADS):
            sl = slice(h * HEAD_DIM, (h + 1) * HEAD_DIM)
            o_ref[0, :, sl] = _softmax_pv(_dot_t(q[:, sl], kk[:, sl]), vv[:, sl]).astype(o_ref.dtype)


def _na_attention(qa, ka, va, bias, layer, n_tiles):
    B = qa.shape[0]

    def bias_idx(t, b):
        return (layer, jnp.where(t == 0, 0, jnp.where(t >= N_LAT_TILES - 1, 2, 1)), 0, 0, 0)

    return pl.pallas_call(
        _na_kernel,
        grid=(n_tiles, B),
        in_specs=[
            pl.BlockSpec((1, TM, 256), lambda t, b: (b, t, 0)),
            pl.BlockSpec((1, T_ALL, 256), lambda t, b: (b, 0, 0)),
            pl.BlockSpec((1, T_ALL, 256), lambda t, b: (b, 0, 0)),
            pl.BlockSpec((1, 1, NA_HEADS, TM, NA_KALL), bias_idx),
        ],
        out_specs=pl.BlockSpec((1, TM, 256), lambda t, b: (b, t, 0)),
        out_shape=jax.ShapeDtypeStruct((B, n_tiles * TM, 256), BF16),
        compiler_params=_cparams(("arbitrary", "arbitrary")),
    )(qa, ka, va, bias)


def _mla_kernel(q_ref, k_ref, v_ref, o_ref):
    t = pl.program_id(1)

    def run(key_lo, n_keys):
        keys = slice(key_lo, key_lo + n_keys)
        ksls = [slice(h * MLA_HEAD_PAD, (h + 1) * MLA_HEAD_PAD) for h in range(MLA_HEADS)]
        vsls = [slice(h * MLA_V, (h + 1) * MLA_V) for h in range(MLA_HEADS)]
        ss = [_dot_t(q_ref[0, :, ksl], k_ref[0, keys, ksl]) for ksl in ksls]
        ms = [jnp.max(s, axis=-1, keepdims=True) for s in ss]
        ps = [jnp.exp(s - m).astype(BF16) for s, m in zip(ss, ms)]
        ones = _onehot(lax.broadcasted_iota(jnp.int32, (n_keys, MLA_V), 1) == 0)
        oa = [_dot(p, jnp.concatenate([v_ref[0, keys, vsl], ones], axis=-1)) for p, vsl in zip(ps, vsls)]
        os_ = [o[:, :MLA_V] / o[:, MLA_V:MLA_V + 1] for o in oa]
        o_ref[0] = jnp.concatenate(os_, axis=-1).astype(o_ref.dtype)

    @pl.when(t < N_LAT_TILES)
    def _():
        run(0, T_ALL)

    @pl.when(t == N_LAT_TILES)
    def _():
        run(SEQ, CTX_LEN)


def _mla_attention(qm, km, vm, n_tiles):
    B = qm.shape[0]
    hw = MLA_HEADS * MLA_HEAD_PAD
    vw = MLA_HEADS * MLA_V
    return pl.pallas_call(
        _mla_kernel,
        grid=(B, n_tiles),
        in_specs=[
            pl.BlockSpec((1, TM, hw), lambda b, t: (b, t, 0)),
            pl.BlockSpec((1, T_ALL, hw), lambda b, t: (b, 0, 0)),
            pl.BlockSpec((1, T_ALL, vw), lambda b, t: (b, 0, 0)),
        ],
        out_specs=pl.BlockSpec((1, TM, vw), lambda b, t: (b, t, 0)),
        out_shape=jax.ShapeDtypeStruct((B, n_tiles * TM, vw), BF16),
        compiler_params=_cparams(("arbitrary", "arbitrary")),
    )(qm, km, vm)


POOL_PAD = POOL_WINDOWS[-1] // 2


def _pool_kernel(u_ref, w_ref, scale_ref, o_ref, pad_ref):
    width = POOL_GROUPS * POOL_CH
    lane = lax.broadcasted_iota(jnp.int32, (1, width), 1)
    half = jnp.left_shift(1, lane // POOL_CH)
    zeros = jnp.zeros((POOL_PAD, width), F32)
    base = 0
    for lo, n in ((0, SEQ), (SEQ, CTX_LEN)):
        x = u_ref[0, lo:lo + n, :]
        pad_ref[base:base + POOL_PAD, :] = zeros
        pad_ref[base + POOL_PAD:base + POOL_PAD + n, :] = x
        pad_ref[base + POOL_PAD + n:base + 2 * POOL_PAD + n, :] = zeros

        def shifted(d):
            return pad_ref[base + POOL_PAD + d:base + POOL_PAD + d + n, :]

        sums = {}
        acc = shifted(-1) + x
        sums[1] = acc
        for h in (2, 4, 8):
            for d in list(range(-h, -h // 2)) + list(range(h // 2, h)):
                acc = acc + shifted(d)
            sums[h] = acc
        win = jnp.where(half == 1, sums[1], jnp.where(half == 2, sums[2], jnp.where(half == 4, sums[4], sums[8])))
        t = lax.broadcasted_iota(jnp.int32, (n, 1), 0)
        cnt = jnp.minimum(t + half, n) - jnp.maximum(t - half, 0)
        y = (win / cnt.astype(F32) - x).astype(BF16)
        o_ref[0, lo:lo + n, :] = (_dot(y, w_ref[...]) * scale_ref[...]).astype(o_ref.dtype)
        base += n + 2 * POOL_PAD


def _pool(up, w_bd, scale):
    B = up.shape[0]
    return pl.pallas_call(
        _pool_kernel,
        grid=(B,),
        in_specs=[
            pl.BlockSpec((1, T_ALL, 256), lambda b: (b, 0, 0)),
            pl.BlockSpec((256, 256), lambda b: (0, 0)),
            pl.BlockSpec((1, 256), lambda b: (0, 0)),
        ],
        out_specs=pl.BlockSpec((1, T_ALL, 256), lambda b: (b, 0, 0)),
        out_shape=jax.ShapeDtypeStruct((B, T_ALL, 256), BF16),
        scratch_shapes=[pltpu.VMEM((T_ALL + 4 * POOL_PAD, POOL_GROUPS * POOL_CH), F32)],
        compiler_params=_cparams(("arbitrary",)),
    )(up, w_bd, scale)


DFT_ROWS = 768


def _dft_kernel(c_ref, s_ref, x_ref, o_ref):
    x = x_ref[0]
    w = FFT_GROUPS * FFT_CH
    o_ref[0] = (_dot(c_ref[...], x[:, :w]) + _dot(s_ref[...], x[:, w:])).astype(o_ref.dtype)


def _dft_positions(cmat, smat, xcs):
    B = xcs.shape[0]
    return pl.pallas_call(
        _dft_kernel,
        grid=(T_ALL // DFT_ROWS, B),
        in_specs=[
            pl.BlockSpec((DFT_ROWS, T_ALL), lambda i, b: (i, 0)),
            pl.BlockSpec((DFT_ROWS, T_ALL), lambda i, b: (i, 0)),
            pl.BlockSpec((1, T_ALL, 512), lambda i, b: (b, 0, 0)),
        ],
        out_specs=pl.BlockSpec((1, DFT_ROWS, 256), lambda i, b: (b, i, 0)),
        out_shape=jax.ShapeDtypeStruct((B, T_ALL, 256), BF16),
        compiler_params=_cparams(("arbitrary", "arbitrary")),
    )(cmat, smat, xcs)


def _merge_kernel(h_ref, mod_ref, ya_ref, yb_ref, yc_ref, yd_ref, wg_ref, wbr_ref, wout_ref, g_ref, b_ref, wr_ref,
                  h1_ref, u2_ref, aff_ref):
    pb = h_ref.shape[0]
    rows = pb * TM
    h = h_ref[...]
    sh1 = mod_ref[:, 0, 0:1, :]
    sc1 = mod_ref[:, 0, 1:2, :]
    g1 = mod_ref[:, 0, 2:3, :]
    sh2 = mod_ref[:, 0, 3:4, :]
    sc2 = mod_ref[:, 0, 4:5, :]
    u = (h * (1.0 + sc1) + sh1).astype(BF16).reshape(rows, D_MODEL)
    ys = (ya_ref, yb_ref, yc_ref, yd_ref)
    acc = jnp.zeros((rows, D_MODEL), F32)
    for n in range(N_BRANCH):
        gate = jax.nn.sigmoid(_dot(u, wg_ref[:, n * D_MODEL:(n + 1) * D_MODEL]))
        acc = acc + gate * _dot(ys[n][...].reshape(rows, BRANCH_W), wbr_ref[n])
    mix = _dot(acc.astype(BF16), wout_ref[...]).reshape(pb, TM, D_MODEL)
    h1 = _layer_norm(DN_ALPHA * h + g1 * mix, g_ref[...], b_ref[...])
    h1_ref[...] = h1
    u2 = h1 * (1.0 + sc2) + sh2
    u2_hi = u2.astype(BF16)
    u2_ref[...] = u2_hi
    u2_lo = (u2 - u2_hi.astype(F32)).astype(BF16).reshape(rows, D_MODEL)
    hi_terms = _dot(u2_hi.reshape(rows, D_MODEL), wr_ref[...])
    logits = hi_terms[:, :ROUTER_PAD] + hi_terms[:, ROUTER_PAD:] + _dot(u2_lo, wr_ref[:, :ROUTER_PAD])
    lane = lax.broadcasted_iota(jnp.int32, (1, ROUTER_PAD), 1)
    logits = jnp.where(lane < N_EXPERTS, logits, NEG)
    e = jnp.exp(logits - jnp.max(logits, axis=-1, keepdims=True))
    aff_ref[...] = (e / jnp.sum(e, axis=-1, keepdims=True)).reshape(pb, TM, ROUTER_PAD)


def _merge(hall, mod_all, ya, yb, yc, yd, wg, wbr, wout, ln_g, ln_b, wr_cat, n_tiles):
    B = hall.shape[0]
    pb = _samples_per_step(B)
    row = lambda b, t: (b, t, 0)
    t_rows = n_tiles * TM
    return pl.pallas_call(
        _merge_kernel,
        grid=(B // pb, n_tiles),
        in_specs=[
            pl.BlockSpec((pb, TM, D_MODEL), row),
            pl.BlockSpec((pb, 1, 6, D_MODEL), lambda b, t: (b, t // N_LAT_TILES, 0, 0)),
            pl.BlockSpec((pb, TM, BRANCH_W), row),
            pl.BlockSpec((pb, TM, BRANCH_W), row),
            pl.BlockSpec((pb, TM, BRANCH_W), row),
            pl.BlockSpec((pb, TM, BRANCH_W), row),
            _resident(wg.shape),
            _resident(wbr.shape),
            _resident(wout.shape),
            _resident((1, D_MODEL)),
            _resident((1, D_MODEL)),
            _resident(wr_cat.shape),
        ],
        out_specs=[
            pl.BlockSpec((pb, TM, D_MODEL), row),
            pl.BlockSpec((pb, TM, D_MODEL), row),
            pl.BlockSpec((pb, TM, ROUTER_PAD), row),
        ],
        out_shape=[
            jax.ShapeDtypeStruct((B, t_rows, D_MODEL), F32),
            jax.ShapeDtypeStruct((B, t_rows, D_MODEL), BF16),
            jax.ShapeDtypeStruct((B, t_rows, ROUTER_PAD), F32),
        ],
        compiler_params=_cparams(("arbitrary", "arbitrary")),
    )(hall, mod_all, ya, yb, yc, yd, wg, wbr, wout, ln_g, ln_b, wr_cat)


def _onehot(match):
    return jnp.where(match, 1.0, 0.0).astype(BF16)


def _gather_kernel(idx_ref, u_ref, o_ref):
    idx = idx_ref[0, 0]
    tok = lax.broadcasted_iota(jnp.int32, (CAP_LAT, SEQ), 1)
    o_ref[0, :CAP_LAT, :] = _dot(_onehot(tok == idx[:CAP_LAT]), u_ref[0, :SEQ, :]).astype(o_ref.dtype)
    if idx.shape[0] > CAP_LAT:
        tok = lax.broadcasted_iota(jnp.int32, (CAP_CTX, CTX_LEN), 1) + SEQ
        o_ref[0, CAP_LAT:, :] = _dot(_onehot(tok == idx[CAP_LAT:]), u_ref[0, SEQ:, :]).astype(o_ref.dtype)


def _gather_tokens(idx_col, u2):
    B, t_rows = u2.shape[:2]
    cap = idx_col.shape[2]
    return pl.pallas_call(
        _gather_kernel,
        grid=(B, N_EXPERTS),
        in_specs=[
            pl.BlockSpec((1, 1, cap, 1), lambda b, e: (b, e, 0, 0)),
            pl.BlockSpec((1, t_rows, D_MODEL), lambda b, e: (b, 0, 0)),
        ],
        out_specs=pl.BlockSpec((1, cap, D_MODEL), lambda b, e: (e, b, 0)),
        out_shape=jax.ShapeDtypeStruct((N_EXPERTS, B * cap, D_MODEL), BF16),
        compiler_params=_cparams(("arbitrary", "arbitrary")),
    )(idx_col, u2)


def _expert_kernel(x_ref, g_ref, wg_ref, wu_ref, wd_ref, o_ref, acc_ref, *, chunk):
    f = pl.program_id(1)

    @pl.when(f == 0)
    def _():
        acc_ref[...] = jnp.zeros_like(acc_ref)

    wg = wg_ref[0, 0].astype(BF16)
    wu = wu_ref[0, 0].astype(BF16)
    wd = wd_ref[0, 0].astype(BF16)
    for c in range(x_ref.shape[1] // chunk):
        rows = pl.ds(c * chunk, chunk)
        x = x_ref[0, rows, :]
        a = _dot(x, wg)
        u = _dot(x, wu)
        acc_ref[rows, :] += _dot((a * jax.nn.sigmoid(a) * u).astype(BF16), wd)

    @pl.when(f == pl.num_programs(1) - 1)
    def _():
        o_ref[0] = (acc_ref[...] * g_ref[0]).astype(o_ref.dtype)


def _experts(xe, gcol, w_gate, w_up, w_down, layer):
    rows = xe.shape[1]
    chunk = next(c for c in (XROW_CHUNK, 512, CAP_ALL, CAP_LAT) if rows % c == 0)
    return pl.pallas_call(
        functools.partial(_expert_kernel, chunk=chunk),
        grid=(N_EXPERTS, EXPERT_FF // FF_TILE),
        in_specs=[
            pl.BlockSpec((1, rows, D_MODEL), lambda e, f: (e, 0, 0)),
            pl.BlockSpec((1, rows, 1), lambda e, f: (e, 0, 0)),
            pl.BlockSpec((1, 1, D_MODEL, FF_TILE), lambda e, f: (layer, e, 0, f)),
            pl.BlockSpec((1, 1, D_MODEL, FF_TILE), lambda e, f: (layer, e, 0, f)),
            pl.BlockSpec((1, 1, FF_TILE, D_MODEL), lambda e, f: (layer, e, f, 0)),
        ],
        out_specs=pl.BlockSpec((1, rows, D_MODEL), lambda e, f: (e, 0, 0)),
        out_shape=jax.ShapeDtypeStruct((N_EXPERTS, rows, D_MODEL), BF16),
        scratch_shapes=[pltpu.VMEM((rows, D_MODEL), F32)],
        compiler_params=_cparams(("arbitrary", "arbitrary")),
    )(xe, gcol, w_gate, w_up, w_down)


def _scatter_kernel(h_ref, mod_ref, y_ref, g_ref, b_ref, *rest):
    idx_refs, o_ref = rest[:-1], rest[-1]
    t = pl.program_id(1)

    def scatter(idx_ref, slot_lo, slot_hi):
        n = N_EXPERTS * (slot_hi - slot_lo)
        tok = lax.broadcasted_iota(jnp.int32, (TM, n), 0) + t * TM
        y = y_ref[:, 0, slot_lo:slot_hi, :].reshape(n, D_MODEL)
        fx = _dot(_onehot(tok == idx_ref[0]), y)
        g2 = mod_ref[0, 0, 5:6, :]
        o_ref[0] = _layer_norm(DN_ALPHA * h_ref[0] + g2 * fx, g_ref[...], b_ref[...])

    if len(idx_refs) == 1:
        scatter(idx_refs[0], 0, CAP_LAT)
    else:
        pl.when(t < N_LAT_TILES)(lambda: scatter(idx_refs[0], 0, CAP_LAT))
        pl.when(t == N_LAT_TILES)(lambda: scatter(idx_refs[1], CAP_LAT, CAP_ALL))


def _scatter_ln(h1, mod_all, ye, ln_g, ln_b, idx_rows):
    B = h1.shape[0]
    cap = ye.shape[2]
    n_tiles = N_LAT_TILES if len(idx_rows) == 1 else N_TILES
    return pl.pallas_call(
        _scatter_kernel,
        grid=(B, n_tiles),
        in_specs=[
            pl.BlockSpec((1, TM, D_MODEL), lambda b, t: (b, t, 0)),
            pl.BlockSpec((1, 1, 6, D_MODEL), lambda b, t: (b, t // N_LAT_TILES, 0, 0)),
            pl.BlockSpec((N_EXPERTS, 1, cap, D_MODEL), lambda b, t: (0, b, 0, 0)),
            pl.BlockSpec((1, D_MODEL), lambda b, t: (0, 0)),
            pl.BlockSpec((1, D_MODEL), lambda b, t: (0, 0)),
        ] + [pl.BlockSpec((1, 1, r.shape[2]), lambda b, t: (b, 0, 0)) for r in idx_rows],
        out_specs=pl.BlockSpec((1, TM, D_MODEL), lambda b, t: (b, t, 0)),
        out_shape=jax.ShapeDtypeStruct((B, n_tiles * TM, D_MODEL), F32),
        compiler_params=_cparams(("arbitrary", "arbitrary")),
    )(h1, mod_all, ye, ln_g, ln_b, *idx_rows)


def _rope_tables():
    n_freq = MLA_ROPE // 4
    inv = ROPE_BASE ** (-jnp.arange(n_freq, dtype=F32) / n_freq)
    t = jnp.arange(SEQ)
    row = (t // GRID_W).astype(F32)
    col = (t % GRID_W).astype(F32)
    ang = jnp.concatenate([row[:, None] * inv, col[:, None] * inv], axis=-1)
    cos, sin = jnp.cos(ang), jnp.sin(ang)
    ones = jnp.ones((SEQ, MLA_NOPE), F32)
    pad1 = jnp.ones((SEQ, MLA_HEAD_PAD - MLA_NOPE - MLA_ROPE), F32)
    cos_t = jnp.concatenate([ones, cos, cos, pad1], axis=-1)
    sin_t = jnp.concatenate([0 * ones, -sin, sin, 0 * pad1], axis=-1)
    cos_t = jnp.concatenate([cos_t, jnp.ones((CTX_LEN, MLA_HEAD_PAD), F32)], axis=0)
    sin_t = jnp.concatenate([sin_t, jnp.zeros((CTX_LEN, MLA_HEAD_PAD), F32)], axis=0)
    return cos_t, sin_t


def _dft_tables():
    def block(n):
        split = 64
        k = jnp.arange(n, dtype=jnp.int32)[None, :]
        jh = jnp.arange(n // split, dtype=jnp.int32)[:, None]
        jl = jnp.arange(split, dtype=jnp.int32)[:, None]
        ang_h = ((jh * split * k) % n).astype(F32) * (2.0 * np.pi / n)
        ang_l = ((jl * k) % n).astype(F32) * (2.0 * np.pi / n)
        ch, sh = jnp.cos(ang_h)[:, None, :], jnp.sin(ang_h)[:, None, :]
        cl_, sl_ = jnp.cos(ang_l)[None, :, :], jnp.sin(ang_l)[None, :, :]
        sc = 1.0 / np.sqrt(n * FFT_CH)
        return ((ch * cl_ - sh * sl_) * sc).reshape(n, n), (-(sh * cl_ + ch * sl_) * sc).reshape(n, n)

    cl, sl = block(SEQ)
    cc, sc_ = block(CTX_LEN)

    def diag(a, b):
        top = jnp.concatenate([a, jnp.zeros((SEQ, CTX_LEN), F32)], axis=1)
        bot = jnp.concatenate([jnp.zeros((CTX_LEN, SEQ), F32), b], axis=1)
        return jnp.concatenate([top, bot], axis=0).astype(BF16)

    j = jnp.arange(FFT_CH, dtype=jnp.int32)
    ang = ((j[:, None] * j[None, :]) % FFT_CH).astype(F32) * (2.0 * np.pi / FFT_CH)
    eye = jnp.eye(FFT_GROUPS, dtype=F32)
    bd = jnp.concatenate([jnp.kron(eye, jnp.cos(ang)), jnp.kron(eye, jnp.sin(ang))], axis=1).astype(BF16)
    return diag(cl, cc), diag(sl, sc_), bd


def _na_bias_index():
    R = SEQ // GRID_W
    pats = [(0, 0), (NA_QROWS, 0), (R - NA_QROWS, R - NA_KROWS)]
    q = np.arange(TM)
    k = np.arange(NA_KWIN)
    qi, qc = q // GRID_W, q % GRID_W
    kj, kc = k // GRID_W, k % GRID_W
    c0 = np.clip(qc - NA_WIN_COLS // 2, 0, GRID_W - NA_WIN_COLS)
    col_in = (kc[None, :] >= c0[:, None]) & (kc[None, :] < c0[:, None] + NA_WIN_COLS)
    col_off = np.clip(kc[None, :] - qc[:, None], -(NA_WIN_COLS - 1), NA_WIN_COLS - 1) + (NA_WIN_COLS - 1)
    del col_off
    row_offs, valids = [], []
    for rb, ks in pats:
        qr = rb + np.arange(NA_QROWS)
        kr = ks + np.arange(NA_KROWS)
        r0 = np.clip(qr - NA_WIN_ROWS // 2, 0, R - NA_WIN_ROWS)
        row_in = (kr[None, :] >= r0[:, None]) & (kr[None, :] < r0[:, None] + NA_WIN_ROWS)
        row_offs.append(np.where(row_in, kr[None, :] - qr[:, None] + (NA_WIN_ROWS - 1), -1))
        valids.append(row_in[qi][:, kj] & col_in)
    return np.stack(row_offs), np.stack(valids)


def _na_bias(rpb):
    row_off, valid = _na_bias_index()
    reach = GRID_W - NA_WIN_COLS
    padded = jnp.pad(rpb, ((0, 0), (0, 0), (0, 0), (reach, reach)))
    band = jnp.stack([padded[..., GRID_W - 1 - qc:2 * GRID_W - 1 - qc] for qc in range(GRID_W)], axis=-2)
    masked = jnp.zeros(band.shape[:2] + band.shape[3:], F32)
    pats = []
    for p in range(row_off.shape[0]):
        qrows = []
        for qi in range(NA_QROWS):
            blocks = [masked if row_off[p, qi, kj] < 0 else band[:, :, row_off[p, qi, kj]] for kj in range(NA_KROWS)]
            qrows.append(jnp.concatenate(blocks, axis=-1))
        pats.append(jnp.concatenate(qrows, axis=-2))
    b = jnp.stack(pats, axis=1)
    b = jnp.where(jnp.asarray(valid)[None, :, None], b, NEG)
    return jnp.concatenate([b, jnp.zeros(b.shape[:-1] + (CTX_LEN,), F32)], axis=-1)


def _rearranged_in_weights(w_in):
    z = lambda n: jnp.zeros((D_MODEL, n), F32)
    kr = w_in[:, 1664:1696]
    kr_e, kr_o = kr[:, 0::2], kr[:, 1::2]
    pad = MLA_HEAD_PAD - MLA_NOPE - MLA_ROPE
    krb = jnp.concatenate([z(MLA_NOPE), kr_e, kr_o, z(pad)], axis=1)
    krs = jnp.concatenate([z(MLA_NOPE), kr_o, kr_e, z(pad)], axis=1)
    wsm = jnp.concatenate([w_in[:, :1664], krb, krs], axis=1).astype(BF16)
    return wsm, w_in[:, 1696:].astype(BF16)


def _rearranged_mla_weights(w_uq, w_ukv):
    pad = MLA_HEAD_PAD - MLA_NOPE - MLA_ROPE
    wq = w_uq.reshape(MLA_Q_RANK, MLA_HEADS, MLA_NOPE + MLA_ROPE)
    nope, rope = wq[..., :MLA_NOPE], wq[..., MLA_NOPE:]
    r_e, r_o = rope[..., 0::2], rope[..., 1::2]
    zq = jnp.zeros((MLA_Q_RANK, MLA_HEADS, pad), F32)
    q_main = jnp.concatenate([nope, r_e, r_o, zq], axis=-1).reshape(MLA_Q_RANK, -1)
    q_swap = jnp.concatenate([nope, r_o, r_e, zq], axis=-1).reshape(MLA_Q_RANK, -1)
    wuq2 = jnp.concatenate([q_main, q_swap], axis=1).astype(BF16)
    wkv = w_ukv.reshape(MLA_KV_RANK, MLA_HEADS, MLA_NOPE + MLA_V)
    k_nope, v = wkv[..., :MLA_NOPE], wkv[..., MLA_NOPE:]
    zk = jnp.zeros((MLA_KV_RANK, MLA_HEADS, MLA_HEAD_PAD - MLA_NOPE), F32)
    k_main = jnp.concatenate([k_nope, zk], axis=-1).reshape(MLA_KV_RANK, -1)
    wukv2 = jnp.concatenate([k_main, v.reshape(MLA_KV_RANK, -1)], axis=1).astype(BF16)
    return wuq2, wukv2


def _route(aff, with_ctx):
    a = aff.transpose(0, 2, 1)
    g_l, i_l = lax.top_k(a[:, :, :SEQ], CAP_LAT)
    if not with_ctx:
        return [i_l], [g_l]
    g_c, i_c = lax.top_k(a[:, :, SEQ:], CAP_CTX)
    return [i_l, i_c + SEQ], [g_l, g_c]


def kernel(x, c, ctx, c_ctx, w_mod, b_mod, w_in, na_rpb, pool_w, pool_scale, mla_q_norm, mla_w_uq, mla_kv_norm,
           mla_w_ukv, w_branch, w_out, ln1_g, ln1_b, w_router, w_gate, w_up, w_down, ln2_g, ln2_b):
    B = x.shape[0]
    assert x.shape == (B, SEQ, D_MODEL) and ctx.shape == (B, CTX_LEN, D_MODEL) and B + 1 <= 16

    cc = jnp.concatenate([c, c_ctx[None], jnp.zeros((16 - B - 1, D_MODEL), F32)], axis=0)
    mod = _modulation(cc, w_mod, b_mod).reshape(DEPTH, 16, 6, D_MODEL)
    cos_t, sin_t = _rope_tables()
    cmat, smat, bd = _dft_tables()
    na_bias = _na_bias(na_rpb)
    hall = jnp.concatenate([x, ctx], axis=1)

    for l in range(DEPTH):
        with_ctx = l < DEPTH - 1
        n_tiles = N_TILES if with_ctx else N_LAT_TILES
        mod_all = jnp.stack([mod[l, :B], jnp.broadcast_to(mod[l, B], (B, 6, D_MODEL))], axis=1)
        wsm, wgates = _rearranged_in_weights(w_in[l])
        wuq2, wukv2 = _rearranged_mla_weights(mla_w_uq[l], mla_w_ukv[l])
        qa, ka, va, up, xcs, qm, km, vm = _inproj(
            hall, mod_all, wsm, mla_q_norm[l][None], mla_kv_norm[l][None], wuq2, wukv2, cos_t, sin_t, bd)
        ya = _na_attention(qa, ka, va, na_bias, l, n_tiles)
        pool_bd = jax.scipy.linalg.block_diag(*[pool_w[l, g] for g in range(POOL_GROUPS)]).astype(BF16)
        yb = _pool(up, pool_bd, pool_scale[l][None])
        yc = _dft_positions(cmat, smat, xcs)
        yd = _mla_attention(qm, km, vm, n_tiles)
        wr = jnp.pad(w_router[l], ((0, 0), (0, ROUTER_PAD - N_EXPERTS)))
        wr_hi = wr.astype(BF16)
        wr_cat = jnp.concatenate([wr_hi, (wr - wr_hi.astype(F32)).astype(BF16)], axis=1)
        h1, u2, aff = _merge(hall, mod_all, ya, yb, yc, yd, wgates, w_branch[l].astype(BF16), w_out[l].astype(BF16),
                             ln1_g[l][None], ln1_b[l][None], wr_cat, n_tiles)
        idx_sets, gate_sets = _route(aff[:, :, :N_EXPERTS], with_ctx)
        idx = jnp.concatenate(idx_sets, axis=-1)
        cap = idx.shape[-1]
        xe = _gather_tokens(idx[..., None], u2)
        gcol = jnp.concatenate(gate_sets, axis=-1).transpose(1, 0, 2).reshape(N_EXPERTS, B * cap, 1)
        ye = _experts(xe, gcol, w_gate, w_up, w_down, l)
        hall = _scatter_ln(h1, mod_all, ye.reshape(N_EXPERTS, B, cap, D_MODEL), ln2_g[l][None], ln2_b[l][None],
                           [i.reshape(B, 1, -1) for i in idx_sets])
    return hall
```

```python
import functools

import numpy as np
import jax
import jax.numpy as jnp
from jax import lax
from jax.experimental import pallas as pl
from jax.experimental.pallas import tpu as pltpu

F32 = jnp.float32
BF16 = jnp.bfloat16

D_MODEL = 1024
DEPTH = 4
GRID_W = 64
SEQ = 2048
CTX_LEN = 256
T_ALL = SEQ + CTX_LEN
HEAD_DIM = 64
NA_HEADS = 4
NA_WIN_ROWS = 8
NA_WIN_COLS = 16
NA_SCALE = HEAD_DIM ** -0.5
POOL_GROUPS = 4
POOL_CH = 64
POOL_WINDOWS = (2, 4, 8, 16)
FFT_GROUPS = 4
FFT_CH = 64
MLA_HEADS = 4
MLA_Q_RANK = 256
MLA_KV_RANK = 128
MLA_NOPE = 64
MLA_ROPE = 32
MLA_V = 64
MLA_SCALE = (MLA_NOPE + MLA_ROPE) ** -0.5
ROPE_BASE = 10000.0
BRANCH_W = 256
N_BRANCH = 4
N_EXPERTS = 16
EXPERT_FF = 2048
EC_CAPACITY = 2
DN_ALPHA = (2 * DEPTH) ** 0.25
LN_EPS = 1e-5
NEG = -1e30

LANE = 128
TM = 256
N_TILES = T_ALL // TM
N_LAT_TILES = SEQ // TM
MLA_HEAD_PAD = 128
NA_QROWS = TM // GRID_W
NA_KROWS = 12
NA_KWIN = NA_KROWS * GRID_W
NA_KALL = NA_KWIN + CTX_LEN
CAP_LAT = EC_CAPACITY * SEQ // N_EXPERTS
CAP_CTX = EC_CAPACITY * CTX_LEN // N_EXPERTS
CAP_ALL = CAP_LAT + CAP_CTX
ROUTER_PAD = 128
ROUTE_EXP_BITS = 11
ROUTE_MANTISSA_STEPS = 52
FF_TILE = 512
XROW_CHUNK = 576
VMEM_LIMIT = 56 * 1024 * 1024

C_QA, C_KA, C_VA, C_UP, C_UF, C_CQ, C_CKV, C_KR, C_KRS, C_END = 0, 256, 512, 768, 1024, 1280, 1536, 1664, 1792, 1920


def _cparams(sem):
    return pltpu.CompilerParams(dimension_semantics=sem, vmem_limit_bytes=VMEM_LIMIT)


def _dot(a, b):
    return jnp.dot(a, b, preferred_element_type=F32)


def _dot_t(a, b):
    return lax.dot_general(a, b, (((1,), (1,)), ((), ())), preferred_element_type=F32)


def _layer_norm(x, g, b):
    mu = jnp.mean(x, axis=-1, keepdims=True)
    xc = x - mu
    var = jnp.mean(xc * xc, axis=-1, keepdims=True)
    return xc * lax.rsqrt(var + LN_EPS) * g + b


def _mod_kernel(c_ref, w_ref, b_ref, o_ref):
    c = c_ref[...]
    s = c * jax.nn.sigmoid(c)
    w = w_ref[0]
    s_hi = s.astype(BF16)
    s_lo = (s - s_hi.astype(F32)).astype(BF16)
    w_hi = w.astype(BF16)
    w_lo = (w - w_hi.astype(F32)).astype(BF16)
    o_ref[0] = _dot(s_hi, w_hi) + _dot(s_hi, w_lo) + _dot(s_lo, w_hi) + b_ref[0]


def _modulation(cc, w_mod, b_mod):
    rows = cc.shape[0]
    n6 = 6 * D_MODEL
    tn = 1536
    return pl.pallas_call(
        _mod_kernel,
        grid=(DEPTH, n6 // tn),
        in_specs=[
            pl.BlockSpec((rows, D_MODEL), lambda l, j: (0, 0)),
            pl.BlockSpec((1, D_MODEL, tn), lambda l, j: (l, 0, j)),
            pl.BlockSpec((1, 1, tn), lambda l, j: (l, 0, j)),
        ],
        out_specs=pl.BlockSpec((1, rows, tn), lambda l, j: (l, 0, j)),
        out_shape=jax.ShapeDtypeStruct((DEPTH, rows, n6), F32),
        compiler_params=_cparams(("arbitrary", "arbitrary")),
    )(cc, w_mod, b_mod.reshape(DEPTH, 1, n6))


def _inproj_kernel(h_ref, mod_ref, wsm_ref, qn_ref, kvn_ref, wuq_ref, wukv_ref, cos_ref, sin_ref, bd_ref,
                   qa_ref, ka_ref, va_ref, up_ref, xcs_ref, qm_ref, km_ref, vm_ref):
    pb = h_ref.shape[0]

    def put(ref, val):
        ref[...] = val.astype(ref.dtype).reshape(ref.shape)

    sh1 = mod_ref[:, 0, 0:1, :]
    sc1 = mod_ref[:, 0, 1:2, :]
    u = (h_ref[...] * (1.0 + sc1) + sh1).astype(BF16).reshape(pb * TM, D_MODEL)
    z = _dot(u, wsm_ref[...])
    put(qa_ref, z[:, C_QA:C_KA] * NA_SCALE)
    put(ka_ref, z[:, C_KA:C_VA])
    put(va_ref, z[:, C_VA:C_UP])
    put(up_ref, z[:, C_UP:C_UF])
    put(xcs_ref, _dot(z[:, C_UF:C_CQ].astype(BF16), bd_ref[...]))

    cos = jnp.concatenate([cos_ref[...]] * pb, axis=0)
    sin = jnp.concatenate([sin_ref[...]] * pb, axis=0)
    cos4 = jnp.concatenate([cos] * MLA_HEADS, axis=-1)
    sin4 = jnp.concatenate([sin] * MLA_HEADS, axis=-1)

    cq = z[:, C_CQ:C_CKV]
    nq = cq * lax.rsqrt(jnp.mean(cq * cq, axis=-1, keepdims=True) + LN_EPS) * qn_ref[...]
    q2 = _dot(nq.astype(BF16), wuq_ref[...])
    hw = MLA_HEADS * MLA_HEAD_PAD
    put(qm_ref, (q2[:, :hw] * cos4 + q2[:, hw:] * sin4) * MLA_SCALE)

    ckv = z[:, C_CKV:C_KR]
    nkv = ckv * lax.rsqrt(jnp.mean(ckv * ckv, axis=-1, keepdims=True) + LN_EPS) * kvn_ref[...]
    kv2 = _dot(nkv.astype(BF16), wukv_ref[...])
    kr = z[:, C_KR:C_KRS] * cos + z[:, C_KRS:C_END] * sin
    put(km_ref, kv2[:, :hw] + jnp.concatenate([kr] * MLA_HEADS, axis=-1))
    put(vm_ref, kv2[:, hw:])


def _samples_per_step(B):
    return 2 if B % 2 == 0 else 1


def _resident(shape):
    return pl.BlockSpec(shape, lambda *_: (0,) * len(shape), pipeline_mode=pl.Buffered(1))


def _inproj(hall, mod_all, wsm, qn, kvn, wuq2, wukv2, cos_t, sin_t, bd):
    B = hall.shape[0]
    pb = _samples_per_step(B)
    hw = MLA_HEADS * MLA_HEAD_PAD
    row = lambda b, t: (b, t, 0)
    outs = [
        (256, BF16), (256, BF16), (256, BF16), (256, F32), (512, BF16), (hw, BF16), (hw, BF16), (MLA_HEADS * MLA_V, BF16),
    ]
    return pl.pallas_call(
        _inproj_kernel,
        grid=(B // pb, N_TILES),
        in_specs=[
            pl.BlockSpec((pb, TM, D_MODEL), row),
            pl.BlockSpec((pb, 1, 6, D_MODEL), lambda b, t: (b, t // N_LAT_TILES, 0, 0)),
            _resident(wsm.shape),
            _resident(qn.shape),
            _resident(kvn.shape),
            _resident(wuq2.shape),
            _resident(wukv2.shape),
            pl.BlockSpec((TM, LANE), lambda b, t: (t, 0)),
            pl.BlockSpec((TM, LANE), lambda b, t: (t, 0)),
            _resident(bd.shape),
        ],
        out_specs=[pl.BlockSpec((pb, TM, w), row) for w, _ in outs],
        out_shape=[jax.ShapeDtypeStruct((B, T_ALL, w), dt) for w, dt in outs],
        compiler_params=_cparams(("arbitrary", "arbitrary")),
    )(hall, mod_all, wsm, qn, kvn, wuq2, wukv2, cos_t, sin_t, bd)


def _softmax_pv(s, v):
    m = jnp.max(s, axis=-1, keepdims=True)
    p = jnp.exp(s - m)
    l = jnp.sum(p, axis=-1, keepdims=True)
    return _dot(p.astype(BF16), v) / l


def _na_kernel(q_ref, k_ref, v_ref, bias_ref, o_ref):
    t = pl.program_id(0)

    @pl.when(t < N_LAT_TILES)
    def _():
        ks = pl.multiple_of(jnp.clip(t * NA_QROWS - NA_WIN_ROWS // 2, 0, SEQ // GRID_W - NA_KROWS) * GRID_W, GRID_W)
        kk = jnp.concatenate([k_ref[0, pl.ds(ks, NA_KWIN), :], k_ref[0, SEQ:T_ALL, :]], axis=0)
        vv = jnp.concatenate([v_ref[0, pl.ds(ks, NA_KWIN), :], v_ref[0, SEQ:T_ALL, :]], axis=0)
        q = q_ref[0]
        sls = [slice(h * HEAD_DIM, (h + 1) * HEAD_DIM) for h in range(NA_HEADS)]
        ss = [_dot_t(q[:, sl], kk[:, sl]) + bias_ref[0, 0, h] for h, sl in enumerate(sls)]
        ms = [jnp.max(s, axis=-1, keepdims=True) for s in ss]
        ps = [jnp.exp(s - m).astype(BF16) for s, m in zip(ss, ms)]
        ones = _onehot(lax.broadcasted_iota(jnp.int32, (NA_KALL, HEAD_DIM), 1) == 0)
        oa = [_dot(p, jnp.concatenate([vv[:, sl], ones], axis=-1)) for p, sl in zip(ps, sls)]
        os_ = [o[:, :HEAD_DIM] / o[:, HEAD_DIM:HEAD_DIM + 1] for o in oa]
        o_ref[0] = jnp.concatenate(os_, axis=-1).astype(o_ref.dtype)

    @pl.when(t == N_LAT_TILES)
    def _():
        kk = k_ref[0, SEQ:T_ALL, :]
        vv = v_ref[0, SEQ:T_ALL, :]
        q = q_ref[0]
        for h in range(NA_HEADS):
            sl = slice(h * HEAD_DIM, (h + 1) * HEAD_DIM)
            o_ref[0, :, sl] = _softmax_pv(_dot_t(q[:, sl], kk[:, sl]), vv[:, sl]).astype(o_ref.dtype)


def _na_attention(qa, ka, va, bias, layer, n_tiles):
    B = qa.shape[0]

    def bias_idx(t, b):
        return (layer, jnp.where(t == 0, 0, jnp.where(t >= N_LAT_TILES - 1, 2, 1)), 0, 0, 0)

    return pl.pallas_call(
        _na_kernel,
        grid=(n_tiles, B),
        in_specs=[
            pl.BlockSpec((1, TM, 256), lambda t, b: (b, t, 0)),
            pl.BlockSpec((1, T_ALL, 256), lambda t, b: (b, 0, 0)),
            pl.BlockSpec((1, T_ALL, 256), lambda t, b: (b, 0, 0)),
            pl.BlockSpec((1, 1, NA_HEADS, TM, NA_KALL), bias_idx),
        ],
        out_specs=pl.BlockSpec((1, TM, 256), lambda t, b: (b, t, 0)),
        out_shape=jax.ShapeDtypeStruct((B, n_tiles * TM, 256), BF16),
        compiler_params=_cparams(("arbitrary", "arbitrary")),
    )(qa, ka, va, bias)


def _mla_kernel(q_ref, k_ref, v_ref, o_ref):
    t = pl.program_id(1)

    def run(key_lo, n_keys):
        keys = slice(key_lo, key_lo + n_keys)
        ksls = [slice(h * MLA_HEAD_PAD, (h + 1) * MLA_HEAD_PAD) for h in range(MLA_HEADS)]
        vsls = [slice(h * MLA_V, (h + 1) * MLA_V) for h in range(MLA_HEADS)]
        ss = [_dot_t(q_ref[0, :, ksl], k_ref[0, keys, ksl]) for ksl in ksls]
        ms = [jnp.max(s, axis=-1, keepdims=True) for s in ss]
        ps = [jnp.exp(s - m).astype(BF16) for s, m in zip(ss, ms)]
        ones = _onehot(lax.broadcasted_iota(jnp.int32, (n_keys, MLA_V), 1) == 0)
        oa = [_dot(p, jnp.concatenate([v_ref[0, keys, vsl], ones], axis=-1)) for p, vsl in zip(ps, vsls)]
        os_ = [o[:, :MLA_V] / o[:, MLA_V:MLA_V + 1] for o in oa]
        o_ref[0] = jnp.concatenate(os_, axis=-1).astype(o_ref.dtype)

    @pl.when(t < N_LAT_TILES)
    def _():
        run(0, T_ALL)

    @pl.when(t == N_LAT_TILES)
    def _():
        run(SEQ, CTX_LEN)


def _mla_attention(qm, km, vm, n_tiles):
    B = qm.shape[0]
    hw = MLA_HEADS * MLA_HEAD_PAD
    vw = MLA_HEADS * MLA_V
    return pl.pallas_call(
        _mla_kernel,
        grid=(B, n_tiles),
        in_specs=[
            pl.BlockSpec((1, TM, hw), lambda b, t: (b, t, 0)),
            pl.BlockSpec((1, T_ALL, hw), lambda b, t: (b, 0, 0)),
            pl.BlockSpec((1, T_ALL, vw), lambda b, t: (b, 0, 0)),
        ],
        out_specs=pl.BlockSpec((1, TM, vw), lambda b, t: (b, t, 0)),
        out_shape=jax.ShapeDtypeStruct((B, n_tiles * TM, vw), BF16),
        compiler_params=_cparams(("arbitrary", "arbitrary")),
    )(qm, km, vm)


POOL_PAD = POOL_WINDOWS[-1] // 2


def _pool_kernel(u_ref, w_ref, scale_ref, o_ref, pad_ref):
    width = POOL_GROUPS * POOL_CH
    lane = lax.broadcasted_iota(jnp.int32, (1, width), 1)
    half = jnp.left_shift(1, lane // POOL_CH)
    zeros = jnp.zeros((POOL_PAD, width), F32)
    base = 0
    for lo, n in ((0, SEQ), (SEQ, CTX_LEN)):
        x = u_ref[0, lo:lo + n, :]
        pad_ref[base:base + POOL_PAD, :] = zeros
        pad_ref[base + POOL_PAD:base + POOL_PAD + n, :] = x
        pad_ref[base + POOL_PAD + n:base + 2 * POOL_PAD + n, :] = zeros

        def shifted(d):
            return pad_ref[base + POOL_PAD + d:base + POOL_PAD + d + n, :]

        sums = {}
        acc = shifted(-1) + x
        sums[1] = acc
        for h in (2, 4, 8):
            for d in list(range(-h, -h // 2)) + list(range(h // 2, h)):
                acc = acc + shifted(d)
            sums[h] = acc
        win = jnp.where(half == 1, sums[1], jnp.where(half == 2, sums[2], jnp.where(half == 4, sums[4], sums[8])))
        t = lax.broadcasted_iota(jnp.int32, (n, 1), 0)
        cnt = jnp.minimum(t + half, n) - jnp.maximum(t - half, 0)
        y = (win / cnt.astype(F32) - x).astype(BF16)
        o_ref[0, lo:lo + n, :] = (_dot(y, w_ref[...]) * scale_ref[...]).astype(o_ref.dtype)
        base += n + 2 * POOL_PAD


def _pool(up, w_bd, scale):
    B = up.shape[0]
    return pl.pallas_call(
        _pool_kernel,
        grid=(B,),
        in_specs=[
            pl.BlockSpec((1, T_ALL, 256), lambda b: (b, 0, 0)),
            pl.BlockSpec((256, 256), lambda b: (0, 0)),
            pl.BlockSpec((1, 256), lambda b: (0, 0)),
        ],
        out_specs=pl.BlockSpec((1, T_ALL, 256), lambda b: (b, 0, 0)),
        out_shape=jax.ShapeDtypeStruct((B, T_ALL, 256), BF16),
        scratch_shapes=[pltpu.VMEM((T_ALL + 4 * POOL_PAD, POOL_GROUPS * POOL_CH), F32)],
        compiler_params=_cparams(("arbitrary",)),
    )(up, w_bd, scale)


DFT_ROWS = 768


def _dft_kernel(c_ref, s_ref, x_ref, o_ref):
    x = x_ref[0]
    w = FFT_GROUPS * FFT_CH
    o_ref[0] = (_dot(c_ref[...], x[:, :w]) + _dot(s_ref[...], x[:, w:])).astype(o_ref.dtype)


def _dft_positions(cmat, smat, xcs):
    B = xcs.shape[0]
    return pl.pallas_call(
        _dft_kernel,
        grid=(T_ALL // DFT_ROWS, B),
        in_specs=[
            pl.BlockSpec((DFT_ROWS, T_ALL), lambda i, b: (i, 0)),
            pl.BlockSpec((DFT_ROWS, T_ALL), lambda i, b: (i, 0)),
            pl.BlockSpec((1, T_ALL, 512), lambda i, b: (b, 0, 0)),
        ],
        out_specs=pl.BlockSpec((1, DFT_ROWS, 256), lambda i, b: (b, i, 0)),
        out_shape=jax.ShapeDtypeStruct((B, T_ALL, 256), BF16),
        compiler_params=_cparams(("arbitrary", "arbitrary")),
    )(cmat, smat, xcs)


def _merge_kernel(h_ref, mod_ref, ya_ref, yb_ref, yc_ref, yd_ref, wg_ref, wbr_ref, wout_ref, g_ref, b_ref, wr_ref,
                  h1_ref, u2_ref, aff_ref):
    pb = h_ref.shape[0]
    rows = pb * TM
    h = h_ref[...]
    sh1 = mod_ref[:, 0, 0:1, :]
    sc1 = mod_ref[:, 0, 1:2, :]
    g1 = mod_ref[:, 0, 2:3, :]
    sh2 = mod_ref[:, 0, 3:4, :]
    sc2 = mod_ref[:, 0, 4:5, :]
    u = (h * (1.0 + sc1) + sh1).astype(BF16).reshape(rows, D_MODEL)
    ys = (ya_ref, yb_ref, yc_ref, yd_ref)
    acc = jnp.zeros((rows, D_MODEL), F32)
    for n in range(N_BRANCH):
        gate = jax.nn.sigmoid(_dot(u, wg_ref[:, n * D_MODEL:(n + 1) * D_MODEL]))
        acc = acc + gate * _dot(ys[n][...].reshape(rows, BRANCH_W), wbr_ref[n])
    mix = _dot(acc.astype(BF16), wout_ref[...]).reshape(pb, TM, D_MODEL)
    h1 = _layer_norm(DN_ALPHA * h + g1 * mix, g_ref[...], b_ref[...])
    h1_ref[...] = h1
    u2 = h1 * (1.0 + sc2) + sh2
    u2_hi = u2.astype(BF16)
    u2_ref[...] = u2_hi
    u2_lo = (u2 - u2_hi.astype(F32)).astype(BF16).reshape(rows, D_MODEL)
    hi_terms = _dot(u2_hi.reshape(rows, D_MODEL), wr_ref[...])
    logits = hi_terms[:, :ROUTER_PAD] + hi_terms[:, ROUTER_PAD:] + _dot(u2_lo, wr_ref[:, :ROUTER_PAD])
    lane = lax.broadcasted_iota(jnp.int32, (1, ROUTER_PAD), 1)
    logits = jnp.where(lane < N_EXPERTS, logits, NEG)
    e = jnp.exp(logits - jnp.max(logits, axis=-1, keepdims=True))
    aff_ref[...] = (e / jnp.sum(e, axis=-1, keepdims=True)).reshape(pb, TM, ROUTER_PAD)


def _merge(hall, mod_all, ya, yb, yc, yd, wg, wbr, wout, ln_g, ln_b, wr_cat, n_tiles):
    B = hall.shape[0]
    pb = _samples_per_step(B)
    row = lambda b, t: (b, t, 0)
    t_rows = n_tiles * TM
    return pl.pallas_call(
        _merge_kernel,
        grid=(B // pb, n_tiles),
        in_specs=[
            pl.BlockSpec((pb, TM, D_MODEL), row),
            pl.BlockSpec((pb, 1, 6, D_MODEL), lambda b, t: (b, t // N_LAT_TILES, 0, 0)),
            pl.BlockSpec((pb, TM, BRANCH_W), row),
            pl.BlockSpec((pb, TM, BRANCH_W), row),
            pl.BlockSpec((pb, TM, BRANCH_W), row),
            pl.BlockSpec((pb, TM, BRANCH_W), row),
            _resident(wg.shape),
            _resident(wbr.shape),
            _resident(wout.shape),
            _resident((1, D_MODEL)),
            _resident((1, D_MODEL)),
            _resident(wr_cat.shape),
        ],
        out_specs=[
            pl.BlockSpec((pb, TM, D_MODEL), row),
            pl.BlockSpec((pb, TM, D_MODEL), row),
            pl.BlockSpec((pb, TM, ROUTER_PAD), row),
        ],
        out_shape=[
            jax.ShapeDtypeStruct((B, t_rows, D_MODEL), F32),
            jax.ShapeDtypeStruct((B, t_rows, D_MODEL), BF16),
            jax.ShapeDtypeStruct((B, t_rows, ROUTER_PAD), F32),
        ],
        compiler_params=_cparams(("arbitrary", "arbitrary")),
    )(hall, mod_all, ya, yb, yc, yd, wg, wbr, wout, ln_g, ln_b, wr_cat)


def _onehot(match):
    return jnp.where(match, 1.0, 0.0).astype(BF16)


def _route_kernel(a_ref, tri_ref, rank_ref, *, cap):
    a = a_ref[...]
    n = a.shape[1]

    def count(mask):
        return jnp.sum(jnp.where(mask, 1.0, 0.0), axis=-1, keepdims=True)

    p = jnp.full((a.shape[0], 1), 2.0, a.dtype)
    for k in range(ROUTE_EXP_BITS - 1, -1, -1):
        cand = p * (2.0 ** -(2 ** k))
        p = jnp.where(count(a >= cand) < cap, cand, p)
    found = count(a >= 0.5 * p) >= cap
    lo = jnp.where(found, 0.5 * p, 0.0)

    def refine(_, carry):
        lo, step = carry
        cand = lo + step
        return jnp.where(count(a >= cand) >= cap, cand, lo), 0.5 * step

    thr, _ = lax.fori_loop(0, ROUTE_MANTISSA_STEPS, refine, (lo, 0.5 * lo))
    above = a > thr
    tie = a == thr
    tri = tri_ref[:n, :n]
    tie_f = jnp.where(tie, 1.0, 0.0)
    ties_before = _dot(tie_f.astype(BF16), tri) - tie_f
    keep = above | (tie & (ties_before < cap - count(above)))
    keep_f = jnp.where(keep, 1.0, 0.0)
    slot = _dot(keep_f.astype(BF16), tri) - keep_f
    rank_ref[...] = jnp.where(keep, slot, -1.0).astype(jnp.int32)


def _route_set(aff_t, tri, cap):
    return pl.pallas_call(
        functools.partial(_route_kernel, cap=cap),
        out_shape=jax.ShapeDtypeStruct(aff_t.shape, jnp.int32),
        compiler_params=pltpu.CompilerParams(vmem_limit_bytes=VMEM_LIMIT),
    )(aff_t, tri[:aff_t.shape[1], :aff_t.shape[1]])


def _gather_kernel(*refs):
    *set_refs, u_ref, o_ref, g_ref = refs
    e = pl.program_id(1)
    row_lo, tok_lo = 0, 0
    for rank_ref, aff_ref in zip(set_refs[0::2], set_refs[1::2]):
        n = rank_ref.shape[1]
        cap = CAP_LAT if n == SEQ else CAP_CTX
        rank = rank_ref[pl.ds(e, 1), :]
        pick = lax.broadcasted_iota(jnp.int32, (cap, n), 0) == rank
        rows = slice(row_lo, row_lo + cap)
        o_ref[0, rows, :] = _dot(_onehot(pick), u_ref[0, tok_lo:tok_lo + n, :]).astype(o_ref.dtype)
        g_ref[0, rows, :] = jnp.sum(jnp.where(pick, aff_ref[pl.ds(e, 1), :], 0.0), axis=-1, keepdims=True)
        row_lo, tok_lo = row_lo + cap, tok_lo + n


def _gather_tokens(ranks, affs, u2):
    B, t_rows = u2.shape[:2]
    cap = CAP_LAT if len(ranks) == 1 else CAP_ALL
    set_specs, set_args = [], []
    for r, a in zip(ranks, affs):
        spec = pl.BlockSpec((N_EXPERTS, r.shape[1]), lambda b, e: (b, 0))
        set_specs += [spec, spec]
        set_args += [r, a]
    return pl.pallas_call(
        _gather_kernel,
        grid=(B, N_EXPERTS),
        in_specs=set_specs + [pl.BlockSpec((1, t_rows, D_MODEL), lambda b, e: (b, 0, 0))],
        out_specs=[
            pl.BlockSpec((1, cap, D_MODEL), lambda b, e: (e, b, 0)),
            pl.BlockSpec((1, cap, 1), lambda b, e: (e, b, 0)),
        ],
        out_shape=[
            jax.ShapeDtypeStruct((N_EXPERTS, B * cap, D_MODEL), BF16),
            jax.ShapeDtypeStruct((N_EXPERTS, B * cap, 1), F32),
        ],
        compiler_params=_cparams(("arbitrary", "arbitrary")),
    )(*set_args, u2)


def _expert_kernel(x_ref, g_ref, wg_ref, wu_ref, wd_ref, o_ref, acc_ref, *, chunk):
    f = pl.program_id(1)

    @pl.when(f == 0)
    def _():
        acc_ref[...] = jnp.zeros_like(acc_ref)

    wg = wg_ref[0, 0].astype(BF16)
    wu = wu_ref[0, 0].astype(BF16)
    wd = wd_ref[0, 0].astype(BF16)
    for c in range(x_ref.shape[1] // chunk):
        rows = pl.ds(c * chunk, chunk)
        x = x_ref[0, rows, :]
        a = _dot(x, wg)
        u = _dot(x, wu)
        acc_ref[rows, :] += _dot((a * jax.nn.sigmoid(a) * u).astype(BF16), wd)

    @pl.when(f == pl.num_programs(1) - 1)
    def _():
        o_ref[0] = (acc_ref[...] * g_ref[0]).astype(o_ref.dtype)


def _experts(xe, gcol, w_gate, w_up, w_down, layer):
    rows = xe.shape[1]
    chunk = next(c for c in (XROW_CHUNK, 512, CAP_ALL, CAP_LAT) if rows % c == 0)
    return pl.pallas_call(
        functools.partial(_expert_kernel, chunk=chunk),
        grid=(N_EXPERTS, EXPERT_FF // FF_TILE),
        in_specs=[
            pl.BlockSpec((1, rows, D_MODEL), lambda e, f: (e, 0, 0)),
            pl.BlockSpec((1, rows, 1), lambda e, f: (e, 0, 0)),
            pl.BlockSpec((1, 1, D_MODEL, FF_TILE), lambda e, f: (layer, e, 0, f)),
            pl.BlockSpec((1, 1, D_MODEL, FF_TILE), lambda e, f: (layer, e, 0, f)),
            pl.BlockSpec((1, 1, FF_TILE, D_MODEL), lambda e, f: (layer, e, f, 0)),
        ],
        out_specs=pl.BlockSpec((1, rows, D_MODEL), lambda e, f: (e, 0, 0)),
        out_shape=jax.ShapeDtypeStruct((N_EXPERTS, rows, D_MODEL), BF16),
        scratch_shapes=[pltpu.VMEM((rows, D_MODEL), F32)],
        compiler_params=_cparams(("arbitrary", "arbitrary")),
    )(xe, gcol, w_gate, w_up, w_down)


def _scatter_kernel(h_ref, mod_ref, y_ref, g_ref, b_ref, rank_ref, o_ref):
    t = pl.program_id(1)
    rank = rank_ref[0]

    def finish(onehot, slot_lo, slot_hi):
        y = y_ref[:, 0, slot_lo:slot_hi, :].reshape(N_EXPERTS * (slot_hi - slot_lo), D_MODEL)
        fx = _dot(onehot, y)
        g2 = mod_ref[0, 0, 5:6, :]
        o_ref[0] = _layer_norm(DN_ALPHA * h_ref[0] + g2 * fx, g_ref[...], b_ref[...])

    def latent():
        slot = lax.broadcasted_iota(jnp.int32, (TM, CAP_LAT), 1)
        finish(jnp.concatenate([_onehot(rank[:, e:e + 1] == slot) for e in range(N_EXPERTS)], axis=-1), 0, CAP_LAT)

    def context():
        n = N_EXPERTS * CAP_CTX
        col = lax.broadcasted_iota(jnp.int32, (N_EXPERTS, n), 1)
        spread = _onehot(col // CAP_CTX == lax.broadcasted_iota(jnp.int32, (N_EXPERTS, n), 0))
        rank_cols = _dot(rank.astype(F32).astype(BF16), spread)
        slot = (lax.broadcasted_iota(jnp.int32, (TM, n), 1) % CAP_CTX).astype(F32)
        finish(_onehot(rank_cols == slot), CAP_LAT, CAP_ALL)

    if y_ref.shape[2] == CAP_LAT:
        latent()
    else:
        pl.when(t < N_LAT_TILES)(latent)
        pl.when(t == N_LAT_TILES)(context)


def _scatter_ln(h1, mod_all, ye, ln_g, ln_b, rank_tok):
    B = h1.shape[0]
    cap = ye.shape[2]
    n_tiles = rank_tok.shape[1] // TM
    return pl.pallas_call(
        _scatter_kernel,
        grid=(B, n_tiles),
        in_specs=[
            pl.BlockSpec((1, TM, D_MODEL), lambda b, t: (b, t, 0)),
            pl.BlockSpec((1, 1, 6, D_MODEL), lambda b, t: (b, t // N_LAT_TILES, 0, 0)),
            pl.BlockSpec((N_EXPERTS, 1, cap, D_MODEL), lambda b, t: (0, b, 0, 0)),
            pl.BlockSpec((1, D_MODEL), lambda b, t: (0, 0)),
            pl.BlockSpec((1, D_MODEL), lambda b, t: (0, 0)),
            pl.BlockSpec((1, TM, N_EXPERTS), lambda b, t: (b, t, 0)),
        ],
        out_specs=pl.BlockSpec((1, TM, D_MODEL), lambda b, t: (b, t, 0)),
        out_shape=jax.ShapeDtypeStruct((B, n_tiles * TM, D_MODEL), F32),
        compiler_params=_cparams(("arbitrary", "arbitrary")),
    )(h1, mod_all, ye, ln_g, ln_b, rank_tok)


def _rope_tables():
    n_freq = MLA_ROPE // 4
    inv = ROPE_BASE ** (-jnp.arange(n_freq, dtype=F32) / n_freq)
    t = jnp.arange(SEQ)
    row = (t // GRID_W).astype(F32)
    col = (t % GRID_W).astype(F32)
    ang = jnp.concatenate([row[:, None] * inv, col[:, None] * inv], axis=-1)
    cos, sin = jnp.cos(ang), jnp.sin(ang)
    ones = jnp.ones((SEQ, MLA_NOPE), F32)
    pad1 = jnp.ones((SEQ, MLA_HEAD_PAD - MLA_NOPE - MLA_ROPE), F32)
    cos_t = jnp.concatenate([ones, cos, cos, pad1], axis=-1)
    sin_t = jnp.concatenate([0 * ones, -sin, sin, 0 * pad1], axis=-1)
    cos_t = jnp.concatenate([cos_t, jnp.ones((CTX_LEN, MLA_HEAD_PAD), F32)], axis=0)
    sin_t = jnp.concatenate([sin_t, jnp.zeros((CTX_LEN, MLA_HEAD_PAD), F32)], axis=0)
    return cos_t, sin_t


def _dft_tables():
    def block(n):
        split = 64
        k = jnp.arange(n, dtype=jnp.int32)[None, :]
        jh = jnp.arange(n // split, dtype=jnp.int32)[:, None]
        jl = jnp.arange(split, dtype=jnp.int32)[:, None]
        ang_h = ((jh * split * k) % n).astype(F32) * (2.0 * np.pi / n)
        ang_l = ((jl * k) % n).astype(F32) * (2.0 * np.pi / n)
        ch, sh = jnp.cos(ang_h)[:, None, :], jnp.sin(ang_h)[:, None, :]
        cl_, sl_ = jnp.cos(ang_l)[None, :, :], jnp.sin(ang_l)[None, :, :]
        sc = 1.0 / np.sqrt(n * FFT_CH)
        return ((ch * cl_ - sh * sl_) * sc).reshape(n, n), (-(sh * cl_ + ch * sl_) * sc).reshape(n, n)

    cl, sl = block(SEQ)
    cc, sc_ = block(CTX_LEN)

    def diag(a, b):
        top = jnp.concatenate([a, jnp.zeros((SEQ, CTX_LEN), F32)], axis=1)
        bot = jnp.concatenate([jnp.zeros((CTX_LEN, SEQ), F32), b], axis=1)
        return jnp.concatenate([top, bot], axis=0).astype(BF16)

    j = jnp.arange(FFT_CH, dtype=jnp.int32)
    ang = ((j[:, None] * j[None, :]) % FFT_CH).astype(F32) * (2.0 * np.pi / FFT_CH)
    eye = jnp.eye(FFT_GROUPS, dtype=F32)
    bd = jnp.concatenate([jnp.kron(eye, jnp.cos(ang)), jnp.kron(eye, jnp.sin(ang))], axis=1).astype(BF16)
    return diag(cl, cc), diag(sl, sc_), bd


def _na_bias_index():
    R = SEQ // GRID_W
    pats = [(0, 0), (NA_QROWS, 0), (R - NA_QROWS, R - NA_KROWS)]
    q = np.arange(TM)
    k = np.arange(NA_KWIN)
    qi, qc = q // GRID_W, q % GRID_W
    kj, kc = k // GRID_W, k % GRID_W
    c0 = np.clip(qc - NA_WIN_COLS // 2, 0, GRID_W - NA_WIN_COLS)
    col_in = (kc[None, :] >= c0[:, None]) & (kc[None, :] < c0[:, None] + NA_WIN_COLS)
    col_off = np.clip(kc[None, :] - qc[:, None], -(NA_WIN_COLS - 1), NA_WIN_COLS - 1) + (NA_WIN_COLS - 1)
    del col_off
    row_offs, valids = [], []
    for rb, ks in pats:
        qr = rb + np.arange(NA_QROWS)
        kr = ks + np.arange(NA_KROWS)
        r0 = np.clip(qr - NA_WIN_ROWS // 2, 0, R - NA_WIN_ROWS)
        row_in = (kr[None, :] >= r0[:, None]) & (kr[None, :] < r0[:, None] + NA_WIN_ROWS)
        row_offs.append(np.where(row_in, kr[None, :] - qr[:, None] + (NA_WIN_ROWS - 1), -1))
        valids.append(row_in[qi][:, kj] & col_in)
    return np.stack(row_offs), np.stack(valids)


def _na_bias(rpb):
    row_off, valid = _na_bias_index()
    reach = GRID_W - NA_WIN_COLS
    padded = jnp.pad(rpb, ((0, 0), (0, 0), (0, 0), (reach, reach)))
    band = jnp.stack([padded[..., GRID_W - 1 - qc:2 * GRID_W - 1 - qc] for qc in range(GRID_W)], axis=-2)
    masked = jnp.zeros(band.shape[:2] + band.shape[3:], F32)
    pats = []
    for p in range(row_off.shape[0]):
        qrows = []
        for qi in range(NA_QROWS):
            blocks = [masked if row_off[p, qi, kj] < 0 else band[:, :, row_off[p, qi, kj]] for kj in range(NA_KROWS)]
            qrows.append(jnp.concatenate(blocks, axis=-1))
        pats.append(jnp.concatenate(qrows, axis=-2))
    b = jnp.stack(pats, axis=1)
    b = jnp.where(jnp.asarray(valid)[None, :, None], b, NEG)
    return jnp.concatenate([b, jnp.zeros(b.shape[:-1] + (CTX_LEN,), F32)], axis=-1)


def _rearranged_in_weights(w_in):
    z = lambda n: jnp.zeros((D_MODEL, n), F32)
    kr = w_in[:, 1664:1696]
    kr_e, kr_o = kr[:, 0::2], kr[:, 1::2]
    pad = MLA_HEAD_PAD - MLA_NOPE - MLA_ROPE
    krb = jnp.concatenate([z(MLA_NOPE), kr_e, kr_o, z(pad)], axis=1)
    krs = jnp.concatenate([z(MLA_NOPE), kr_o, kr_e, z(pad)], axis=1)
    wsm = jnp.concatenate([w_in[:, :1664], krb, krs], axis=1).astype(BF16)
    return wsm, w_in[:, 1696:].astype(BF16)


def _rearranged_mla_weights(w_uq, w_ukv):
    pad = MLA_HEAD_PAD - MLA_NOPE - MLA_ROPE
    wq = w_uq.reshape(MLA_Q_RANK, MLA_HEADS, MLA_NOPE + MLA_ROPE)
    nope, rope = wq[..., :MLA_NOPE], wq[..., MLA_NOPE:]
    r_e, r_o = rope[..., 0::2], rope[..., 1::2]
    zq = jnp.zeros((MLA_Q_RANK, MLA_HEADS, pad), F32)
    q_main = jnp.concatenate([nope, r_e, r_o, zq], axis=-1).reshape(MLA_Q_RANK, -1)
    q_swap = jnp.concatenate([nope, r_o, r_e, zq], axis=-1).reshape(MLA_Q_RANK, -1)
    wuq2 = jnp.concatenate([q_main, q_swap], axis=1).astype(BF16)
    wkv = w_ukv.reshape(MLA_KV_RANK, MLA_HEADS, MLA_NOPE + MLA_V)
    k_nope, v = wkv[..., :MLA_NOPE], wkv[..., MLA_NOPE:]
    zk = jnp.zeros((MLA_KV_RANK, MLA_HEADS, MLA_HEAD_PAD - MLA_NOPE), F32)
    k_main = jnp.concatenate([k_nope, zk], axis=-1).reshape(MLA_KV_RANK, -1)
    wukv2 = jnp.concatenate([k_main, v.reshape(MLA_KV_RANK, -1)], axis=1).astype(BF16)
    return wuq2, wukv2


def _route(aff, tri, with_ctx):
    B = aff.shape[0]
    a = aff.transpose(0, 2, 1).reshape(B * N_EXPERTS, -1)
    affs = [a[:, :SEQ]] + ([a[:, SEQ:]] if with_ctx else [])
    ranks = [_route_set(s, tri, cap) for s, cap in zip(affs, (CAP_LAT, CAP_CTX))]
    rank_tok = jnp.concatenate([r.reshape(B, N_EXPERTS, -1) for r in ranks], axis=-1).transpose(0, 2, 1)
    return affs, ranks, rank_tok


def kernel(x, c, ctx, c_ctx, w_mod, b_mod, w_in, na_rpb, pool_w, pool_scale, mla_q_norm, mla_w_uq, mla_kv_norm,
           mla_w_ukv, w_branch, w_out, ln1_g, ln1_b, w_router, w_gate, w_up, w_down, ln2_g, ln2_b):
    B = x.shape[0]
    assert x.shape == (B, SEQ, D_MODEL) and ctx.shape == (B, CTX_LEN, D_MODEL) and B + 1 <= 16

    cc = jnp.concatenate([c, c_ctx[None], jnp.zeros((16 - B - 1, D_MODEL), F32)], axis=0)
    mod = _modulation(cc, w_mod, b_mod).reshape(DEPTH, 16, 6, D_MODEL)
    cos_t, sin_t = _rope_tables()
    cmat, smat, bd = _dft_tables()
    na_bias = _na_bias(na_rpb)
    tri = jnp.triu(jnp.ones((SEQ, SEQ), BF16))
    hall = jnp.concatenate([x, ctx], axis=1)

    for l in range(DEPTH):
        with_ctx = l < DEPTH - 1
        n_tiles = N_TILES if with_ctx else N_LAT_TILES
        mod_all = jnp.stack([mod[l, :B], jnp.broadcast_to(mod[l, B], (B, 6, D_MODEL))], axis=1)
        wsm, wgates = _rearranged_in_weights(w_in[l])
        wuq2, wukv2 = _rearranged_mla_weights(mla_w_uq[l], mla_w_ukv[l])
        qa, ka, va, up, xcs, qm, km, vm = _inproj(
            hall, mod_all, wsm, mla_q_norm[l][None], mla_kv_norm[l][None], wuq2, wukv2, cos_t, sin_t, bd)
        ya = _na_attention(qa, ka, va, na_bias, l, n_tiles)
        pool_bd = jax.scipy.linalg.block_diag(*[pool_w[l, g] for g in range(POOL_GROUPS)]).astype(BF16)
        yb = _pool(up, pool_bd, pool_scale[l][None])
        yc = _dft_positions(cmat, smat, xcs)
        yd = _mla_attention(qm, km, vm, n_tiles)
        wr = jnp.pad(w_router[l], ((0, 0), (0, ROUTER_PAD - N_EXPERTS)))
        wr_hi = wr.astype(BF16)
        wr_cat = jnp.concatenate([wr_hi, (wr - wr_hi.astype(F32)).astype(BF16)], axis=1)
        h1, u2, aff = _merge(hall, mod_all, ya, yb, yc, yd, wgates, w_branch[l].astype(BF16), w_out[l].astype(BF16),
                             ln1_g[l][None], ln1_b[l][None], wr_cat, n_tiles)
        affs, ranks, rank_tok = _route(aff[:, :, :N_EXPERTS], tri, with_ctx)
        xe, gcol = _gather_tokens(ranks, affs, u2)
        ye = _experts(xe, gcol, w_gate, w_up, w_down, l)
        hall = _scatter_ln(h1, mod_all, ye.reshape(N_EXPERTS, B, -1, D_MODEL), ln2_g[l][None], ln2_b[l][None], rank_tok)
    return hall
```

```python
import functools

import numpy as np
import jax
import jax.numpy as jnp
from jax import lax
from jax.experimental import pallas as pl
from jax.experimental.pallas import tpu as pltpu

F32 = jnp.float32
BF16 = jnp.bfloat16

D_MODEL = 1024
DEPTH = 4
GRID_W = 64
SEQ = 2048
CTX_LEN = 256
T_ALL = SEQ + CTX_LEN
HEAD_DIM = 64
NA_HEADS = 4
NA_WIN_ROWS = 8
NA_WIN_COLS = 16
NA_SCALE = HEAD_DIM ** -0.5
POOL_GROUPS = 4
POOL_CH = 64
POOL_WINDOWS = (2, 4, 8, 16)
FFT_GROUPS = 4
FFT_CH = 64
MLA_HEADS = 4
MLA_Q_RANK = 256
MLA_KV_RANK = 128
MLA_NOPE = 64
MLA_ROPE = 32
MLA_V = 64
MLA_SCALE = (MLA_NOPE + MLA_ROPE) ** -0.5
ROPE_BASE = 10000.0
BRANCH_W = 256
N_BRANCH = 4
N_EXPERTS = 16
EXPERT_FF = 2048
EC_CAPACITY = 2
DN_ALPHA = (2 * DEPTH) ** 0.25
LN_EPS = 1e-5
NEG = -1e30

LANE = 128
TM = 256
N_TILES = T_ALL // TM
N_LAT_TILES = SEQ // TM
MLA_HEAD_PAD = 128
NA_QROWS = TM // GRID_W
NA_KROWS = 12
NA_KWIN = NA_KROWS * GRID_W
NA_KALL = NA_KWIN + CTX_LEN
CAP_LAT = EC_CAPACITY * SEQ // N_EXPERTS
CAP_CTX = EC_CAPACITY * CTX_LEN // N_EXPERTS
CAP_ALL = CAP_LAT + CAP_CTX
ROUTER_PAD = 128
ROUTE_EXP_BITS = 11
ROUTE_MANTISSA_STEPS = 52
FF_TILE = 512
XROW_CHUNK = 576
VMEM_LIMIT = 56 * 1024 * 1024

C_QA, C_KA, C_VA, C_UP, C_UF, C_CQ, C_CKV, C_KR, C_KRS, C_END = 0, 256, 512, 768, 1024, 1280, 1536, 1664, 1792, 1920


def _cparams(sem):
    return pltpu.CompilerParams(dimension_semantics=sem, vmem_limit_bytes=VMEM_LIMIT)


def _dot(a, b):
    return jnp.dot(a, b, preferred_element_type=F32)


def _dot_t(a, b):
    return lax.dot_general(a, b, (((1,), (1,)), ((), ())), preferred_element_type=F32)


def _layer_norm(x, g, b):
    mu = jnp.mean(x, axis=-1, keepdims=True)
    xc = x - mu
    var = jnp.mean(xc * xc, axis=-1, keepdims=True)
    return xc * lax.rsqrt(var + LN_EPS) * g + b


def _mod_kernel(c_ref, w_ref, b_ref, o_ref):
    c = c_ref[...]
    s = c * jax.nn.sigmoid(c)
    w = w_ref[0]
    s_hi = s.astype(BF16)
    s_lo = (s - s_hi.astype(F32)).astype(BF16)
    w_hi = w.astype(BF16)
    w_lo = (w - w_hi.astype(F32)).astype(BF16)
    o_ref[0] = _dot(s_hi, w_hi) + _dot(s_hi, w_lo) + _dot(s_lo, w_hi) + b_ref[0]


def _modulation(cc, w_mod, b_mod):
    rows = cc.shape[0]
    n6 = 6 * D_MODEL
    tn = 1536
    return pl.pallas_call(
        _mod_kernel,
        grid=(DEPTH, n6 // tn),
        in_specs=[
            pl.BlockSpec((rows, D_MODEL), lambda l, j: (0, 0)),
            pl.BlockSpec((1, D_MODEL, tn), lambda l, j: (l, 0, j)),
            pl.BlockSpec((1, 1, tn), lambda l, j: (l, 0, j)),
        ],
        out_specs=pl.BlockSpec((1, rows, tn), lambda l, j: (l, 0, j)),
        out_shape=jax.ShapeDtypeStruct((DEPTH, rows, n6), F32),
        compiler_params=_cparams(("arbitrary", "arbitrary")),
    )(cc, w_mod, b_mod.reshape(DEPTH, 1, n6))


def _inproj_kernel(h_ref, mod_ref, wsm_ref, qn_ref, kvn_ref, wuq_ref, wukv_ref, cos_ref, sin_ref, bd_ref,
                   qa_ref, ka_ref, va_ref, up_ref, xcs_ref, qm_ref, km_ref, vm_ref):
    pb = h_ref.shape[0]

    def put(ref, val):
        ref[...] = val.astype(ref.dtype).reshape(ref.shape)

    sh1 = mod_ref[:, 0, 0:1, :]
    sc1 = mod_ref[:, 0, 1:2, :]
    u = (h_ref[...] * (1.0 + sc1) + sh1).astype(BF16).reshape(pb * TM, D_MODEL)
    z = _dot(u, wsm_ref[...])
    put(qa_ref, z[:, C_QA:C_KA] * NA_SCALE)
    put(ka_ref, z[:, C_KA:C_VA])
    put(va_ref, z[:, C_VA:C_UP])
    put(up_ref, z[:, C_UP:C_UF])
    put(xcs_ref, _dot(z[:, C_UF:C_CQ].astype(BF16), bd_ref[...]))

    cos = jnp.concatenate([cos_ref[...]] * pb, axis=0)
    sin = jnp.concatenate([sin_ref[...]] * pb, axis=0)
    cos4 = jnp.concatenate([cos] * MLA_HEADS, axis=-1)
    sin4 = jnp.concatenate([sin] * MLA_HEADS, axis=-1)

    cq = z[:, C_CQ:C_CKV]
    nq = cq * lax.rsqrt(jnp.mean(cq * cq, axis=-1, keepdims=True) + LN_EPS) * qn_ref[...]
    q2 = _dot(nq.astype(BF16), wuq_ref[...])
    hw = MLA_HEADS * MLA_HEAD_PAD
    put(qm_ref, (q2[:, :hw] * cos4 + q2[:, hw:] * sin4) * MLA_SCALE)

    ckv = z[:, C_CKV:C_KR]
    nkv = ckv * lax.rsqrt(jnp.mean(ckv * ckv, axis=-1, keepdims=True) + LN_EPS) * kvn_ref[...]
    kv2 = _dot(nkv.astype(BF16), wukv_ref[...])
    kr = z[:, C_KR:C_KRS] * cos + z[:, C_KRS:C_END] * sin
    put(km_ref, kv2[:, :hw] + jnp.concatenate([kr] * MLA_HEADS, axis=-1))
    put(vm_ref, kv2[:, hw:])


def _samples_per_step(B):
    return 2 if B % 2 == 0 else 1


def _resident(shape):
    return pl.BlockSpec(shape, lambda *_: (0,) * len(shape), pipeline_mode=pl.Buffered(1))


def _inproj(hall, mod_all, wsm, qn, kvn, wuq2, wukv2, cos_t, sin_t, bd):
    B = hall.shape[0]
    pb = _samples_per_step(B)
    hw = MLA_HEADS * MLA_HEAD_PAD
    row = lambda b, t: (b, t, 0)
    outs = [
        (256, BF16), (256, BF16), (256, BF16), (256, F32), (512, BF16), (hw, BF16), (hw, BF16), (MLA_HEADS * MLA_V, BF16),
    ]
    return pl.pallas_call(
        _inproj_kernel,
        grid=(B // pb, N_TILES),
        in_specs=[
            pl.BlockSpec((pb, TM, D_MODEL), row),
            pl.BlockSpec((pb, 1, 6, D_MODEL), lambda b, t: (b, t // N_LAT_TILES, 0, 0)),
            _resident(wsm.shape),
            _resident(qn.shape),
            _resident(kvn.shape),
            _resident(wuq2.shape),
            _resident(wukv2.shape),
            pl.BlockSpec((TM, LANE), lambda b, t: (t, 0)),
            pl.BlockSpec((TM, LANE), lambda b, t: (t, 0)),
            _resident(bd.shape),
        ],
        out_specs=[pl.BlockSpec((pb, TM, w), row) for w, _ in outs],
        out_shape=[jax.ShapeDtypeStruct((B, T_ALL, w), dt) for w, dt in outs],
        compiler_params=_cparams(("arbitrary", "arbitrary")),
    )(hall, mod_all, wsm, qn, kvn, wuq2, wukv2, cos_t, sin_t, bd)


def _softmax_pv(s, v):
    m = jnp.max(s, axis=-1, keepdims=True)
    p = jnp.exp(s - m)
    l = jnp.sum(p, axis=-1, keepdims=True)
    return _dot(p.astype(BF16), v) / l


def _na_kernel(q_ref, k_ref, v_ref, bias_ref, o_ref):
    t = pl.program_id(0)

    @pl.when(t < N_LAT_TILES)
    def _():
        ks = pl.multiple_of(jnp.clip(t * NA_QROWS - NA_WIN_ROWS // 2, 0, SEQ // GRID_W - NA_KROWS) * GRID_W, GRID_W)
        kk = jnp.concatenate([k_ref[0, pl.ds(ks, NA_KWIN), :], k_ref[0, SEQ:T_ALL, :]], axis=0)
        vv = jnp.concatenate([v_ref[0, pl.ds(ks, NA_KWIN), :], v_ref[0, SEQ:T_ALL, :]], axis=0)
        q = q_ref[0]
        sls = [slice(h * HEAD_DIM, (h + 1) * HEAD_DIM) for h in range(NA_HEADS)]
        ss = [_dot_t(q[:, sl], kk[:, sl]) + bias_ref[0, 0, h] for h, sl in enumerate(sls)]
        ms = [jnp.max(s, axis=-1, keepdims=True) for s in ss]
        ps = [jnp.exp(s - m).astype(BF16) for s, m in zip(ss, ms)]
        ones = _onehot(lax.broadcasted_iota(jnp.int32, (NA_KALL, HEAD_DIM), 1) == 0)
        oa = [_dot(p, jnp.concatenate([vv[:, sl], ones], axis=-1)) for p, sl in zip(ps, sls)]
        os_ = [o[:, :HEAD_DIM] / o[:, HEAD_DIM:HEAD_DIM + 1] for o in oa]
        o_ref[0] = jnp.concatenate(os_, axis=-1).astype(o_ref.dtype)

    @pl.when(t == N_LAT_TILES)
    def _():
        kk = k_ref[0, SEQ:T_ALL, :]
        vv = v_ref[0, SEQ:T_ALL, :]
        q = q_ref[0]
        for h in range(NA_HEADS):
            sl = slice(h * HEAD_DIM, (h + 1) * HEAD_DIM)
            o_ref[0, :, sl] = _softmax_pv(_dot_t(q[:, sl], kk[:, sl]), vv[:, sl]).astype(o_ref.dtype)


def _na_attention(qa, ka, va, bias, layer, n_tiles):
    B = qa.shape[0]

    def bias_idx(t, b):
        return (layer, jnp.where(t == 0, 0, jnp.where(t >= N_LAT_TILES - 1, 2, 1)), 0, 0, 0)

    return pl.pallas_call(
        _na_kernel,
        grid=(n_tiles, B),
        in_specs=[
            pl.BlockSpec((1, TM, 256), lambda t, b: (b, t, 0)),
            pl.BlockSpec((1, T_ALL, 256), lambda t, b: (b, 0, 0)),
            pl.BlockSpec((1, T_ALL, 256), lambda t, b: (b, 0, 0)),
            pl.BlockSpec((1, 1, NA_HEADS, TM, NA_KALL), bias_idx),
        ],
        out_specs=pl.BlockSpec((1, TM, 256), lambda t, b: (b, t, 0)),
        out_shape=jax.ShapeDtypeStruct((B, n_tiles * TM, 256), BF16),
        compiler_params=_cparams(("arbitrary", "arbitrary")),
    )(qa, ka, va, bias)


def _mla_kernel(q_ref, k_ref, v_ref, o_ref):
    t = pl.program_id(1)

    def run(key_lo, n_keys):
        keys = slice(key_lo, key_lo + n_keys)
        ksls = [slice(h * MLA_HEAD_PAD, (h + 1) * MLA_HEAD_PAD) for h in range(MLA_HEADS)]
        vsls = [slice(h * MLA_V, (h + 1) * MLA_V) for h in range(MLA_HEADS)]
        ss = [_dot_t(q_ref[0, :, ksl], k_ref[0, keys, ksl]) for ksl in ksls]
        ms = [jnp.max(s, axis=-1, keepdims=True) for s in ss]
        ps = [jnp.exp(s - m).astype(BF16) for s, m in zip(ss, ms)]
        ones = _onehot(lax.broadcasted_iota(jnp.int32, (n_keys, MLA_V), 1) == 0)
        oa = [_dot(p, jnp.concatenate([v_ref[0, keys, vsl], ones], axis=-1)) for p, vsl in zip(ps, vsls)]
        os_ = [o[:, :MLA_V] / o[:, MLA_V:MLA_V + 1] for o in oa]
        o_ref[0] = jnp.concatenate(os_, axis=-1).astype(o_ref.dtype)

    @pl.when(t < N_LAT_TILES)
    def _():
        run(0, T_ALL)

    @pl.when(t == N_LAT_TILES)
    def _():
        run(SEQ, CTX_LEN)


def _mla_attention(qm, km, vm, n_tiles):
    B = qm.shape[0]
    hw = MLA_HEADS * MLA_HEAD_PAD
    vw = MLA_HEADS * MLA_V
    return pl.pallas_call(
        _mla_kernel,
        grid=(B, n_tiles),
        in_specs=[
            pl.BlockSpec((1, TM, hw), lambda b, t: (b, t, 0)),
            pl.BlockSpec((1, T_ALL, hw), lambda b, t: (b, 0, 0)),
            pl.BlockSpec((1, T_ALL, vw), lambda b, t: (b, 0, 0)),
        ],
        out_specs=pl.BlockSpec((1, TM, vw), lambda b, t: (b, t, 0)),
        out_shape=jax.ShapeDtypeStruct((B, n_tiles * TM, vw), BF16),
        compiler_params=_cparams(("arbitrary", "arbitrary")),
    )(qm, km, vm)


POOL_PAD = POOL_WINDOWS[-1] // 2


def _pool_kernel(u_ref, w_ref, scale_ref, o_ref, pad_ref):
    width = POOL_GROUPS * POOL_CH
    lane = lax.broadcasted_iota(jnp.int32, (1, width), 1)
    half = jnp.left_shift(1, lane // POOL_CH)
    zeros = jnp.zeros((POOL_PAD, width), F32)
    base = 0
    for lo, n in ((0, SEQ), (SEQ, CTX_LEN)):
        x = u_ref[0, lo:lo + n, :]
        pad_ref[base:base + POOL_PAD, :] = zeros
        pad_ref[base + POOL_PAD:base + POOL_PAD + n, :] = x
        pad_ref[base + POOL_PAD + n:base + 2 * POOL_PAD + n, :] = zeros

        def shifted(d):
            return pad_ref[base + POOL_PAD + d:base + POOL_PAD + d + n, :]

        sums = {}
        acc = shifted(-1) + x
        sums[1] = acc
        for h in (2, 4, 8):
            for d in list(range(-h, -h // 2)) + list(range(h // 2, h)):
                acc = acc + shifted(d)
            sums[h] = acc
        win = jnp.where(half == 1, sums[1], jnp.where(half == 2, sums[2], jnp.where(half == 4, sums[4], sums[8])))
        t = lax.broadcasted_iota(jnp.int32, (n, 1), 0)
        cnt = jnp.minimum(t + half, n) - jnp.maximum(t - half, 0)
        y = (win / cnt.astype(F32) - x).astype(BF16)
        o_ref[0, lo:lo + n, :] = (_dot(y, w_ref[...]) * scale_ref[...]).astype(o_ref.dtype)
        base += n + 2 * POOL_PAD


def _pool(up, w_bd, scale):
    B = up.shape[0]
    return pl.pallas_call(
        _pool_kernel,
        grid=(B,),
        in_specs=[
            pl.BlockSpec((1, T_ALL, 256), lambda b: (b, 0, 0)),
            pl.BlockSpec((256, 256), lambda b: (0, 0)),
            pl.BlockSpec((1, 256), lambda b: (0, 0)),
        ],
        out_specs=pl.BlockSpec((1, T_ALL, 256), lambda b: (b, 0, 0)),
        out_shape=jax.ShapeDtypeStruct((B, T_ALL, 256), BF16),
        scratch_shapes=[pltpu.VMEM((T_ALL + 4 * POOL_PAD, POOL_GROUPS * POOL_CH), F32)],
        compiler_params=_cparams(("arbitrary",)),
    )(up, w_bd, scale)


DFT_ROWS = 768


def _dft_kernel(c_ref, s_ref, x_ref, o_ref):
    x = x_ref[0]
    w = FFT_GROUPS * FFT_CH
    o_ref[0] = (_dot(c_ref[...], x[:, :w]) + _dot(s_ref[...], x[:, w:])).astype(o_ref.dtype)


def _dft_positions(cmat, smat, xcs):
    B = xcs.shape[0]
    return pl.pallas_call(
        _dft_kernel,
        grid=(T_ALL // DFT_ROWS, B),
        in_specs=[
            pl.BlockSpec((DFT_ROWS, T_ALL), lambda i, b: (i, 0)),
            pl.BlockSpec((DFT_ROWS, T_ALL), lambda i, b: (i, 0)),
            pl.BlockSpec((1, T_ALL, 512), lambda i, b: (b, 0, 0)),
        ],
        out_specs=pl.BlockSpec((1, DFT_ROWS, 256), lambda i, b: (b, i, 0)),
        out_shape=jax.ShapeDtypeStruct((B, T_ALL, 256), BF16),
        compiler_params=_cparams(("arbitrary", "arbitrary")),
    )(cmat, smat, xcs)


def _merge_kernel(h_ref, mod_ref, ya_ref, yb_ref, yc_ref, yd_ref, wg_ref, wbr_ref, wout_ref, g_ref, b_ref, wr_ref,
                  h1_ref, u2_ref, aff_ref):
    pb = h_ref.shape[0]
    rows = pb * TM
    h = h_ref[...]
    sh1 = mod_ref[:, 0, 0:1, :]
    sc1 = mod_ref[:, 0, 1:2, :]
    g1 = mod_ref[:, 0, 2:3, :]
    sh2 = mod_ref[:, 0, 3:4, :]
    sc2 = mod_ref[:, 0, 4:5, :]
    u = (h * (1.0 + sc1) + sh1).astype(BF16).reshape(rows, D_MODEL)
    ys = (ya_ref, yb_ref, yc_ref, yd_ref)
    acc = jnp.zeros((rows, D_MODEL), F32)
    for n in range(N_BRANCH):
        gate = jax.nn.sigmoid(_dot(u, wg_ref[:, n * D_MODEL:(n + 1) * D_MODEL]))
        acc = acc + gate * _dot(ys[n][...].reshape(rows, BRANCH_W), wbr_ref[n])
    mix = _dot(acc.astype(BF16), wout_ref[...]).reshape(pb, TM, D_MODEL)
    h1 = _layer_norm(DN_ALPHA * h + g1 * mix, g_ref[...], b_ref[...])
    h1_ref[...] = h1
    u2 = h1 * (1.0 + sc2) + sh2
    u2_hi = u2.astype(BF16)
    u2_ref[...] = u2_hi
    u2_lo = (u2 - u2_hi.astype(F32)).astype(BF16).reshape(rows, D_MODEL)
    hi_terms = _dot(u2_hi.reshape(rows, D_MODEL), wr_ref[...])
    logits = hi_terms[:, :ROUTER_PAD] + hi_terms[:, ROUTER_PAD:] + _dot(u2_lo, wr_ref[:, :ROUTER_PAD])
    lane = lax.broadcasted_iota(jnp.int32, (1, ROUTER_PAD), 1)
    logits = jnp.where(lane < N_EXPERTS, logits, NEG)
    e = jnp.exp(logits - jnp.max(logits, axis=-1, keepdims=True))
    aff_ref[...] = (e / jnp.sum(e, axis=-1, keepdims=True)).reshape(pb, TM, ROUTER_PAD)


def _merge(hall, mod_all, ya, yb, yc, yd, wg, wbr, wout, ln_g, ln_b, wr_cat, n_tiles):
    B = hall.shape[0]
    pb = _samples_per_step(B)
    row = lambda b, t: (b, t, 0)
    t_rows = n_tiles * TM
    return pl.pallas_call(
        _merge_kernel,
        grid=(B // pb, n_tiles),
        in_specs=[
            pl.BlockSpec((pb, TM, D_MODEL), row),
            pl.BlockSpec((pb, 1, 6, D_MODEL), lambda b, t: (b, t // N_LAT_TILES, 0, 0)),
            pl.BlockSpec((pb, TM, BRANCH_W), row),
            pl.BlockSpec((pb, TM, BRANCH_W), row),
            pl.BlockSpec((pb, TM, BRANCH_W), row),
            pl.BlockSpec((pb, TM, BRANCH_W), row),
            _resident(wg.shape),
            _resident(wbr.shape),
            _resident(wout.shape),
            _resident((1, D_MODEL)),
            _resident((1, D_MODEL)),
            _resident(wr_cat.shape),
        ],
        out_specs=[
            pl.BlockSpec((pb, TM, D_MODEL), row),
            pl.BlockSpec((pb, TM, D_MODEL), row),
            pl.BlockSpec((pb, TM, ROUTER_PAD), row),
        ],
        out_shape=[
            jax.ShapeDtypeStruct((B, t_rows, D_MODEL), F32),
            jax.ShapeDtypeStruct((B, t_rows, D_MODEL), BF16),
            jax.ShapeDtypeStruct((B, t_rows, ROUTER_PAD), F32),
        ],
        compiler_params=_cparams(("arbitrary", "arbitrary")),
    )(hall, mod_all, ya, yb, yc, yd, wg, wbr, wout, ln_g, ln_b, wr_cat)


def _onehot(match):
    return jnp.where(match, 1.0, 0.0).astype(BF16)


def _route_kernel(a_ref, tri_ref, rank_ref, *, cap):
    a = a_ref[...]
    n = a.shape[1]

    def count(mask):
        return jnp.sum(jnp.where(mask, 1.0, 0.0), axis=-1, keepdims=True)

    p = jnp.full((a.shape[0], 1), 2.0, a.dtype)
    for k in range(ROUTE_EXP_BITS - 1, -1, -1):
        cand = p * (2.0 ** -(2 ** k))
        p = jnp.where(count(a >= cand) < cap, cand, p)
    found = count(a >= 0.5 * p) >= cap
    lo = jnp.where(found, 0.5 * p, 0.0)

    def refine(_, carry):
        lo, step = carry
        cand = lo + step
        return jnp.where(count(a >= cand) >= cap, cand, lo), 0.5 * step

    thr, _ = lax.fori_loop(0, ROUTE_MANTISSA_STEPS, refine, (lo, 0.5 * lo))
    above = a > thr
    tie = a == thr
    tri = tri_ref[:n, :n]
    tie_f = jnp.where(tie, 1.0, 0.0)
    ties_before = _dot(tie_f.astype(BF16), tri) - tie_f
    keep = above | (tie & (ties_before < cap - count(above)))
    keep_f = jnp.where(keep, 1.0, 0.0)
    slot = _dot(keep_f.astype(BF16), tri) - keep_f
    rank_ref[...] = jnp.where(keep, slot, -1.0).astype(jnp.int32)


def _route_set(aff_t, tri, cap):
    return pl.pallas_call(
        functools.partial(_route_kernel, cap=cap),
        out_shape=jax.ShapeDtypeStruct(aff_t.shape, jnp.int32),
        compiler_params=pltpu.CompilerParams(vmem_limit_bytes=VMEM_LIMIT),
    )(aff_t, tri[:aff_t.shape[1], :aff_t.shape[1]])


def _gather_kernel(*refs):
    *set_refs, u_ref, o_ref, g_ref = refs
    e = pl.program_id(1)
    row_lo, tok_lo = 0, 0
    for rank_ref, aff_ref in zip(set_refs[0::2], set_refs[1::2]):
        n = rank_ref.shape[1]
        cap = CAP_LAT if n == SEQ else CAP_CTX
        rank = rank_ref[pl.ds(e, 1), :]
        pick = lax.broadcasted_iota(jnp.int32, (cap, n), 0) == rank
        rows = slice(row_lo, row_lo + cap)
        o_ref[0, rows, :] = _dot(_onehot(pick), u_ref[0, tok_lo:tok_lo + n, :]).astype(o_ref.dtype)
        g_ref[0, rows, :] = jnp.sum(jnp.where(pick, aff_ref[pl.ds(e, 1), :], 0.0), axis=-1, keepdims=True)
        row_lo, tok_lo = row_lo + cap, tok_lo + n


def _gather_tokens(ranks, affs, u2):
    B, t_rows = u2.shape[:2]
    cap = CAP_LAT if len(ranks) == 1 else CAP_ALL
    set_specs, set_args = [], []
    for r, a in zip(ranks, affs):
        spec = pl.BlockSpec((N_EXPERTS, r.shape[1]), lambda b, e: (b, 0))
        set_specs += [spec, spec]
        set_args += [r, a]
    return pl.pallas_call(
        _gather_kernel,
        grid=(B, N_EXPERTS),
        in_specs=set_specs + [pl.BlockSpec((1, t_rows, D_MODEL), lambda b, e: (b, 0, 0))],
        out_specs=[
            pl.BlockSpec((1, cap, D_MODEL), lambda b, e: (e, b, 0)),
            pl.BlockSpec((1, cap, 1), lambda b, e: (e, b, 0)),
        ],
        out_shape=[
            jax.ShapeDtypeStruct((N_EXPERTS, B * cap, D_MODEL), BF16),
            jax.ShapeDtypeStruct((N_EXPERTS, B * cap, 1), F32),
        ],
        compiler_params=_cparams(("arbitrary", "arbitrary")),
    )(*set_args, u2)


def _expert_kernel(x_ref, g_ref, wg_ref, wu_ref, wd_ref, o_ref, acc_ref, *, chunk):
    f = pl.program_id(1)

    @pl.when(f == 0)
    def _():
        acc_ref[...] = jnp.zeros_like(acc_ref)

    wg = wg_ref[0, 0].astype(BF16)
    wu = wu_ref[0, 0].astype(BF16)
    wd = wd_ref[0, 0].astype(BF16)
    for c in range(x_ref.shape[1] // chunk):
        rows = pl.ds(c * chunk, chunk)
        x = x_ref[0, rows, :]
        a = _dot(x, wg)
        u = _dot(x, wu)
        acc_ref[rows, :] += _dot((a * jax.nn.sigmoid(a) * u).astype(BF16), wd)

    @pl.when(f == pl.num_programs(1) - 1)
    def _():
        o_ref[0] = (acc_ref[...] * g_ref[0]).astype(o_ref.dtype)


def _experts(xe, gcol, w_gate, w_up, w_down, layer):
    rows = xe.shape[1]
    chunk = next(c for c in (XROW_CHUNK, 512, CAP_ALL, CAP_LAT) if rows % c == 0)
    return pl.pallas_call(
        functools.partial(_expert_kernel, chunk=chunk),
        grid=(N_EXPERTS, EXPERT_FF // FF_TILE),
        in_specs=[
            pl.BlockSpec((1, rows, D_MODEL), lambda e, f: (e, 0, 0)),
            pl.BlockSpec((1, rows, 1), lambda e, f: (e, 0, 0)),
            pl.BlockSpec((1, 1, D_MODEL, FF_TILE), lambda e, f: (layer, e, 0, f)),
            pl.BlockSpec((1, 1, D_MODEL, FF_TILE), lambda e, f: (layer, e, 0, f)),
            pl.BlockSpec((1, 1, FF_TILE, D_MODEL), lambda e, f: (layer, e, f, 0)),
        ],
        out_specs=pl.BlockSpec((1, rows, D_MODEL), lambda e, f: (e, 0, 0)),
        out_shape=jax.ShapeDtypeStruct((N_EXPERTS, rows, D_MODEL), BF16),
        scratch_shapes=[pltpu.VMEM((rows, D_MODEL), F32)],
        compiler_params=_cparams(("arbitrary", "arbitrary")),
    )(xe, gcol, w_gate, w_up, w_down)


SCATTER_WIN = 64


def _scatter_kernel(start_ref, nwin_ref, h_ref, mod_ref, y_ref, g_ref, b_ref, rank_ref, o_ref, fx_ref):
    b = pl.program_id(0)
    t = pl.program_id(1)
    rank = rank_ref[0]

    def spread_ranks(width):
        n = N_EXPERTS * width
        col = lax.broadcasted_iota(jnp.int32, (N_EXPERTS, n), 1)
        spread = _onehot(col // width == lax.broadcasted_iota(jnp.int32, (N_EXPERTS, n), 0))
        return _dot(rank.astype(F32).astype(BF16), spread)

    def finish(fx):
        g2 = mod_ref[0, 0, 5:6, :]
        o_ref[0] = _layer_norm(DN_ALPHA * h_ref[0] + g2 * fx, g_ref[...], b_ref[...])

    def latent():
        n = N_EXPERTS * SCATTER_WIN
        rank_cols = spread_ranks(SCATTER_WIN)
        lane = lax.broadcasted_iota(jnp.int32, (1, n), 1)
        block = lane // SCATTER_WIN
        offset = (lane % SCATTER_WIN).astype(F32)
        fx_ref[...] = jnp.zeros_like(fx_ref)

        def one_pass(k, carry):
            want = jnp.zeros((1, n), F32)
            have = jnp.zeros((1, n), F32)
            rows = []
            for e in range(N_EXPERTS):
                w = start_ref[b, t, e] + k * SCATTER_WIN
                lo = pl.multiple_of(jnp.minimum(w, CAP_LAT - SCATTER_WIN), 16)
                rows.append(y_ref[e, 0, pl.ds(lo, SCATTER_WIN), :])
                want = jnp.where(block == e, w.astype(F32), want)
                have = jnp.where(block == e, lo.astype(F32), have)
            hit = (rank_cols - have == offset) & (rank_cols >= want)
            fx_ref[...] += _dot(_onehot(hit), jnp.concatenate(rows, axis=0))
            return carry

        lax.fori_loop(0, nwin_ref[b, t], one_pass, 0)
        finish(fx_ref[...])

    def context():
        n = N_EXPERTS * CAP_CTX
        slot = (lax.broadcasted_iota(jnp.int32, (TM, n), 1) % CAP_CTX).astype(F32)
        y = y_ref[:, 0, CAP_LAT:CAP_ALL, :].reshape(n, D_MODEL)
        finish(_dot(_onehot(spread_ranks(CAP_CTX) == slot), y))

    if y_ref.shape[2] == CAP_LAT:
        latent()
    else:
        pl.when(t < N_LAT_TILES)(latent)
        pl.when(t == N_LAT_TILES)(context)


def _scatter_windows(rank_lat):
    B = rank_lat.shape[0] // N_EXPERTS
    cnt = jnp.sum((rank_lat >= 0).reshape(B, N_EXPERTS, N_LAT_TILES, TM), axis=-1, dtype=jnp.int32)
    first = jnp.cumsum(cnt, axis=-1) - cnt
    start = (first // 16) * 16
    nwin = jnp.max((first - start + cnt + SCATTER_WIN - 1) // SCATTER_WIN, axis=1)
    return start.transpose(0, 2, 1), nwin


def _scatter_ln(h1, mod_all, ye, ln_g, ln_b, rank_tok, rank_lat):
    B = h1.shape[0]
    cap = ye.shape[2]
    n_tiles = rank_tok.shape[1] // TM
    start, nwin = _scatter_windows(rank_lat)
    grid_spec = pltpu.PrefetchScalarGridSpec(
        num_scalar_prefetch=2,
        grid=(B, n_tiles),
        in_specs=[
            pl.BlockSpec((1, TM, D_MODEL), lambda b, t, *_: (b, t, 0)),
            pl.BlockSpec((1, 1, 6, D_MODEL), lambda b, t, *_: (b, t // N_LAT_TILES, 0, 0)),
            pl.BlockSpec((N_EXPERTS, 1, cap, D_MODEL), lambda b, t, *_: (0, b, 0, 0)),
            pl.BlockSpec((1, D_MODEL), lambda b, t, *_: (0, 0)),
            pl.BlockSpec((1, D_MODEL), lambda b, t, *_: (0, 0)),
            pl.BlockSpec((1, TM, N_EXPERTS), lambda b, t, *_: (b, t, 0)),
        ],
        out_specs=pl.BlockSpec((1, TM, D_MODEL), lambda b, t, *_: (b, t, 0)),
        scratch_shapes=[pltpu.VMEM((TM, D_MODEL), F32)],
    )
    return pl.pallas_call(
        _scatter_kernel,
        grid_spec=grid_spec,
        out_shape=jax.ShapeDtypeStruct((B, n_tiles * TM, D_MODEL), F32),
        compiler_params=_cparams(("arbitrary", "arbitrary")),
    )(start, nwin, h1, mod_all, ye, ln_g, ln_b, rank_tok)


def _rope_tables():
    n_freq = MLA_ROPE // 4
    inv = ROPE_BASE ** (-jnp.arange(n_freq, dtype=F32) / n_freq)
    t = jnp.arange(SEQ)
    row = (t // GRID_W).astype(F32)
    col = (t % GRID_W).astype(F32)
    ang = jnp.concatenate([row[:, None] * inv, col[:, None] * inv], axis=-1)
    cos, sin = jnp.cos(ang), jnp.sin(ang)
    ones = jnp.ones((SEQ, MLA_NOPE), F32)
    pad1 = jnp.ones((SEQ, MLA_HEAD_PAD - MLA_NOPE - MLA_ROPE), F32)
    cos_t = jnp.concatenate([ones, cos, cos, pad1], axis=-1)
    sin_t = jnp.concatenate([0 * ones, -sin, sin, 0 * pad1], axis=-1)
    cos_t = jnp.concatenate([cos_t, jnp.ones((CTX_LEN, MLA_HEAD_PAD), F32)], axis=0)
    sin_t = jnp.concatenate([sin_t, jnp.zeros((CTX_LEN, MLA_HEAD_PAD), F32)], axis=0)
    return cos_t, sin_t


def _dft_tables():
    def block(n):
        split = 64
        k = jnp.arange(n, dtype=jnp.int32)[None, :]
        jh = jnp.arange(n // split, dtype=jnp.int32)[:, None]
        jl = jnp.arange(split, dtype=jnp.int32)[:, None]
        ang_h = ((jh * split * k) % n).astype(F32) * (2.0 * np.pi / n)
        ang_l = ((jl * k) % n).astype(F32) * (2.0 * np.pi / n)
        ch, sh = jnp.cos(ang_h)[:, None, :], jnp.sin(ang_h)[:, None, :]
        cl_, sl_ = jnp.cos(ang_l)[None, :, :], jnp.sin(ang_l)[None, :, :]
        sc = 1.0 / np.sqrt(n * FFT_CH)
        return ((ch * cl_ - sh * sl_) * sc).reshape(n, n), (-(sh * cl_ + ch * sl_) * sc).reshape(n, n)

    cl, sl = block(SEQ)
    cc, sc_ = block(CTX_LEN)

    def diag(a, b):
        top = jnp.concatenate([a, jnp.zeros((SEQ, CTX_LEN), F32)], axis=1)
        bot = jnp.concatenate([jnp.zeros((CTX_LEN, SEQ), F32), b], axis=1)
        return jnp.concatenate([top, bot], axis=0).astype(BF16)

    j = jnp.arange(FFT_CH, dtype=jnp.int32)
    ang = ((j[:, None] * j[None, :]) % FFT_CH).astype(F32) * (2.0 * np.pi / FFT_CH)
    eye = jnp.eye(FFT_GROUPS, dtype=F32)
    bd = jnp.concatenate([jnp.kron(eye, jnp.cos(ang)), jnp.kron(eye, jnp.sin(ang))], axis=1).astype(BF16)
    return diag(cl, cc), diag(sl, sc_), bd


def _na_bias_index():
    R = SEQ // GRID_W
    pats = [(0, 0), (NA_QROWS, 0), (R - NA_QROWS, R - NA_KROWS)]
    q = np.arange(TM)
    k = np.arange(NA_KWIN)
    qi, qc = q // GRID_W, q % GRID_W
    kj, kc = k // GRID_W, k % GRID_W
    c0 = np.clip(qc - NA_WIN_COLS // 2, 0, GRID_W - NA_WIN_COLS)
    col_in = (kc[None, :] >= c0[:, None]) & (kc[None, :] < c0[:, None] + NA_WIN_COLS)
    col_off = np.clip(kc[None, :] - qc[:, None], -(NA_WIN_COLS - 1), NA_WIN_COLS - 1) + (NA_WIN_COLS - 1)
    del col_off
    row_offs, valids = [], []
    for rb, ks in pats:
        qr = rb + np.arange(NA_QROWS)
        kr = ks + np.arange(NA_KROWS)
        r0 = np.clip(qr - NA_WIN_ROWS // 2, 0, R - NA_WIN_ROWS)
        row_in = (kr[None, :] >= r0[:, None]) & (kr[None, :] < r0[:, None] + NA_WIN_ROWS)
        row_offs.append(np.where(row_in, kr[None, :] - qr[:, None] + (NA_WIN_ROWS - 1), -1))
        valids.append(row_in[qi][:, kj] & col_in)
    return np.stack(row_offs), np.stack(valids)


def _na_bias(rpb):
    row_off, valid = _na_bias_index()
    reach = GRID_W - NA_WIN_COLS
    padded = jnp.pad(rpb, ((0, 0), (0, 0), (0, 0), (reach, reach)))
    band = jnp.stack([padded[..., GRID_W - 1 - qc:2 * GRID_W - 1 - qc] for qc in range(GRID_W)], axis=-2)
    masked = jnp.zeros(band.shape[:2] + band.shape[3:], F32)
    pats = []
    for p in range(row_off.shape[0]):
        qrows = []
        for qi in range(NA_QROWS):
            blocks = [masked if row_off[p, qi, kj] < 0 else band[:, :, row_off[p, qi, kj]] for kj in range(NA_KROWS)]
            qrows.append(jnp.concatenate(blocks, axis=-1))
        pats.append(jnp.concatenate(qrows, axis=-2))
    b = jnp.stack(pats, axis=1)
    b = jnp.where(jnp.asarray(valid)[None, :, None], b, NEG)
    return jnp.concatenate([b, jnp.zeros(b.shape[:-1] + (CTX_LEN,), F32)], axis=-1)


def _rearranged_in_weights(w_in):
    z = lambda n: jnp.zeros((D_MODEL, n), F32)
    kr = w_in[:, 1664:1696]
    kr_e, kr_o = kr[:, 0::2], kr[:, 1::2]
    pad = MLA_HEAD_PAD - MLA_NOPE - MLA_ROPE
    krb = jnp.concatenate([z(MLA_NOPE), kr_e, kr_o, z(pad)], axis=1)
    krs = jnp.concatenate([z(MLA_NOPE), kr_o, kr_e, z(pad)], axis=1)
    wsm = jnp.concatenate([w_in[:, :1664], krb, krs], axis=1).astype(BF16)
    return wsm, w_in[:, 1696:].astype(BF16)


def _rearranged_mla_weights(w_uq, w_ukv):
    pad = MLA_HEAD_PAD - MLA_NOPE - MLA_ROPE
    wq = w_uq.reshape(MLA_Q_RANK, MLA_HEADS, MLA_NOPE + MLA_ROPE)
    nope, rope = wq[..., :MLA_NOPE], wq[..., MLA_NOPE:]
    r_e, r_o = rope[..., 0::2], rope[..., 1::2]
    zq = jnp.zeros((MLA_Q_RANK, MLA_HEADS, pad), F32)
    q_main = jnp.concatenate([nope, r_e, r_o, zq], axis=-1).reshape(MLA_Q_RANK, -1)
    q_swap = jnp.concatenate([nope, r_o, r_e, zq], axis=-1).reshape(MLA_Q_RANK, -1)
    wuq2 = jnp.concatenate([q_main, q_swap], axis=1).astype(BF16)
    wkv = w_ukv.reshape(MLA_KV_RANK, MLA_HEADS, MLA_NOPE + MLA_V)
    k_nope, v = wkv[..., :MLA_NOPE], wkv[..., MLA_NOPE:]
    zk = jnp.zeros((MLA_KV_RANK, MLA_HEADS, MLA_HEAD_PAD - MLA_NOPE), F32)
    k_main = jnp.concatenate([k_nope, zk], axis=-1).reshape(MLA_KV_RANK, -1)
    wukv2 = jnp.concatenate([k_main, v.reshape(MLA_KV_RANK, -1)], axis=1).astype(BF16)
    return wuq2, wukv2


def _route(aff, tri, with_ctx):
    B = aff.shape[0]
    a = aff.transpose(0, 2, 1).reshape(B * N_EXPERTS, -1)
    affs = [a[:, :SEQ]] + ([a[:, SEQ:]] if with_ctx else [])
    ranks = [_route_set(s, tri, cap) for s, cap in zip(affs, (CAP_LAT, CAP_CTX))]
    rank_tok = jnp.concatenate([r.reshape(B, N_EXPERTS, -1) for r in ranks], axis=-1).transpose(0, 2, 1)
    return affs, ranks, rank_tok


def kernel(x, c, ctx, c_ctx, w_mod, b_mod, w_in, na_rpb, pool_w, pool_scale, mla_q_norm, mla_w_uq, mla_kv_norm,
           mla_w_ukv, w_branch, w_out, ln1_g, ln1_b, w_router, w_gate, w_up, w_down, ln2_g, ln2_b):
    B = x.shape[0]
    assert x.shape == (B, SEQ, D_MODEL) and ctx.shape == (B, CTX_LEN, D_MODEL) and B + 1 <= 16

    cc = jnp.concatenate([c, c_ctx[None], jnp.zeros((16 - B - 1, D_MODEL), F32)], axis=0)
    mod = _modulation(cc, w_mod, b_mod).reshape(DEPTH, 16, 6, D_MODEL)
    cos_t, sin_t = _rope_tables()
    cmat, smat, bd = _dft_tables()
    na_bias = _na_bias(na_rpb)
    tri = jnp.triu(jnp.ones((SEQ, SEQ), BF16))
    hall = jnp.concatenate([x, ctx], axis=1)

    for l in range(DEPTH):
        with_ctx = l < DEPTH - 1
        n_tiles = N_TILES if with_ctx else N_LAT_TILES
        mod_all = jnp.stack([mod[l, :B], jnp.broadcast_to(mod[l, B], (B, 6, D_MODEL))], axis=1)
        wsm, wgates = _rearranged_in_weights(w_in[l])
        wuq2, wukv2 = _rearranged_mla_weights(mla_w_uq[l], mla_w_ukv[l])
        qa, ka, va, up, xcs, qm, km, vm = _inproj(
            hall, mod_all, wsm, mla_q_norm[l][None], mla_kv_norm[l][None], wuq2, wukv2, cos_t, sin_t, bd)
        ya = _na_attention(qa, ka, va, na_bias, l, n_tiles)
        pool_bd = jax.scipy.linalg.block_diag(*[pool_w[l, g] for g in range(POOL_GROUPS)]).astype(BF16)
        yb = _pool(up, pool_bd, pool_scale[l][None])
        yc = _dft_positions(cmat, smat, xcs)
        yd = _mla_attention(qm, km, vm, n_tiles)
        wr = jnp.pad(w_router[l], ((0, 0), (0, ROUTER_PAD - N_EXPERTS)))
        wr_hi = wr.astype(BF16)
        wr_cat = jnp.concatenate([wr_hi, (wr - wr_hi.astype(F32)).astype(BF16)], axis=1)
        h1, u2, aff = _merge(hall, mod_all, ya, yb, yc, yd, wgates, w_branch[l].astype(BF16), w_out[l].astype(BF16),
                             ln1_g[l][None], ln1_b[l][None], wr_cat, n_tiles)
        affs, ranks, rank_tok = _route(aff[:, :, :N_EXPERTS], tri, with_ctx)
        xe, gcol = _gather_tokens(ranks, affs, u2)
        ye = _experts(xe, gcol, w_gate, w_up, w_down, l)
        hall = _scatter_ln(h1, mod_all, ye.reshape(N_EXPERTS, B, -1, D_MODEL), ln2_g[l][None], ln2_b[l][None], rank_tok,
                           ranks[0])
    return hall
```

```python
import functools

import numpy as np
import jax
import jax.numpy as jnp
from jax import lax
from jax.experimental import pallas as pl
from jax.experimental.pallas import tpu as pltpu

F32 = jnp.float32
BF16 = jnp.bfloat16

D_MODEL = 1024
DEPTH = 4
GRID_W = 64
SEQ = 2048
CTX_LEN = 256
T_ALL = SEQ + CTX_LEN
HEAD_DIM = 64
NA_HEADS = 4
NA_WIN_ROWS = 8
NA_WIN_COLS = 16
NA_SCALE = HEAD_DIM ** -0.5
POOL_GROUPS = 4
POOL_CH = 64
POOL_WINDOWS = (2, 4, 8, 16)
FFT_GROUPS = 4
FFT_CH = 64
MLA_HEADS = 4
MLA_Q_RANK = 256
MLA_KV_RANK = 128
MLA_NOPE = 64
MLA_ROPE = 32
MLA_V = 64
MLA_SCALE = (MLA_NOPE + MLA_ROPE) ** -0.5
ROPE_BASE = 10000.0
BRANCH_W = 256
N_BRANCH = 4
N_EXPERTS = 16
EXPERT_FF = 2048
EC_CAPACITY = 2
DN_ALPHA = (2 * DEPTH) ** 0.25
LN_EPS = 1e-5
NEG = -1e30

LANE = 128
TM = 256
N_TILES = T_ALL // TM
N_LAT_TILES = SEQ // TM
MLA_HEAD_PAD = 128
NA_QROWS = TM // GRID_W
NA_KROWS = 12
NA_KWIN = NA_KROWS * GRID_W
NA_KALL = NA_KWIN + CTX_LEN
CAP_LAT = EC_CAPACITY * SEQ // N_EXPERTS
CAP_CTX = EC_CAPACITY * CTX_LEN // N_EXPERTS
CAP_ALL = CAP_LAT + CAP_CTX
ROUTER_PAD = 128
ROUTE_EXP_BITS = 11
ROUTE_MANTISSA_STEPS = 52
FF_TILE = 512
XROW_CHUNK = 576
VMEM_LIMIT = 56 * 1024 * 1024

C_QA, C_KA, C_VA, C_UP, C_UF, C_CQ, C_CKV, C_KR, C_KRS, C_END = 0, 256, 512, 768, 1024, 1280, 1536, 1664, 1792, 1920


def _cparams(sem):
    return pltpu.CompilerParams(dimension_semantics=sem, vmem_limit_bytes=VMEM_LIMIT)


def _dot(a, b):
    return jnp.dot(a, b, preferred_element_type=F32)


def _dot_t(a, b):
    return lax.dot_general(a, b, (((1,), (1,)), ((), ())), preferred_element_type=F32)


def _layer_norm(x, g, b):
    mu = jnp.mean(x, axis=-1, keepdims=True)
    xc = x - mu
    var = jnp.mean(xc * xc, axis=-1, keepdims=True)
    return xc * lax.rsqrt(var + LN_EPS) * g + b


def _mod_kernel(c_ref, w_ref, b_ref, o_ref):
    c = c_ref[...]
    s = c * jax.nn.sigmoid(c)
    w = w_ref[0]
    s_hi = s.astype(BF16)
    s_lo = (s - s_hi.astype(F32)).astype(BF16)
    w_hi = w.astype(BF16)
    w_lo = (w - w_hi.astype(F32)).astype(BF16)
    o_ref[0] = _dot(s_hi, w_hi) + _dot(s_hi, w_lo) + _dot(s_lo, w_hi) + b_ref[0]


def _modulation(cc, w_mod, b_mod):
    rows = cc.shape[0]
    n6 = 6 * D_MODEL
    tn = 1536
    return pl.pallas_call(
        _mod_kernel,
        grid=(DEPTH, n6 // tn),
        in_specs=[
            pl.BlockSpec((rows, D_MODEL), lambda l, j: (0, 0)),
            pl.BlockSpec((1, D_MODEL, tn), lambda l, j: (l, 0, j)),
            pl.BlockSpec((1, 1, tn), lambda l, j: (l, 0, j)),
        ],
        out_specs=pl.BlockSpec((1, rows, tn), lambda l, j: (l, 0, j)),
        out_shape=jax.ShapeDtypeStruct((DEPTH, rows, n6), F32),
        compiler_params=_cparams(("arbitrary", "arbitrary")),
    )(cc, w_mod, b_mod.reshape(DEPTH, 1, n6))


def _inproj_kernel(h_ref, mod_ref, wsm_ref, qn_ref, kvn_ref, wuq_ref, wukv_ref, cos_ref, sin_ref, bd_ref,
                   qa_ref, ka_ref, va_ref, up_ref, xcs_ref, qm_ref, km_ref, vm_ref):
    pb = h_ref.shape[0]

    def put(ref, val):
        ref[...] = val.astype(ref.dtype).reshape(ref.shape)

    sh1 = mod_ref[:, 0, 0:1, :]
    sc1 = mod_ref[:, 0, 1:2, :]
    u = (h_ref[...] * (1.0 + sc1) + sh1).astype(BF16).reshape(pb * TM, D_MODEL)
    z = _dot(u, wsm_ref[...])
    put(qa_ref, z[:, C_QA:C_KA] * NA_SCALE)
    put(ka_ref, z[:, C_KA:C_VA])
    put(va_ref, z[:, C_VA:C_UP])
    put(up_ref, z[:, C_UP:C_UF])
    put(xcs_ref, _dot(z[:, C_UF:C_CQ].astype(BF16), bd_ref[...]))

    cos = jnp.concatenate([cos_ref[...]] * pb, axis=0)
    sin = jnp.concatenate([sin_ref[...]] * pb, axis=0)
    cos4 = jnp.concatenate([cos] * MLA_HEADS, axis=-1)
    sin4 = jnp.concatenate([sin] * MLA_HEADS, axis=-1)

    cq = z[:, C_CQ:C_CKV]
    nq = cq * lax.rsqrt(jnp.mean(cq * cq, axis=-1, keepdims=True) + LN_EPS) * qn_ref[...]
    q2 = _dot(nq.astype(BF16), wuq_ref[...])
    hw = MLA_HEADS * MLA_HEAD_PAD
    put(qm_ref, (q2[:, :hw] * cos4 + q2[:, hw:] * sin4) * MLA_SCALE)

    ckv = z[:, C_CKV:C_KR]
    nkv = ckv * lax.rsqrt(jnp.mean(ckv * ckv, axis=-1, keepdims=True) + LN_EPS) * kvn_ref[...]
    kv2 = _dot(nkv.astype(BF16), wukv_ref[...])
    kr = z[:, C_KR:C_KRS] * cos + z[:, C_KRS:C_END] * sin
    put(km_ref, kv2[:, :hw] + jnp.concatenate([kr] * MLA_HEADS, axis=-1))
    put(vm_ref, kv2[:, hw:])


def _samples_per_step(B):
    return 2 if B % 2 == 0 else 1


def _resident(shape):
    return pl.BlockSpec(shape, lambda *_: (0,) * len(shape), pipeline_mode=pl.Buffered(1))


def _inproj(hall, mod_all, wsm, qn, kvn, wuq2, wukv2, cos_t, sin_t, bd):
    B = hall.shape[0]
    pb = _samples_per_step(B)
    hw = MLA_HEADS * MLA_HEAD_PAD
    row = lambda b, t: (b, t, 0)
    outs = [
        (256, BF16), (256, BF16), (256, BF16), (256, F32), (512, BF16), (hw, BF16), (hw, BF16), (MLA_HEADS * MLA_V, BF16),
    ]
    return pl.pallas_call(
        _inproj_kernel,
        grid=(B // pb, N_TILES),
        in_specs=[
            pl.BlockSpec((pb, TM, D_MODEL), row),
            pl.BlockSpec((pb, 1, 6, D_MODEL), lambda b, t: (b, t // N_LAT_TILES, 0, 0)),
            _resident(wsm.shape),
            _resident(qn.shape),
            _resident(kvn.shape),
            _resident(wuq2.shape),
            _resident(wukv2.shape),
            pl.BlockSpec((TM, LANE), lambda b, t: (t, 0)),
            pl.BlockSpec((TM, LANE), lambda b, t: (t, 0)),
            _resident(bd.shape),
        ],
        out_specs=[pl.BlockSpec((pb, TM, w), row) for w, _ in outs],
        out_shape=[jax.ShapeDtypeStruct((B, T_ALL, w), dt) for w, dt in outs],
        compiler_params=_cparams(("arbitrary", "arbitrary")),
    )(hall, mod_all, wsm, qn, kvn, wuq2, wukv2, cos_t, sin_t, bd)


def _softmax_pv(s, v):
    m = jnp.max(s, axis=-1, keepdims=True)
    p = jnp.exp(s - m)
    l = jnp.sum(p, axis=-1, keepdims=True)
    return _dot(p.astype(BF16), v) / l


def _na_kernel(q_ref, k_ref, v_ref, bias_ref, o_ref):
    t = pl.program_id(0)

    @pl.when(t < N_LAT_TILES)
    def _():
        ks = pl.multiple_of(jnp.clip(t * NA_QROWS - NA_WIN_ROWS // 2, 0, SEQ // GRID_W - NA_KROWS) * GRID_W, GRID_W)
        kk = jnp.concatenate([k_ref[0, pl.ds(ks, NA_KWIN), :], k_ref[0, SEQ:T_ALL, :]], axis=0)
        vv = jnp.concatenate([v_ref[0, pl.ds(ks, NA_KWIN), :], v_ref[0, SEQ:T_ALL, :]], axis=0)
        q = q_ref[0]
        sls = [slice(h * HEAD_DIM, (h + 1) * HEAD_DIM) for h in range(NA_HEADS)]
        ss = [_dot_t(q[:, sl], kk[:, sl]) + bias_ref[0, 0, h] for h, sl in enumerate(sls)]
        ms = [jnp.max(s, axis=-1, keepdims=True) for s in ss]
        ps = [jnp.exp(s - m).astype(BF16) for s, m in zip(ss, ms)]
        ones = _onehot(lax.broadcasted_iota(jnp.int32, (NA_KALL, HEAD_DIM), 1) == 0)
        oa = [_dot(p, jnp.concatenate([vv[:, sl], ones], axis=-1)) for p, sl in zip(ps, sls)]
        os_ = [o[:, :HEAD_DIM] / o[:, HEAD_DIM:HEAD_DIM + 1] for o in oa]
        o_ref[0] = jnp.concatenate(os_, axis=-1).astype(o_ref.dtype)

    @pl.when(t == N_LAT_TILES)
    def _():
        kk = k_ref[0, SEQ:T_ALL, :]
        vv = v_ref[0, SEQ:T_ALL, :]
        q = q_ref[0]
        for h in range(NA_HEADS):
            sl = slice(h * HEAD_DIM, (h + 1) * HEAD_DIM)
            o_ref[0, :, sl] = _softmax_pv(_dot_t(q[:, sl], kk[:, sl]), vv[:, sl]).astype(o_ref.dtype)


def _na_attention(qa, ka, va, bias, layer, n_tiles):
    B = qa.shape[0]

    def bias_idx(t, b):
        return (layer, jnp.where(t == 0, 0, jnp.where(t >= N_LAT_TILES - 1, 2, 1)), 0, 0, 0)

    return pl.pallas_call(
        _na_kernel,
        grid=(n_tiles, B),
        in_specs=[
            pl.BlockSpec((1, TM, 256), lambda t, b: (b, t, 0)),
            pl.BlockSpec((1, T_ALL, 256), lambda t, b: (b, 0, 0)),
            pl.BlockSpec((1, T_ALL, 256), lambda t, b: (b, 0, 0)),
            pl.BlockSpec((1, 1, NA_HEADS, TM, NA_KALL), bias_idx),
        ],
        out_specs=pl.BlockSpec((1, TM, 256), lambda t, b: (b, t, 0)),
        out_shape=jax.ShapeDtypeStruct((B, n_tiles * TM, 256), BF16),
        compiler_params=_cparams(("arbitrary", "arbitrary")),
    )(qa, ka, va, bias)


def _mla_kernel(q_ref, k_ref, v_ref, o_ref):
    t = pl.program_id(1)

    def run(key_lo, n_keys):
        keys = slice(key_lo, key_lo + n_keys)
        ksls = [slice(h * MLA_HEAD_PAD, (h + 1) * MLA_HEAD_PAD) for h in range(MLA_HEADS)]
        vsls = [slice(h * MLA_V, (h + 1) * MLA_V) for h in range(MLA_HEADS)]
        ss = [_dot_t(q_ref[0, :, ksl], k_ref[0, keys, ksl]) for ksl in ksls]
        ms = [jnp.max(s, axis=-1, keepdims=True) for s in ss]
        ps = [jnp.exp(s - m).astype(BF16) for s, m in zip(ss, ms)]
        ones = _onehot(lax.broadcasted_iota(jnp.int32, (n_keys, MLA_V), 1) == 0)
        oa = [_dot(p, jnp.concatenate([v_ref[0, keys, vsl], ones], axis=-1)) for p, vsl in zip(ps, vsls)]
        os_ = [o[:, :MLA_V] / o[:, MLA_V:MLA_V + 1] for o in oa]
        o_ref[0] = jnp.concatenate(os_, axis=-1).astype(o_ref.dtype)

    @pl.when(t < N_LAT_TILES)
    def _():
        run(0, T_ALL)

    @pl.when(t == N_LAT_TILES)
    def _():
        run(SEQ, CTX_LEN)


def _mla_attention(qm, km, vm, n_tiles):
    B = qm.shape[0]
    hw = MLA_HEADS * MLA_HEAD_PAD
    vw = MLA_HEADS * MLA_V
    return pl.pallas_call(
        _mla_kernel,
        grid=(B, n_tiles),
        in_specs=[
            pl.BlockSpec((1, TM, hw), lambda b, t: (b, t, 0)),
            pl.BlockSpec((1, T_ALL, hw), lambda b, t: (b, 0, 0)),
            pl.BlockSpec((1, T_ALL, vw), lambda b, t: (b, 0, 0)),
        ],
        out_specs=pl.BlockSpec((1, TM, vw), lambda b, t: (b, t, 0)),
        out_shape=jax.ShapeDtypeStruct((B, n_tiles * TM, vw), BF16),
        compiler_params=_cparams(("arbitrary", "arbitrary")),
    )(qm, km, vm)


POOL_PAD = POOL_WINDOWS[-1] // 2


def _pool_kernel(u_ref, w_ref, scale_ref, o_ref, pad_ref):
    width = POOL_GROUPS * POOL_CH
    lane = lax.broadcasted_iota(jnp.int32, (1, width), 1)
    half = jnp.left_shift(1, lane // POOL_CH)
    zeros = jnp.zeros((POOL_PAD, width), F32)
    base = 0
    for lo, n in ((0, SEQ), (SEQ, CTX_LEN)):
        x = u_ref[0, lo:lo + n, :]
        pad_ref[base:base + POOL_PAD, :] = zeros
        pad_ref[base + POOL_PAD:base + POOL_PAD + n, :] = x
        pad_ref[base + POOL_PAD + n:base + 2 * POOL_PAD + n, :] = zeros

        def shifted(d):
            return pad_ref[base + POOL_PAD + d:base + POOL_PAD + d + n, :]

        sums = {}
        acc = shifted(-1) + x
        sums[1] = acc
        for h in (2, 4, 8):
            for d in list(range(-h, -h // 2)) + list(range(h // 2, h)):
                acc = acc + shifted(d)
            sums[h] = acc
        win = jnp.where(half == 1, sums[1], jnp.where(half == 2, sums[2], jnp.where(half == 4, sums[4], sums[8])))
        t = lax.broadcasted_iota(jnp.int32, (n, 1), 0)
        cnt = jnp.minimum(t + half, n) - jnp.maximum(t - half, 0)
        y = (win / cnt.astype(F32) - x).astype(BF16)
        o_ref[0, lo:lo + n, :] = (_dot(y, w_ref[...]) * scale_ref[...]).astype(o_ref.dtype)
        base += n + 2 * POOL_PAD


def _pool(up, w_bd, scale):
    B = up.shape[0]
    return pl.pallas_call(
        _pool_kernel,
        grid=(B,),
        in_specs=[
            pl.BlockSpec((1, T_ALL, 256), lambda b: (b, 0, 0)),
            pl.BlockSpec((256, 256), lambda b: (0, 0)),
            pl.BlockSpec((1, 256), lambda b: (0, 0)),
        ],
        out_specs=pl.BlockSpec((1, T_ALL, 256), lambda b: (b, 0, 0)),
        out_shape=jax.ShapeDtypeStruct((B, T_ALL, 256), BF16),
        scratch_shapes=[pltpu.VMEM((T_ALL + 4 * POOL_PAD, POOL_GROUPS * POOL_CH), F32)],
        compiler_params=_cparams(("arbitrary",)),
    )(up, w_bd, scale)


DFT_ROWS = 768


def _dft_kernel(c_ref, s_ref, x_ref, o_ref):
    x = x_ref[0]
    w = FFT_GROUPS * FFT_CH
    o_ref[0] = (_dot(c_ref[...], x[:, :w]) + _dot(s_ref[...], x[:, w:])).astype(o_ref.dtype)


def _dft_positions(cmat, smat, xcs):
    B = xcs.shape[0]
    return pl.pallas_call(
        _dft_kernel,
        grid=(T_ALL // DFT_ROWS, B),
        in_specs=[
            pl.BlockSpec((DFT_ROWS, T_ALL), lambda i, b: (i, 0)),
            pl.BlockSpec((DFT_ROWS, T_ALL), lambda i, b: (i, 0)),
            pl.BlockSpec((1, T_ALL, 512), lambda i, b: (b, 0, 0)),
        ],
        out_specs=pl.BlockSpec((1, DFT_ROWS, 256), lambda i, b: (b, i, 0)),
        out_shape=jax.ShapeDtypeStruct((B, T_ALL, 256), BF16),
        compiler_params=_cparams(("arbitrary", "arbitrary")),
    )(cmat, smat, xcs)


def _merge_kernel(h_ref, mod_ref, ya_ref, yb_ref, yc_ref, yd_ref, wg_ref, wbr_ref, wout_ref, g_ref, b_ref, wr_ref,
                  h1_ref, u2_ref, aff_ref):
    pb = h_ref.shape[0]
    rows = pb * TM
    h = h_ref[...]
    sh1 = mod_ref[:, 0, 0:1, :]
    sc1 = mod_ref[:, 0, 1:2, :]
    g1 = mod_ref[:, 0, 2:3, :]
    sh2 = mod_ref[:, 0, 3:4, :]
    sc2 = mod_ref[:, 0, 4:5, :]
    u = (h * (1.0 + sc1) + sh1).astype(BF16).reshape(rows, D_MODEL)
    ys = (ya_ref, yb_ref, yc_ref, yd_ref)
    acc = jnp.zeros((rows, D_MODEL), F32)
    for n in range(N_BRANCH):
        gate = jax.nn.sigmoid(_dot(u, wg_ref[:, n * D_MODEL:(n + 1) * D_MODEL]))
        acc = acc + gate * _dot(ys[n][...].reshape(rows, BRANCH_W), wbr_ref[n])
    mix = _dot(acc.astype(BF16), wout_ref[...]).reshape(pb, TM, D_MODEL)
    h1 = _layer_norm(DN_ALPHA * h + g1 * mix, g_ref[...], b_ref[...])
    h1_ref[...] = h1
    u2 = h1 * (1.0 + sc2) + sh2
    u2_hi = u2.astype(BF16)
    u2_ref[...] = u2_hi
    u2_lo = (u2 - u2_hi.astype(F32)).astype(BF16).reshape(rows, D_MODEL)
    hi_terms = _dot(u2_hi.reshape(rows, D_MODEL), wr_ref[...])
    logits = hi_terms[:, :ROUTER_PAD] + hi_terms[:, ROUTER_PAD:] + _dot(u2_lo, wr_ref[:, :ROUTER_PAD])
    lane = lax.broadcasted_iota(jnp.int32, (1, ROUTER_PAD), 1)
    logits = jnp.where(lane < N_EXPERTS, logits, NEG)
    e = jnp.exp(logits - jnp.max(logits, axis=-1, keepdims=True))
    aff_ref[...] = (e / jnp.sum(e, axis=-1, keepdims=True)).reshape(pb, TM, ROUTER_PAD)


def _merge(hall, mod_all, ya, yb, yc, yd, wg, wbr, wout, ln_g, ln_b, wr_cat, n_tiles):
    B = hall.shape[0]
    pb = _samples_per_step(B)
    row = lambda b, t: (b, t, 0)
    t_rows = n_tiles * TM
    return pl.pallas_call(
        _merge_kernel,
        grid=(B // pb, n_tiles),
        in_specs=[
            pl.BlockSpec((pb, TM, D_MODEL), row),
            pl.BlockSpec((pb, 1, 6, D_MODEL), lambda b, t: (b, t // N_LAT_TILES, 0, 0)),
            pl.BlockSpec((pb, TM, BRANCH_W), row),
            pl.BlockSpec((pb, TM, BRANCH_W), row),
            pl.BlockSpec((pb, TM, BRANCH_W), row),
            pl.BlockSpec((pb, TM, BRANCH_W), row),
            _resident(wg.shape),
            _resident(wbr.shape),
            _resident(wout.shape),
            _resident((1, D_MODEL)),
            _resident((1, D_MODEL)),
            _resident(wr_cat.shape),
        ],
        out_specs=[
            pl.BlockSpec((pb, TM, D_MODEL), row),
            pl.BlockSpec((pb, TM, D_MODEL), row),
            pl.BlockSpec((pb, TM, ROUTER_PAD), row),
        ],
        out_shape=[
            jax.ShapeDtypeStruct((B, t_rows, D_MODEL), F32),
            jax.ShapeDtypeStruct((B, t_rows, D_MODEL), BF16),
            jax.ShapeDtypeStruct((B, t_rows, ROUTER_PAD), F32),
        ],
        compiler_params=_cparams(("arbitrary", "arbitrary")),
    )(hall, mod_all, ya, yb, yc, yd, wg, wbr, wout, ln_g, ln_b, wr_cat)


def _onehot(match):
    return jnp.where(match, 1.0, 0.0).astype(BF16)


def _route_kernel(a_ref, tri_ref, rank_ref, *, cap):
    a = a_ref[...]
    n = a.shape[1]

    def count(mask):
        return jnp.sum(jnp.where(mask, 1.0, 0.0), axis=-1, keepdims=True)

    p = jnp.full((a.shape[0], 1), 2.0, a.dtype)
    for k in range(ROUTE_EXP_BITS - 1, -1, -1):
        cand = p * (2.0 ** -(2 ** k))
        p = jnp.where(count(a >= cand) < cap, cand, p)
    found = count(a >= 0.5 * p) >= cap
    lo = jnp.where(found, 0.5 * p, 0.0)

    def refine(_, carry):
        lo, step = carry
        cand = lo + step
        return jnp.where(count(a >= cand) >= cap, cand, lo), 0.5 * step

    thr, _ = lax.fori_loop(0, ROUTE_MANTISSA_STEPS, refine, (lo, 0.5 * lo))
    above = a > thr
    tie = a == thr
    tri = tri_ref[:n, :n]
    tie_f = jnp.where(tie, 1.0, 0.0)
    ties_before = _dot(tie_f.astype(BF16), tri) - tie_f
    keep = above | (tie & (ties_before < cap - count(above)))
    keep_f = jnp.where(keep, 1.0, 0.0)
    slot = _dot(keep_f.astype(BF16), tri) - keep_f
    rank_ref[...] = jnp.where(keep, slot, -1.0).astype(jnp.int32)


def _route_set(aff_t, tri, cap):
    return pl.pallas_call(
        functools.partial(_route_kernel, cap=cap),
        out_shape=jax.ShapeDtypeStruct(aff_t.shape, jnp.int32),
        compiler_params=pltpu.CompilerParams(vmem_limit_bytes=VMEM_LIMIT),
    )(aff_t, tri[:aff_t.shape[1], :aff_t.shape[1]])


def _gather_kernel(start_ref, nwin_ref, rank_ref, aff_ref, *rest):
    *ctx_refs, u_ref, o_ref, g_ref = rest
    b = pl.program_id(0)
    t = pl.program_id(1)

    @pl.when(t == 0)
    def _():
        o_ref[...] = jnp.zeros_like(o_ref)
        g_ref[...] = jnp.zeros_like(g_ref)

    def latent():
        rank = rank_ref[...]
        aff = aff_ref[...]
        u = u_ref[0]
        offset = lax.broadcasted_iota(jnp.int32, (SLOT_WIN, TM), 0)

        def one_pass(k, carry):
            picks, los = [], []
            for e in range(N_EXPERTS):
                w = start_ref[b, t, e] + k * SLOT_WIN
                lo = pl.multiple_of(jnp.minimum(w, CAP_LAT - SLOT_WIN), 16)
                r = rank[e:e + 1, :]
                picks.append((r - lo == offset) & (r >= w))
                los.append(lo)
            rows = _dot(jnp.concatenate([_onehot(p) for p in picks], axis=0), u)
            for e in range(N_EXPERTS):
                sl = pl.ds(los[e], SLOT_WIN)
                new = o_ref[e, sl, :].astype(F32) + rows[e * SLOT_WIN:(e + 1) * SLOT_WIN]
                o_ref[e, sl, :] = new.astype(o_ref.dtype)
                g_ref[e, sl, :] += jnp.sum(jnp.where(picks[e], aff[e:e + 1, :], 0.0), axis=-1, keepdims=True)
            return carry

        lax.fori_loop(0, nwin_ref[b, t], one_pass, 0)

    def context():
        rank = ctx_refs[0][...]
        aff = ctx_refs[1][...]
        slot = lax.broadcasted_iota(jnp.int32, (CAP_CTX, CTX_LEN), 0)
        picks = [slot == rank[e:e + 1, :] for e in range(N_EXPERTS)]
        rows = _dot(jnp.concatenate([_onehot(p) for p in picks], axis=0), u_ref[0])
        for e in range(N_EXPERTS):
            o_ref[e, CAP_LAT:CAP_ALL, :] = rows[e * CAP_CTX:(e + 1) * CAP_CTX].astype(o_ref.dtype)
            g_ref[e, CAP_LAT:CAP_ALL, :] = jnp.sum(jnp.where(picks[e], aff[e:e + 1, :], 0.0), axis=-1, keepdims=True)

    if ctx_refs:
        pl.when(t < N_LAT_TILES)(latent)
        pl.when(t == N_LAT_TILES)(context)
    else:
        latent()


def _gather_tokens(ranks, affs, u2, start, nwin):
    B = u2.shape[0]
    with_ctx = len(ranks) > 1
    cap = CAP_ALL if with_ctx else CAP_LAT
    lat_spec = pl.BlockSpec((N_EXPERTS, TM), lambda b, t, *_: (b, jnp.minimum(t, N_LAT_TILES - 1)))
    ctx_spec = pl.BlockSpec((N_EXPERTS, CTX_LEN), lambda b, t, *_: (b, 0))
    grid_spec = pltpu.PrefetchScalarGridSpec(
        num_scalar_prefetch=2,
        grid=(B, N_TILES if with_ctx else N_LAT_TILES),
        in_specs=[lat_spec, lat_spec] + ([ctx_spec, ctx_spec] if with_ctx else [])
        + [pl.BlockSpec((1, TM, D_MODEL), lambda b, t, *_: (b, t, 0))],
        out_specs=[
            pl.BlockSpec((N_EXPERTS, cap, D_MODEL), lambda b, t, *_: (0, b, 0)),
            pl.BlockSpec((N_EXPERTS, cap, 1), lambda b, t, *_: (0, b, 0)),
        ],
    )
    args = [ranks[0], affs[0]] + ([ranks[1], affs[1]] if with_ctx else [])
    return pl.pallas_call(
        _gather_kernel,
        grid_spec=grid_spec,
        out_shape=[
            jax.ShapeDtypeStruct((N_EXPERTS, B * cap, D_MODEL), BF16),
            jax.ShapeDtypeStruct((N_EXPERTS, B * cap, 1), F32),
        ],
        compiler_params=_cparams(("arbitrary", "arbitrary")),
    )(start, nwin, *args, u2)


def _expert_kernel(x_ref, g_ref, wg_ref, wu_ref, wd_ref, o_ref, acc_ref, *, chunk):
    f = pl.program_id(1)

    @pl.when(f == 0)
    def _():
        acc_ref[...] = jnp.zeros_like(acc_ref)

    wg = wg_ref[0, 0].astype(BF16)
    wu = wu_ref[0, 0].astype(BF16)
    wd = wd_ref[0, 0].astype(BF16)
    for c in range(x_ref.shape[1] // chunk):
        rows = pl.ds(c * chunk, chunk)
        x = x_ref[0, rows, :]
        a = _dot(x, wg)
        u = _dot(x, wu)
        acc_ref[rows, :] += _dot((a * jax.nn.sigmoid(a) * u).astype(BF16), wd)

    @pl.when(f == pl.num_programs(1) - 1)
    def _():
        o_ref[0] = (acc_ref[...] * g_ref[0]).astype(o_ref.dtype)


def _experts(xe, gcol, w_gate, w_up, w_down, layer):
    rows = xe.shape[1]
    chunk = next(c for c in (XROW_CHUNK, 512, CAP_ALL, CAP_LAT) if rows % c == 0)
    return pl.pallas_call(
        functools.partial(_expert_kernel, chunk=chunk),
        grid=(N_EXPERTS, EXPERT_FF // FF_TILE),
        in_specs=[
            pl.BlockSpec((1, rows, D_MODEL), lambda e, f: (e, 0, 0)),
            pl.BlockSpec((1, rows, 1), lambda e, f: (e, 0, 0)),
            pl.BlockSpec((1, 1, D_MODEL, FF_TILE), lambda e, f: (layer, e, 0, f)),
            pl.BlockSpec((1, 1, D_MODEL, FF_TILE), lambda e, f: (layer, e, 0, f)),
            pl.BlockSpec((1, 1, FF_TILE, D_MODEL), lambda e, f: (layer, e, f, 0)),
        ],
        out_specs=pl.BlockSpec((1, rows, D_MODEL), lambda e, f: (e, 0, 0)),
        out_shape=jax.ShapeDtypeStruct((N_EXPERTS, rows, D_MODEL), BF16),
        scratch_shapes=[pltpu.VMEM((rows, D_MODEL), F32)],
        compiler_params=_cparams(("arbitrary", "arbitrary")),
    )(xe, gcol, w_gate, w_up, w_down)


SLOT_WIN = 64


def _scatter_kernel(start_ref, nwin_ref, h_ref, mod_ref, y_ref, g_ref, b_ref, rank_ref, o_ref, fx_ref):
    b = pl.program_id(0)
    t = pl.program_id(1)
    rank = rank_ref[0]

    def spread_ranks(width):
        n = N_EXPERTS * width
        col = lax.broadcasted_iota(jnp.int32, (N_EXPERTS, n), 1)
        spread = _onehot(col // width == lax.broadcasted_iota(jnp.int32, (N_EXPERTS, n), 0))
        return _dot(rank.astype(F32).astype(BF16), spread)

    def finish(fx):
        g2 = mod_ref[0, 0, 5:6, :]
        o_ref[0] = _layer_norm(DN_ALPHA * h_ref[0] + g2 * fx, g_ref[...], b_ref[...])

    def latent():
        n = N_EXPERTS * SLOT_WIN
        rank_cols = spread_ranks(SLOT_WIN)
        lane = lax.broadcasted_iota(jnp.int32, (1, n), 1)
        block = lane // SLOT_WIN
        offset = (lane % SLOT_WIN).astype(F32)
        fx_ref[...] = jnp.zeros_like(fx_ref)

        def one_pass(k, carry):
            want = jnp.zeros((1, n), F32)
            have = jnp.zeros((1, n), F32)
            rows = []
            for e in range(N_EXPERTS):
                w = start_ref[b, t, e] + k * SLOT_WIN
                lo = pl.multiple_of(jnp.minimum(w, CAP_LAT - SLOT_WIN), 16)
                rows.append(y_ref[e, 0, pl.ds(lo, SLOT_WIN), :])
                want = jnp.where(block == e, w.astype(F32), want)
                have = jnp.where(block == e, lo.astype(F32), have)
            hit = (rank_cols - have == offset) & (rank_cols >= want)
            fx_ref[...] += _dot(_onehot(hit), jnp.concatenate(rows, axis=0))
            return carry

        lax.fori_loop(0, nwin_ref[b, t], one_pass, 0)
        finish(fx_ref[...])

    def context():
        n = N_EXPERTS * CAP_CTX
        slot = (lax.broadcasted_iota(jnp.int32, (TM, n), 1) % CAP_CTX).astype(F32)
        y = y_ref[:, 0, CAP_LAT:CAP_ALL, :].reshape(n, D_MODEL)
        finish(_dot(_onehot(spread_ranks(CAP_CTX) == slot), y))

    if y_ref.shape[2] == CAP_LAT:
        latent()
    else:
        pl.when(t < N_LAT_TILES)(latent)
        pl.when(t == N_LAT_TILES)(context)


def _slot_windows(rank_lat):
    B = rank_lat.shape[0] // N_EXPERTS
    cnt = jnp.sum((rank_lat >= 0).reshape(B, N_EXPERTS, N_LAT_TILES, TM), axis=-1, dtype=jnp.int32)
    first = jnp.cumsum(cnt, axis=-1) - cnt
    start = (first // 16) * 16
    nwin = jnp.max((first - start + cnt + SLOT_WIN - 1) // SLOT_WIN, axis=1)
    return start.transpose(0, 2, 1), nwin


def _scatter_ln(h1, mod_all, ye, ln_g, ln_b, rank_tok, start, nwin):
    B = h1.shape[0]
    cap = ye.shape[2]
    n_tiles = rank_tok.shape[1] // TM
    grid_spec = pltpu.PrefetchScalarGridSpec(
        num_scalar_prefetch=2,
        grid=(B, n_tiles),
        in_specs=[
            pl.BlockSpec((1, TM, D_MODEL), lambda b, t, *_: (b, t, 0)),
            pl.BlockSpec((1, 1, 6, D_MODEL), lambda b, t, *_: (b, t // N_LAT_TILES, 0, 0)),
            pl.BlockSpec((N_EXPERTS, 1, cap, D_MODEL), lambda b, t, *_: (0, b, 0, 0)),
            pl.BlockSpec((1, D_MODEL), lambda b, t, *_: (0, 0)),
            pl.BlockSpec((1, D_MODEL), lambda b, t, *_: (0, 0)),
            pl.BlockSpec((1, TM, N_EXPERTS), lambda b, t, *_: (b, t, 0)),
        ],
        out_specs=pl.BlockSpec((1, TM, D_MODEL), lambda b, t, *_: (b, t, 0)),
        scratch_shapes=[pltpu.VMEM((TM, D_MODEL), F32)],
    )
    return pl.pallas_call(
        _scatter_kernel,
        grid_spec=grid_spec,
        out_shape=jax.ShapeDtypeStruct((B, n_tiles * TM, D_MODEL), F32),
        compiler_params=_cparams(("arbitrary", "arbitrary")),
    )(start, nwin, h1, mod_all, ye, ln_g, ln_b, rank_tok)


def _rope_tables():
    n_freq = MLA_ROPE // 4
    inv = ROPE_BASE ** (-jnp.arange(n_freq, dtype=F32) / n_freq)
    t = jnp.arange(SEQ)
    row = (t // GRID_W).astype(F32)
    col = (t % GRID_W).astype(F32)
    ang = jnp.concatenate([row[:, None] * inv, col[:, None] * inv], axis=-1)
    cos, sin = jnp.cos(ang), jnp.sin(ang)
    ones = jnp.ones((SEQ, MLA_NOPE), F32)
    pad1 = jnp.ones((SEQ, MLA_HEAD_PAD - MLA_NOPE - MLA_ROPE), F32)
    cos_t = jnp.concatenate([ones, cos, cos, pad1], axis=-1)
    sin_t = jnp.concatenate([0 * ones, -sin, sin, 0 * pad1], axis=-1)
    cos_t = jnp.concatenate([cos_t, jnp.ones((CTX_LEN, MLA_HEAD_PAD), F32)], axis=0)
    sin_t = jnp.concatenate([sin_t, jnp.zeros((CTX_LEN, MLA_HEAD_PAD), F32)], axis=0)
    return cos_t, sin_t


def _dft_tables():
    def block(n):
        split = 64
        k = jnp.arange(n, dtype=jnp.int32)[None, :]
        jh = jnp.arange(n // split, dtype=jnp.int32)[:, None]
        jl = jnp.arange(split, dtype=jnp.int32)[:, None]
        ang_h = ((jh * split * k) % n).astype(F32) * (2.0 * np.pi / n)
        ang_l = ((jl * k) % n).astype(F32) * (2.0 * np.pi / n)
        ch, sh = jnp.cos(ang_h)[:, None, :], jnp.sin(ang_h)[:, None, :]
        cl_, sl_ = jnp.cos(ang_l)[None, :, :], jnp.sin(ang_l)[None, :, :]
        sc = 1.0 / np.sqrt(n * FFT_CH)
        return ((ch * cl_ - sh * sl_) * sc).reshape(n, n), (-(sh * cl_ + ch * sl_) * sc).reshape(n, n)

    cl, sl = block(SEQ)
    cc, sc_ = block(CTX_LEN)

    def diag(a, b):
        top = jnp.concatenate([a, jnp.zeros((SEQ, CTX_LEN), F32)], axis=1)
        bot = jnp.concatenate([jnp.zeros((CTX_LEN, SEQ), F32), b], axis=1)
        return jnp.concatenate([top, bot], axis=0).astype(BF16)

    j = jnp.arange(FFT_CH, dtype=jnp.int32)
    ang = ((j[:, None] * j[None, :]) % FFT_CH).astype(F32) * (2.0 * np.pi / FFT_CH)
    eye = jnp.eye(FFT_GROUPS, dtype=F32)
    bd = jnp.concatenate([jnp.kron(eye, jnp.cos(ang)), jnp.kron(eye, jnp.sin(ang))], axis=1).astype(BF16)
    return diag(cl, cc), diag(sl, sc_), bd


def _na_bias_index():
    R = SEQ // GRID_W
    pats = [(0, 0), (NA_QROWS, 0), (R - NA_QROWS, R - NA_KROWS)]
    q = np.arange(TM)
    k = np.arange(NA_KWIN)
    qi, qc = q // GRID_W, q % GRID_W
    kj, kc = k // GRID_W, k % GRID_W
    c0 = np.clip(qc - NA_WIN_COLS // 2, 0, GRID_W - NA_WIN_COLS)
    col_in = (kc[None, :] >= c0[:, None]) & (kc[None, :] < c0[:, None] + NA_WIN_COLS)
    col_off = np.clip(kc[None, :] - qc[:, None], -(NA_WIN_COLS - 1), NA_WIN_COLS - 1) + (NA_WIN_COLS - 1)
    del col_off
    row_offs, valids = [], []
    for rb, ks in pats:
        qr = rb + np.arange(NA_QROWS)
        kr = ks + np.arange(NA_KROWS)
        r0 = np.clip(qr - NA_WIN_ROWS // 2, 0, R - NA_WIN_ROWS)
        row_in = (kr[None, :] >= r0[:, None]) & (kr[None, :] < r0[:, None] + NA_WIN_ROWS)
        row_offs.append(np.where(row_in, kr[None, :] - qr[:, None] + (NA_WIN_ROWS - 1), -1))
        valids.append(row_in[qi][:, kj] & col_in)
    return np.stack(row_offs), np.stack(valids)


def _na_bias(rpb):
    row_off, valid = _na_bias_index()
    reach = GRID_W - NA_WIN_COLS
    padded = jnp.pad(rpb, ((0, 0), (0, 0), (0, 0), (reach, reach)))
    band = jnp.stack([padded[..., GRID_W - 1 - qc:2 * GRID_W - 1 - qc] for qc in range(GRID_W)], axis=-2)
    masked = jnp.zeros(band.shape[:2] + band.shape[3:], F32)
    pats = []
    for p in range(row_off.shape[0]):
        qrows = []
        for qi in range(NA_QROWS):
            blocks = [masked if row_off[p, qi, kj] < 0 else band[:, :, row_off[p, qi, kj]] for kj in range(NA_KROWS)]
            qrows.append(jnp.concatenate(blocks, axis=-1))
        pats.append(jnp.concatenate(qrows, axis=-2))
    b = jnp.stack(pats, axis=1)
    b = jnp.where(jnp.asarray(valid)[None, :, None], b, NEG)
    return jnp.concatenate([b, jnp.zeros(b.shape[:-1] + (CTX_LEN,), F32)], axis=-1)


def _rearranged_in_weights(w_in):
    z = lambda n: jnp.zeros((D_MODEL, n), F32)
    kr = w_in[:, 1664:1696]
    kr_e, kr_o = kr[:, 0::2], kr[:, 1::2]
    pad = MLA_HEAD_PAD - MLA_NOPE - MLA_ROPE
    krb = jnp.concatenate([z(MLA_NOPE), kr_e, kr_o, z(pad)], axis=1)
    krs = jnp.concatenate([z(MLA_NOPE), kr_o, kr_e, z(pad)], axis=1)
    wsm = jnp.concatenate([w_in[:, :1664], krb, krs], axis=1).astype(BF16)
    return wsm, w_in[:, 1696:].astype(BF16)


def _rearranged_mla_weights(w_uq, w_ukv):
    pad = MLA_HEAD_PAD - MLA_NOPE - MLA_ROPE
    wq = w_uq.reshape(MLA_Q_RANK, MLA_HEADS, MLA_NOPE + MLA_ROPE)
    nope, rope = wq[..., :MLA_NOPE], wq[..., MLA_NOPE:]
    r_e, r_o = rope[..., 0::2], rope[..., 1::2]
    zq = jnp.zeros((MLA_Q_RANK, MLA_HEADS, pad), F32)
    q_main = jnp.concatenate([nope, r_e, r_o, zq], axis=-1).reshape(MLA_Q_RANK, -1)
    q_swap = jnp.concatenate([nope, r_o, r_e, zq], axis=-1).reshape(MLA_Q_RANK, -1)
    wuq2 = jnp.concatenate([q_main, q_swap], axis=1).astype(BF16)
    wkv = w_ukv.reshape(MLA_KV_RANK, MLA_HEADS, MLA_NOPE + MLA_V)
    k_nope, v = wkv[..., :MLA_NOPE], wkv[..., MLA_NOPE:]
    zk = jnp.zeros((MLA_KV_RANK, MLA_HEADS, MLA_HEAD_PAD - MLA_NOPE), F32)
    k_main = jnp.concatenate([k_nope, zk], axis=-1).reshape(MLA_KV_RANK, -1)
    wukv2 = jnp.concatenate([k_main, v.reshape(MLA_KV_RANK, -1)], axis=1).astype(BF16)
    return wuq2, wukv2


def _route(aff, tri, with_ctx):
    B = aff.shape[0]
    a = aff.transpose(0, 2, 1).reshape(B * N_EXPERTS, -1)
    affs = [a[:, :SEQ]] + ([a[:, SEQ:]] if with_ctx else [])
    ranks = [_route_set(s, tri, cap) for s, cap in zip(affs, (CAP_LAT, CAP_CTX))]
    rank_tok = jnp.concatenate([r.reshape(B, N_EXPERTS, -1) for r in ranks], axis=-1).transpose(0, 2, 1)
    return affs, ranks, rank_tok


def kernel(x, c, ctx, c_ctx, w_mod, b_mod, w_in, na_rpb, pool_w, pool_scale, mla_q_norm, mla_w_uq, mla_kv_norm,
           mla_w_ukv, w_branch, w_out, ln1_g, ln1_b, w_router, w_gate, w_up, w_down, ln2_g, ln2_b):
    B = x.shape[0]
    assert x.shape == (B, SEQ, D_MODEL) and ctx.shape == (B, CTX_LEN, D_MODEL) and B + 1 <= 16

    cc = jnp.concatenate([c, c_ctx[None], jnp.zeros((16 - B - 1, D_MODEL), F32)], axis=0)
    mod = _modulation(cc, w_mod, b_mod).reshape(DEPTH, 16, 6, D_MODEL)
    cos_t, sin_t = _rope_tables()
    cmat, smat, bd = _dft_tables()
    na_bias = _na_bias(na_rpb)
    tri = jnp.triu(jnp.ones((SEQ, SEQ), BF16))
    hall = jnp.concatenate([x, ctx], axis=1)

    for l in range(DEPTH):
        with_ctx = l < DEPTH - 1
        n_tiles = N_TILES if with_ctx else N_LAT_TILES
        mod_all = jnp.stack([mod[l, :B], jnp.broadcast_to(mod[l, B], (B, 6, D_MODEL))], axis=1)
        wsm, wgates = _rearranged_in_weights(w_in[l])
        wuq2, wukv2 = _rearranged_mla_weights(mla_w_uq[l], mla_w_ukv[l])
        qa, ka, va, up, xcs, qm, km, vm = _inproj(
            hall, mod_all, wsm, mla_q_norm[l][None], mla_kv_norm[l][None], wuq2, wukv2, cos_t, sin_t, bd)
        ya = _na_attention(qa, ka, va, na_bias, l, n_tiles)
        pool_bd = jax.scipy.linalg.block_diag(*[pool_w[l, g] for g in range(POOL_GROUPS)]).astype(BF16)
        yb = _pool(up, pool_bd, pool_scale[l][None])
        yc = _dft_positions(cmat, smat, xcs)
        yd = _mla_attention(qm, km, vm, n_tiles)
        wr = jnp.pad(w_router[l], ((0, 0), (0, ROUTER_PAD - N_EXPERTS)))
        wr_hi = wr.astype(BF16)
        wr_cat = jnp.concatenate([wr_hi, (wr - wr_hi.astype(F32)).astype(BF16)], axis=1)
        h1, u2, aff = _merge(hall, mod_all, ya, yb, yc, yd, wgates, w_branch[l].astype(BF16), w_out[l].astype(BF16),
                             ln1_g[l][None], ln1_b[l][None], wr_cat, n_tiles)
        affs, ranks, rank_tok = _route(aff[:, :, :N_EXPERTS], tri, with_ctx)
        start, nwin = _slot_windows(ranks[0])
        xe, gcol = _gather_tokens(ranks, affs, u2, start, nwin)
        ye = _experts(xe, gcol, w_gate, w_up, w_down, l)
        hall = _scatter_ln(h1, mod_all, ye.reshape(N_EXPERTS, B, -1, D_MODEL), ln2_g[l][None], ln2_b[l][None], rank_tok,
                           start, nwin)
    return hall
```

```python
import functools

import numpy as np
import jax
import jax.numpy as jnp
from jax import lax
from jax.experimental import pallas as pl
from jax.experimental.pallas import tpu as pltpu

F32 = jnp.float32
BF16 = jnp.bfloat16

D_MODEL = 1024
DEPTH = 4
GRID_W = 64
SEQ = 2048
CTX_LEN = 256
T_ALL = SEQ + CTX_LEN
HEAD_DIM = 64
NA_HEADS = 4
NA_WIN_ROWS = 8
NA_WIN_COLS = 16
NA_SCALE = HEAD_DIM ** -0.5
POOL_GROUPS = 4
POOL_CH = 64
POOL_WINDOWS = (2, 4, 8, 16)
FFT_GROUPS = 4
FFT_CH = 64
MLA_HEADS = 4
MLA_Q_RANK = 256
MLA_KV_RANK = 128
MLA_NOPE = 64
MLA_ROPE = 32
MLA_V = 64
MLA_SCALE = (MLA_NOPE + MLA_ROPE) ** -0.5
ROPE_BASE = 10000.0
BRANCH_W = 256
N_BRANCH = 4
N_EXPERTS = 16
EXPERT_FF = 2048
EC_CAPACITY = 2
DN_ALPHA = (2 * DEPTH) ** 0.25
LN_EPS = 1e-5
NEG = -1e30

LANE = 128
TM = 256
N_TILES = T_ALL // TM
N_LAT_TILES = SEQ // TM
MLA_HEAD_PAD = 128
NA_QROWS = TM // GRID_W
NA_KROWS = 12
NA_KWIN = NA_KROWS * GRID_W
NA_KALL = NA_KWIN + CTX_LEN
CAP_LAT = EC_CAPACITY * SEQ // N_EXPERTS
CAP_CTX = EC_CAPACITY * CTX_LEN // N_EXPERTS
CAP_ALL = CAP_LAT + CAP_CTX
ROUTER_PAD = 128
ROUTE_EXP_BITS = 11
ROUTE_MANTISSA_STEPS = 52
FF_TILE = 512
XROW_CHUNK = 576
VMEM_LIMIT = 56 * 1024 * 1024

C_QA, C_KA, C_VA, C_UP, C_UF, C_CQ, C_CKV, C_KR, C_KRS, C_END = 0, 256, 512, 768, 1024, 1280, 1536, 1664, 1792, 1920


def _cparams(sem):
    return pltpu.CompilerParams(dimension_semantics=sem, vmem_limit_bytes=VMEM_LIMIT)


def _dot(a, b):
    return jnp.dot(a, b, preferred_element_type=F32)


def _dot_t(a, b):
    return lax.dot_general(a, b, (((1,), (1,)), ((), ())), preferred_element_type=F32)


def _layer_norm(x, g, b):
    mu = jnp.mean(x, axis=-1, keepdims=True)
    xc = x - mu
    var = jnp.mean(xc * xc, axis=-1, keepdims=True)
    return xc * lax.rsqrt(var + LN_EPS) * g + b


def _mod_kernel(c_ref, w_ref, b_ref, o_ref):
    c = c_ref[...]
    s = c * jax.nn.sigmoid(c)
    w = w_ref[0]
    s_hi = s.astype(BF16)
    s_lo = (s - s_hi.astype(F32)).astype(BF16)
    w_hi = w.astype(BF16)
    w_lo = (w - w_hi.astype(F32)).astype(BF16)
    o_ref[0] = _dot(s_hi, w_hi) + _dot(s_hi, w_lo) + _dot(s_lo, w_hi) + b_ref[0]


def _modulation(cc, w_mod, b_mod):
    rows = cc.shape[0]
    n6 = 6 * D_MODEL
    tn = 1536
    return pl.pallas_call(
        _mod_kernel,
        grid=(DEPTH, n6 // tn),
        in_specs=[
            pl.BlockSpec((rows, D_MODEL), lambda l, j: (0, 0)),
            pl.BlockSpec((1, D_MODEL, tn), lambda l, j: (l, 0, j)),
            pl.BlockSpec((1, 1, tn), lambda l, j: (l, 0, j)),
        ],
        out_specs=pl.BlockSpec((1, rows, tn), lambda l, j: (l, 0, j)),
        out_shape=jax.ShapeDtypeStruct((DEPTH, rows, n6), F32),
        compiler_params=_cparams(("arbitrary", "arbitrary")),
    )(cc, w_mod, b_mod.reshape(DEPTH, 1, n6))


def _inproj_kernel(h_ref, mod_ref, wsm_ref, qn_ref, kvn_ref, wuq_ref, wukv_ref, cos_ref, sin_ref, bd_ref,
                   qa_ref, ka_ref, va_ref, up_ref, xcs_ref, qm_ref, km_ref, vm_ref):
    pb = h_ref.shape[0]

    def put(ref, val):
        ref[...] = val.astype(ref.dtype).reshape(ref.shape)

    sh1 = mod_ref[:, 0, 0:1, :]
    sc1 = mod_ref[:, 0, 1:2, :]
    u = (h_ref[...] * (1.0 + sc1) + sh1).astype(BF16).reshape(pb * TM, D_MODEL)
    z = _dot(u, wsm_ref[...])
    put(qa_ref, z[:, C_QA:C_KA] * NA_SCALE)
    put(ka_ref, z[:, C_KA:C_VA])
    put(va_ref, z[:, C_VA:C_UP])
    put(up_ref, z[:, C_UP:C_UF])
    put(xcs_ref, _dot(z[:, C_UF:C_CQ].astype(BF16), bd_ref[...]))

    cos = jnp.concatenate([cos_ref[...]] * pb, axis=0)
    sin = jnp.concatenate([sin_ref[...]] * pb, axis=0)
    cos4 = jnp.concatenate([cos] * MLA_HEADS, axis=-1)
    sin4 = jnp.concatenate([sin] * MLA_HEADS, axis=-1)

    cq = z[:, C_CQ:C_CKV]
    nq = cq * lax.rsqrt(jnp.mean(cq * cq, axis=-1, keepdims=True) + LN_EPS) * qn_ref[...]
    q2 = _dot(nq.astype(BF16), wuq_ref[...])
    hw = MLA_HEADS * MLA_HEAD_PAD
    put(qm_ref, (q2[:, :hw] * cos4 + q2[:, hw:] * sin4) * MLA_SCALE)

    ckv = z[:, C_CKV:C_KR]
    nkv = ckv * lax.rsqrt(jnp.mean(ckv * ckv, axis=-1, keepdims=True) + LN_EPS) * kvn_ref[...]
    kv2 = _dot(nkv.astype(BF16), wukv_ref[...])
    kr = z[:, C_KR:C_KRS] * cos + z[:, C_KRS:C_END] * sin
    put(km_ref, kv2[:, :hw] + jnp.concatenate([kr] * MLA_HEADS, axis=-1))
    put(vm_ref, kv2[:, hw:])


def _samples_per_step(B):
    return 2 if B % 2 == 0 else 1


def _resident(shape):
    return pl.BlockSpec(shape, lambda *_: (0,) * len(shape), pipeline_mode=pl.Buffered(1))


def _inproj(hall, mod_all, wsm, qn, kvn, wuq2, wukv2, cos_t, sin_t, bd):
    B = hall.shape[0]
    pb = _samples_per_step(B)
    hw = MLA_HEADS * MLA_HEAD_PAD
    row = lambda b, t: (b, t, 0)
    outs = [
        (256, BF16), (256, BF16), (256, BF16), (256, F32), (512, BF16), (hw, BF16), (hw, BF16), (MLA_HEADS * MLA_V, BF16),
    ]
    return pl.pallas_call(
        _inproj_kernel,
        grid=(B // pb, N_TILES),
        in_specs=[
            pl.BlockSpec((pb, TM, D_MODEL), row),
            pl.BlockSpec((pb, 1, 6, D_MODEL), lambda b, t: (b, t // N_LAT_TILES, 0, 0)),
            _resident(wsm.shape),
            _resident(qn.shape),
            _resident(kvn.shape),
            _resident(wuq2.shape),
            _resident(wukv2.shape),
            pl.BlockSpec((TM, LANE), lambda b, t: (t, 0)),
            pl.BlockSpec((TM, LANE), lambda b, t: (t, 0)),
            _resident(bd.shape),
        ],
        out_specs=[pl.BlockSpec((pb, TM, w), row) for w, _ in outs],
        out_shape=[jax.ShapeDtypeStruct((B, T_ALL, w), dt) for w, dt in outs],
        compiler_params=_cparams(("arbitrary", "arbitrary")),
    )(hall, mod_all, wsm, qn, kvn, wuq2, wukv2, cos_t, sin_t, bd)


def _softmax_pv(s, v):
    m = jnp.max(s, axis=-1, keepdims=True)
    p = jnp.exp(s - m)
    l = jnp.sum(p, axis=-1, keepdims=True)
    return _dot(p.astype(BF16), v) / l


def _na_kernel(q_ref, k_ref, v_ref, bias_ref, o_ref):
    t = pl.program_id(0)

    @pl.when(t < N_LAT_TILES)
    def _():
        ks = pl.multiple_of(jnp.clip(t * NA_QROWS - NA_WIN_ROWS // 2, 0, SEQ // GRID_W - NA_KROWS) * GRID_W, GRID_W)
        kk = jnp.concatenate([k_ref[0, pl.ds(ks, NA_KWIN), :], k_ref[0, SEQ:T_ALL, :]], axis=0)
        vv = jnp.concatenate([v_ref[0, pl.ds(ks, NA_KWIN), :], v_ref[0, SEQ:T_ALL, :]], axis=0)
        q = q_ref[0]
        sls = [slice(h * HEAD_DIM, (h + 1) * HEAD_DIM) for h in range(NA_HEADS)]
        ss = [_dot_t(q[:, sl], kk[:, sl]) for sl in sls]
        ss = [jnp.concatenate([s[:, :NA_KWIN] + bias_ref[0, 0, h], s[:, NA_KWIN:]], axis=-1) for h, s in enumerate(ss)]
        ms = [jnp.max(s, axis=-1, keepdims=True) for s in ss]
        ps = [jnp.exp(s - m).astype(BF16) for s, m in zip(ss, ms)]
        ones = _onehot(lax.broadcasted_iota(jnp.int32, (NA_KALL, HEAD_DIM), 1) == 0)
        oa = [_dot(p, jnp.concatenate([vv[:, sl], ones], axis=-1)) for p, sl in zip(ps, sls)]
        os_ = [o[:, :HEAD_DIM] / o[:, HEAD_DIM:HEAD_DIM + 1] for o in oa]
        o_ref[0] = jnp.concatenate(os_, axis=-1).astype(o_ref.dtype)

    @pl.when(t == N_LAT_TILES)
    def _():
        kk = k_ref[0, SEQ:T_ALL, :]
        vv = v_ref[0, SEQ:T_ALL, :]
        q = q_ref[0]
        for h in range(NA_HEADS):
            sl = slice(h * HEAD_DIM, (h + 1) * HEAD_DIM)
            o_ref[0, :, sl] = _softmax_pv(_dot_t(q[:, sl], kk[:, sl]), vv[:, sl]).astype(o_ref.dtype)


def _na_attention(qa, ka, va, bias, layer, n_tiles):
    B = qa.shape[0]

    def bias_idx(t, b):
        return (layer, jnp.where(t == 0, 0, jnp.where(t >= N_LAT_TILES - 1, 2, 1)), 0, 0, 0)

    return pl.pallas_call(
        _na_kernel,
        grid=(n_tiles, B),
        in_specs=[
            pl.BlockSpec((1, TM, 256), lambda t, b: (b, t, 0)),
            pl.BlockSpec((1, T_ALL, 256), lambda t, b: (b, 0, 0)),
            pl.BlockSpec((1, T_ALL, 256), lambda t, b: (b, 0, 0)),
            pl.BlockSpec((1, 1, NA_HEADS, TM, NA_KWIN), bias_idx),
        ],
        out_specs=pl.BlockSpec((1, TM, 256), lambda t, b: (b, t, 0)),
        out_shape=jax.ShapeDtypeStruct((B, n_tiles * TM, 256), BF16),
        compiler_params=_cparams(("arbitrary", "arbitrary")),
    )(qa, ka, va, bias)


def _mla_kernel(q_ref, k_ref, v_ref, o_ref):
    t = pl.program_id(1)

    def run(key_lo, n_keys):
        keys = slice(key_lo, key_lo + n_keys)
        ksls = [slice(h * MLA_HEAD_PAD, (h + 1) * MLA_HEAD_PAD) for h in range(MLA_HEADS)]
        vsls = [slice(h * MLA_V, (h + 1) * MLA_V) for h in range(MLA_HEADS)]
        ss = [_dot_t(q_ref[0, :, ksl], k_ref[0, keys, ksl]) for ksl in ksls]
        ms = [jnp.max(s, axis=-1, keepdims=True) for s in ss]
        ps = [jnp.exp(s - m).astype(BF16) for s, m in zip(ss, ms)]
        ones = _onehot(lax.broadcasted_iota(jnp.int32, (n_keys, MLA_V), 1) == 0)
        oa = [_dot(p, jnp.concatenate([v_ref[0, keys, vsl], ones], axis=-1)) for p, vsl in zip(ps, vsls)]
        os_ = [o[:, :MLA_V] / o[:, MLA_V:MLA_V + 1] for o in oa]
        o_ref[0] = jnp.concatenate(os_, axis=-1).astype(o_ref.dtype)

    @pl.when(t < N_LAT_TILES)
    def _():
        run(0, T_ALL)

    @pl.when(t == N_LAT_TILES)
    def _():
        run(SEQ, CTX_LEN)


def _mla_attention(qm, km, vm, n_tiles):
    B = qm.shape[0]
    hw = MLA_HEADS * MLA_HEAD_PAD
    vw = MLA_HEADS * MLA_V
    return pl.pallas_call(
        _mla_kernel,
        grid=(B, n_tiles),
        in_specs=[
            pl.BlockSpec((1, TM, hw), lambda b, t: (b, t, 0)),
            pl.BlockSpec((1, T_ALL, hw), lambda b, t: (b, 0, 0)),
            pl.BlockSpec((1, T_ALL, vw), lambda b, t: (b, 0, 0)),
        ],
        out_specs=pl.BlockSpec((1, TM, vw), lambda b, t: (b, t, 0)),
        out_shape=jax.ShapeDtypeStruct((B, n_tiles * TM, vw), BF16),
        compiler_params=_cparams(("arbitrary", "arbitrary")),
    )(qm, km, vm)


POOL_PAD = POOL_WINDOWS[-1] // 2


def _pool_kernel(u_ref, w_ref, scale_ref, o_ref, pad_ref):
    width = POOL_GROUPS * POOL_CH
    lane = lax.broadcasted_iota(jnp.int32, (1, width), 1)
    half = jnp.left_shift(1, lane // POOL_CH)
    zeros = jnp.zeros((POOL_PAD, width), F32)
    base = 0
    for lo, n in ((0, SEQ), (SEQ, CTX_LEN)):
        x = u_ref[0, lo:lo + n, :]
        pad_ref[base:base + POOL_PAD, :] = zeros
        pad_ref[base + POOL_PAD:base + POOL_PAD + n, :] = x
        pad_ref[base + POOL_PAD + n:base + 2 * POOL_PAD + n, :] = zeros

        xp = pad_ref[base:base + n + 2 * POOL_PAD, :]

        def ahead(v, d):
            return jnp.concatenate([v[d:], jnp.zeros((d, width), F32)], axis=0)

        runs, run = {}, xp
        for h in (1, 2, 4, 8):
            run = run + ahead(run, h)
            runs[h] = run[POOL_PAD - h:POOL_PAD - h + n]
        win = jnp.where(half == 1, runs[1], jnp.where(half == 2, runs[2], jnp.where(half == 4, runs[4], runs[8])))
        t = lax.broadcasted_iota(jnp.int32, (n, 1), 0)
        cnt = jnp.minimum(t + half, n) - jnp.maximum(t - half, 0)
        y = (win / cnt.astype(F32) - x).astype(BF16)
        o_ref[0, lo:lo + n, :] = (_dot(y, w_ref[...]) * scale_ref[...]).astype(o_ref.dtype)
        base += n + 2 * POOL_PAD


def _pool(up, w_bd, scale):
    B = up.shape[0]
    return pl.pallas_call(
        _pool_kernel,
        grid=(B,),
        in_specs=[
            pl.BlockSpec((1, T_ALL, 256), lambda b: (b, 0, 0)),
            pl.BlockSpec((256, 256), lambda b: (0, 0)),
            pl.BlockSpec((1, 256), lambda b: (0, 0)),
        ],
        out_specs=pl.BlockSpec((1, T_ALL, 256), lambda b: (b, 0, 0)),
        out_shape=jax.ShapeDtypeStruct((B, T_ALL, 256), BF16),
        scratch_shapes=[pltpu.VMEM((T_ALL + 4 * POOL_PAD, POOL_GROUPS * POOL_CH), F32)],
        compiler_params=_cparams(("arbitrary",)),
    )(up, w_bd, scale)


DFT_ROWS = 768


def _dft_kernel(c_ref, s_ref, x_ref, o_ref):
    x = x_ref[0]
    w = FFT_GROUPS * FFT_CH
    o_ref[0] = (_dot(c_ref[...], x[:, :w]) + _dot(s_ref[...], x[:, w:])).astype(o_ref.dtype)


def _dft_positions(cmat, smat, xcs):
    B = xcs.shape[0]
    return pl.pallas_call(
        _dft_kernel,
        grid=(T_ALL // DFT_ROWS, B),
        in_specs=[
            pl.BlockSpec((DFT_ROWS, T_ALL), lambda i, b: (i, 0)),
            pl.BlockSpec((DFT_ROWS, T_ALL), lambda i, b: (i, 0)),
            pl.BlockSpec((1, T_ALL, 512), lambda i, b: (b, 0, 0)),
        ],
        out_specs=pl.BlockSpec((1, DFT_ROWS, 256), lambda i, b: (b, i, 0)),
        out_shape=jax.ShapeDtypeStruct((B, T_ALL, 256), BF16),
        compiler_params=_cparams(("arbitrary", "arbitrary")),
    )(cmat, smat, xcs)


def _merge_kernel(h_ref, mod_ref, ya_ref, yb_ref, yc_ref, yd_ref, wg_ref, wbr_ref, wout_ref, g_ref, b_ref, wr_ref,
                  h1_ref, u2_ref, aff_ref):
    pb = h_ref.shape[0]
    rows = pb * TM
    h = h_ref[...]
    sh1 = mod_ref[:, 0, 0:1, :]
    sc1 = mod_ref[:, 0, 1:2, :]
    g1 = mod_ref[:, 0, 2:3, :]
    sh2 = mod_ref[:, 0, 3:4, :]
    sc2 = mod_ref[:, 0, 4:5, :]
    u = (h * (1.0 + sc1) + sh1).astype(BF16).reshape(rows, D_MODEL)
    ys = (ya_ref, yb_ref, yc_ref, yd_ref)
    acc = jnp.zeros((rows, D_MODEL), F32)
    for n in range(N_BRANCH):
        gate = jax.nn.sigmoid(_dot(u, wg_ref[:, n * D_MODEL:(n + 1) * D_MODEL]))
        acc = acc + gate * _dot(ys[n][...].reshape(rows, BRANCH_W), wbr_ref[n])
    mix = _dot(acc.astype(BF16), wout_ref[...]).reshape(pb, TM, D_MODEL)
    h1 = _layer_norm(DN_ALPHA * h + g1 * mix, g_ref[...], b_ref[...])
    h1_ref[...] = h1
    u2 = h1 * (1.0 + sc2) + sh2
    u2_hi = u2.astype(BF16)
    u2_ref[...] = u2_hi
    u2_lo = (u2 - u2_hi.astype(F32)).astype(BF16).reshape(rows, D_MODEL)
    hi_terms = _dot(u2_hi.reshape(rows, D_MODEL), wr_ref[...])
    logits = hi_terms[:, :ROUTER_PAD] + hi_terms[:, ROUTER_PAD:] + _dot(u2_lo, wr_ref[:, :ROUTER_PAD])
    lane = lax.broadcasted_iota(jnp.int32, (1, ROUTER_PAD), 1)
    logits = jnp.where(lane < N_EXPERTS, logits, NEG)
    e = jnp.exp(logits - jnp.max(logits, axis=-1, keepdims=True))
    aff_ref[...] = (e / jnp.sum(e, axis=-1, keepdims=True)).reshape(pb, TM, ROUTER_PAD)


def _merge(hall, mod_all, ya, yb, yc, yd, wg, wbr, wout, ln_g, ln_b, wr_cat, n_tiles):
    B = hall.shape[0]
    pb = _samples_per_step(B)
    row = lambda b, t: (b, t, 0)
    t_rows = n_tiles * TM
    return pl.pallas_call(
        _merge_kernel,
        grid=(B // pb, n_tiles),
        in_specs=[
            pl.BlockSpec((pb, TM, D_MODEL), row),
            pl.BlockSpec((pb, 1, 6, D_MODEL), lambda b, t: (b, t // N_LAT_TILES, 0, 0)),
            pl.BlockSpec((pb, TM, BRANCH_W), row),
            pl.BlockSpec((pb, TM, BRANCH_W), row),
            pl.BlockSpec((pb, TM, BRANCH_W), row),
            pl.BlockSpec((pb, TM, BRANCH_W), row),
            _resident(wg.shape),
            _resident(wbr.shape),
            _resident(wout.shape),
            _resident((1, D_MODEL)),
            _resident((1, D_MODEL)),
            _resident(wr_cat.shape),
        ],
        out_specs=[
            pl.BlockSpec((pb, TM, D_MODEL), row),
            pl.BlockSpec((pb, TM, D_MODEL), row),
            pl.BlockSpec((pb, TM, ROUTER_PAD), row),
        ],
        out_shape=[
            jax.ShapeDtypeStruct((B, t_rows, D_MODEL), F32),
            jax.ShapeDtypeStruct((B, t_rows, D_MODEL), BF16),
            jax.ShapeDtypeStruct((B, t_rows, ROUTER_PAD), F32),
        ],
        compiler_params=_cparams(("arbitrary", "arbitrary")),
    )(hall, mod_all, ya, yb, yc, yd, wg, wbr, wout, ln_g, ln_b, wr_cat)


def _onehot(match):
    return jnp.where(match, 1.0, 0.0).astype(BF16)


def _route_kernel(a_ref, tri_ref, rank_ref, *, cap):
    a = a_ref[...]
    n = a.shape[1]

    def count(mask):
        return jnp.sum(jnp.where(mask, 1.0, 0.0), axis=-1, keepdims=True)

    p = jnp.full((a.shape[0], 1), 2.0, a.dtype)
    for k in range(ROUTE_EXP_BITS - 1, -1, -1):
        cand = p * (2.0 ** -(2 ** k))
        p = jnp.where(count(a >= cand) < cap, cand, p)
    found = count(a >= 0.5 * p) >= cap
    lo = jnp.where(found, 0.5 * p, 0.0)

    def refine(_, carry):
        lo, step = carry
        cand = lo + step
        return jnp.where(count(a >= cand) >= cap, cand, lo), 0.5 * step

    thr, _ = lax.fori_loop(0, ROUTE_MANTISSA_STEPS, refine, (lo, 0.5 * lo))
    above = a > thr
    tie = a == thr
    tri = tri_ref[:n, :n]
    tie_f = jnp.where(tie, 1.0, 0.0)
    ties_before = _dot(tie_f.astype(BF16), tri) - tie_f
    keep = above | (tie & (ties_before < cap - count(above)))
    keep_f = jnp.where(keep, 1.0, 0.0)
    slot = _dot(keep_f.astype(BF16), tri) - keep_f
    rank_ref[...] = jnp.where(keep, slot, -1.0).astype(jnp.int32)


def _route_set(aff_t, tri, cap):
    return pl.pallas_call(
        functools.partial(_route_kernel, cap=cap),
        out_shape=jax.ShapeDtypeStruct(aff_t.shape, jnp.int32),
        compiler_params=pltpu.CompilerParams(vmem_limit_bytes=VMEM_LIMIT),
    )(aff_t, tri[:aff_t.shape[1], :aff_t.shape[1]])


def _gather_kernel(start_ref, nwin_ref, rank_ref, aff_ref, *rest):
    *ctx_refs, u_ref, o_ref, g_ref = rest
    b = pl.program_id(0)
    t = pl.program_id(1)

    @pl.when(t == 0)
    def _():
        o_ref[...] = jnp.zeros_like(o_ref)
        g_ref[...] = jnp.zeros_like(g_ref)

    def latent():
        rank = rank_ref[...]
        aff = aff_ref[...]
        u = u_ref[0]
        offset = lax.broadcasted_iota(jnp.int32, (SLOT_WIN, TM), 0)

        def one_pass(k, carry):
            picks, los = [], []
            for e in range(N_EXPERTS):
                w = start_ref[b, t, e] + k * SLOT_WIN
                lo = pl.multiple_of(jnp.minimum(w, CAP_LAT - SLOT_WIN), 16)
                r = rank[e:e + 1, :]
                picks.append((r - lo == offset) & (r >= w))
                los.append(lo)
            rows = _dot(jnp.concatenate([_onehot(p) for p in picks], axis=0), u)
            for e in range(N_EXPERTS):
                sl = pl.ds(los[e], SLOT_WIN)
                new = o_ref[e, sl, :].astype(F32) + rows[e * SLOT_WIN:(e + 1) * SLOT_WIN]
                o_ref[e, sl, :] = new.astype(o_ref.dtype)
                g_ref[e, sl, :] += jnp.sum(jnp.where(picks[e], aff[e:e + 1, :], 0.0), axis=-1, keepdims=True)
            return carry

        lax.fori_loop(0, nwin_ref[b, t], one_pass, 0)

    def context():
        rank = ctx_refs[0][...]
        aff = ctx_refs[1][...]
        slot = lax.broadcasted_iota(jnp.int32, (CAP_CTX, CTX_LEN), 0)
        picks = [slot == rank[e:e + 1, :] for e in range(N_EXPERTS)]
        rows = _dot(jnp.concatenate([_onehot(p) for p in picks], axis=0), u_ref[0])
        for e in range(N_EXPERTS):
            o_ref[e, CAP_LAT:CAP_ALL, :] = rows[e * CAP_CTX:(e + 1) * CAP_CTX].astype(o_ref.dtype)
            g_ref[e, CAP_LAT:CAP_ALL, :] = jnp.sum(jnp.where(picks[e], aff[e:e + 1, :], 0.0), axis=-1, keepdims=True)

    if ctx_refs:
        pl.when(t < N_LAT_TILES)(latent)
        pl.when(t == N_LAT_TILES)(context)
    else:
        latent()


def _gather_tokens(ranks, affs, u2, start, nwin):
    B = u2.shape[0]
    with_ctx = len(ranks) > 1
    cap = CAP_ALL if with_ctx else CAP_LAT
    lat_spec = pl.BlockSpec((N_EXPERTS, TM), lambda b, t, *_: (b, jnp.minimum(t, N_LAT_TILES - 1)))
    ctx_spec = pl.BlockSpec((N_EXPERTS, CTX_LEN), lambda b, t, *_: (b, 0))
    grid_spec = pltpu.PrefetchScalarGridSpec(
        num_scalar_prefetch=2,
        grid=(B, N_TILES if with_ctx else N_LAT_TILES),
        in_specs=[lat_spec, lat_spec] + ([ctx_spec, ctx_spec] if with_ctx else [])
        + [pl.BlockSpec((1, TM, D_MODEL), lambda b, t, *_: (b, t, 0))],
        out_specs=[
            pl.BlockSpec((N_EXPERTS, cap, D_MODEL), lambda b, t, *_: (0, b, 0)),
            pl.BlockSpec((N_EXPERTS, cap, 1), lambda b, t, *_: (0, b, 0)),
        ],
    )
    args = [ranks[0], affs[0]] + ([ranks[1], affs[1]] if with_ctx else [])
    return pl.pallas_call(
        _gather_kernel,
        grid_spec=grid_spec,
        out_shape=[
            jax.ShapeDtypeStruct((N_EXPERTS, B * cap, D_MODEL), BF16),
            jax.ShapeDtypeStruct((N_EXPERTS, B * cap, 1), F32),
        ],
        compiler_params=_cparams(("arbitrary", "arbitrary")),
    )(start, nwin, *args, u2)


def _expert_kernel(x_ref, g_ref, wg_ref, wu_ref, wd_ref, o_ref, acc_ref, *, chunk):
    f = pl.program_id(1)

    @pl.when(f == 0)
    def _():
        acc_ref[...] = jnp.zeros_like(acc_ref)

    wg = wg_ref[0, 0].astype(BF16)
    wu = wu_ref[0, 0].astype(BF16)
    wd = wd_ref[0, 0].astype(BF16)
    for c in range(x_ref.shape[1] // chunk):
        rows = pl.ds(c * chunk, chunk)
        x = x_ref[0, rows, :]
        a = _dot(x, wg)
        u = _dot(x, wu)
        acc_ref[rows, :] += _dot((a * jax.nn.sigmoid(a) * u).astype(BF16), wd)

    @pl.when(f == pl.num_programs(1) - 1)
    def _():
        o_ref[0] = (acc_ref[...] * g_ref[0]).astype(o_ref.dtype)


def _experts(xe, gcol, w_gate, w_up, w_down, layer):
    rows = xe.shape[1]
    chunk = next(c for c in (XROW_CHUNK, 512, CAP_ALL, CAP_LAT) if rows % c == 0)
    return pl.pallas_call(
        functools.partial(_expert_kernel, chunk=chunk),
        grid=(N_EXPERTS, EXPERT_FF // FF_TILE),
        in_specs=[
            pl.BlockSpec((1, rows, D_MODEL), lambda e, f: (e, 0, 0)),
            pl.BlockSpec((1, rows, 1), lambda e, f: (e, 0, 0)),
            pl.BlockSpec((1, 1, D_MODEL, FF_TILE), lambda e, f: (layer, e, 0, f)),
            pl.BlockSpec((1, 1, D_MODEL, FF_TILE), lambda e, f: (layer, e, 0, f)),
            pl.BlockSpec((1, 1, FF_TILE, D_MODEL), lambda e, f: (layer, e, f, 0)),
        ],
        out_specs=pl.BlockSpec((1, rows, D_MODEL), lambda e, f: (e, 0, 0)),
        out_shape=jax.ShapeDtypeStruct((N_EXPERTS, rows, D_MODEL), BF16),
        scratch_shapes=[pltpu.VMEM((rows, D_MODEL), F32)],
        compiler_params=_cparams(("arbitrary", "arbitrary")),
    )(xe, gcol, w_gate, w_up, w_down)


SLOT_WIN = 64


def _scatter_kernel(start_ref, nwin_ref, h_ref, mod_ref, y_ref, g_ref, b_ref, rank_ref, o_ref, fx_ref):
    b = pl.program_id(0)
    t = pl.program_id(1)
    rank = rank_ref[0]

    def spread_ranks(width):
        n = N_EXPERTS * width
        col = lax.broadcasted_iota(jnp.int32, (N_EXPERTS, n), 1)
        spread = _onehot(col // width == lax.broadcasted_iota(jnp.int32, (N_EXPERTS, n), 0))
        return _dot(rank.astype(F32).astype(BF16), spread)

    def finish(fx):
        g2 = mod_ref[0, 0, 5:6, :]
        o_ref[0] = _layer_norm(DN_ALPHA * h_ref[0] + g2 * fx, g_ref[...], b_ref[...])

    def latent():
        n = N_EXPERTS * SLOT_WIN
        rank_cols = spread_ranks(SLOT_WIN)
        lane = lax.broadcasted_iota(jnp.int32, (1, n), 1)
        block = lane // SLOT_WIN
        offset = (lane % SLOT_WIN).astype(F32)
        fx_ref[...] = jnp.zeros_like(fx_ref)

        def one_pass(k, carry):
            want = jnp.zeros((1, n), F32)
            have = jnp.zeros((1, n), F32)
            rows = []
            for e in range(N_EXPERTS):
                w = start_ref[b, t, e] + k * SLOT_WIN
                lo = pl.multiple_of(jnp.minimum(w, CAP_LAT - SLOT_WIN), 16)
                rows.append(y_ref[e, 0, pl.ds(lo, SLOT_WIN), :])
                want = jnp.where(block == e, w.astype(F32), want)
                have = jnp.where(block == e, lo.astype(F32), have)
            hit = (rank_cols - have == offset) & (rank_cols >= want)
            fx_ref[...] += _dot(_onehot(hit), jnp.concatenate(rows, axis=0))
            return carry

        lax.fori_loop(0, nwin_ref[b, t], one_pass, 0)
        finish(fx_ref[...])

    def context():
        n = N_EXPERTS * CAP_CTX
        slot = (lax.broadcasted_iota(jnp.int32, (TM, n), 1) % CAP_CTX).astype(F32)
        y = y_ref[:, 0, CAP_LAT:CAP_ALL, :].reshape(n, D_MODEL)
        finish(_dot(_onehot(spread_ranks(CAP_CTX) == slot), y))

    if y_ref.shape[2] == CAP_LAT:
        latent()
    else:
        pl.when(t < N_LAT_TILES)(latent)
        pl.when(t == N_LAT_TILES)(context)


def _slot_windows(rank_lat):
    B = rank_lat.shape[0] // N_EXPERTS
    cnt = jnp.sum((rank_lat >= 0).reshape(B, N_EXPERTS, N_LAT_TILES, TM), axis=-1, dtype=jnp.int32)
    first = jnp.cumsum(cnt, axis=-1) - cnt
    start = (first // 16) * 16
    nwin = jnp.max((first - start + cnt + SLOT_WIN - 1) // SLOT_WIN, axis=1)
    return start.transpose(0, 2, 1), nwin


def _scatter_ln(h1, mod_all, ye, ln_g, ln_b, rank_tok, start, nwin):
    B = h1.shape[0]
    cap = ye.shape[2]
    n_tiles = rank_tok.shape[1] // TM
    grid_spec = pltpu.PrefetchScalarGridSpec(
        num_scalar_prefetch=2,
        grid=(B, n_tiles),
        in_specs=[
            pl.BlockSpec((1, TM, D_MODEL), lambda b, t, *_: (b, t, 0)),
            pl.BlockSpec((1, 1, 6, D_MODEL), lambda b, t, *_: (b, t // N_LAT_TILES, 0, 0)),
            pl.BlockSpec((N_EXPERTS, 1, cap, D_MODEL), lambda b, t, *_: (0, b, 0, 0)),
            pl.BlockSpec((1, D_MODEL), lambda b, t, *_: (0, 0)),
            pl.BlockSpec((1, D_MODEL), lambda b, t, *_: (0, 0)),
            pl.BlockSpec((1, TM, N_EXPERTS), lambda b, t, *_: (b, t, 0)),
        ],
        out_specs=pl.BlockSpec((1, TM, D_MODEL), lambda b, t, *_: (b, t, 0)),
        scratch_shapes=[pltpu.VMEM((TM, D_MODEL), F32)],
    )
    return pl.pallas_call(
        _scatter_kernel,
        grid_spec=grid_spec,
        out_shape=jax.ShapeDtypeStruct((B, n_tiles * TM, D_MODEL), F32),
        compiler_params=_cparams(("arbitrary", "arbitrary")),
    )(start, nwin, h1, mod_all, ye, ln_g, ln_b, rank_tok)


def _rope_tables():
    n_freq = MLA_ROPE // 4
    inv = ROPE_BASE ** (-jnp.arange(n_freq, dtype=F32) / n_freq)
    t = jnp.arange(SEQ)
    row = (t // GRID_W).astype(F32)
    col = (t % GRID_W).astype(F32)
    ang = jnp.concatenate([row[:, None] * inv, col[:, None] * inv], axis=-1)
    cos, sin = jnp.cos(ang), jnp.sin(ang)
    ones = jnp.ones((SEQ, MLA_NOPE), F32)
    pad1 = jnp.ones((SEQ, MLA_HEAD_PAD - MLA_NOPE - MLA_ROPE), F32)
    cos_t = jnp.concatenate([ones, cos, cos, pad1], axis=-1)
    sin_t = jnp.concatenate([0 * ones, -sin, sin, 0 * pad1], axis=-1)
    cos_t = jnp.concatenate([cos_t, jnp.ones((CTX_LEN, MLA_HEAD_PAD), F32)], axis=0)
    sin_t = jnp.concatenate([sin_t, jnp.zeros((CTX_LEN, MLA_HEAD_PAD), F32)], axis=0)
    return cos_t, sin_t


def _dft_tables():
    def block(n):
        split = 64
        k = jnp.arange(n, dtype=jnp.int32)[None, :]
        jh = jnp.arange(n // split, dtype=jnp.int32)[:, None]
        jl = jnp.arange(split, dtype=jnp.int32)[:, None]
        ang_h = ((jh * split * k) % n).astype(F32) * (2.0 * np.pi / n)
        ang_l = ((jl * k) % n).astype(F32) * (2.0 * np.pi / n)
        ch, sh = jnp.cos(ang_h)[:, None, :], jnp.sin(ang_h)[:, None, :]
        cl_, sl_ = jnp.cos(ang_l)[None, :, :], jnp.sin(ang_l)[None, :, :]
        sc = 1.0 / np.sqrt(n * FFT_CH)
        return ((ch * cl_ - sh * sl_) * sc).reshape(n, n), (-(sh * cl_ + ch * sl_) * sc).reshape(n, n)

    cl, sl = block(SEQ)
    cc, sc_ = block(CTX_LEN)

    def diag(a, b):
        top = jnp.concatenate([a, jnp.zeros((SEQ, CTX_LEN), F32)], axis=1)
        bot = jnp.concatenate([jnp.zeros((CTX_LEN, SEQ), F32), b], axis=1)
        return jnp.concatenate([top, bot], axis=0).astype(BF16)

    j = jnp.arange(FFT_CH, dtype=jnp.int32)
    ang = ((j[:, None] * j[None, :]) % FFT_CH).astype(F32) * (2.0 * np.pi / FFT_CH)
    eye = jnp.eye(FFT_GROUPS, dtype=F32)
    bd = jnp.concatenate([jnp.kron(eye, jnp.cos(ang)), jnp.kron(eye, jnp.sin(ang))], axis=1).astype(BF16)
    return diag(cl, cc), diag(sl, sc_), bd


def _na_bias_index():
    R = SEQ // GRID_W
    pats = [(0, 0), (NA_QROWS, 0), (R - NA_QROWS, R - NA_KROWS)]
    q = np.arange(TM)
    k = np.arange(NA_KWIN)
    qi, qc = q // GRID_W, q % GRID_W
    kj, kc = k // GRID_W, k % GRID_W
    c0 = np.clip(qc - NA_WIN_COLS // 2, 0, GRID_W - NA_WIN_COLS)
    col_in = (kc[None, :] >= c0[:, None]) & (kc[None, :] < c0[:, None] + NA_WIN_COLS)
    col_off = np.clip(kc[None, :] - qc[:, None], -(NA_WIN_COLS - 1), NA_WIN_COLS - 1) + (NA_WIN_COLS - 1)
    del col_off
    row_offs, valids = [], []
    for rb, ks in pats:
        qr = rb + np.arange(NA_QROWS)
        kr = ks + np.arange(NA_KROWS)
        r0 = np.clip(qr - NA_WIN_ROWS // 2, 0, R - NA_WIN_ROWS)
        row_in = (kr[None, :] >= r0[:, None]) & (kr[None, :] < r0[:, None] + NA_WIN_ROWS)
        row_offs.append(np.where(row_in, kr[None, :] - qr[:, None] + (NA_WIN_ROWS - 1), -1))
        valids.append(row_in[qi][:, kj] & col_in)
    return np.stack(row_offs), np.stack(valids)


def _na_bias(rpb):
    row_off, valid = _na_bias_index()
    reach = GRID_W - NA_WIN_COLS
    padded = jnp.pad(rpb, ((0, 0), (0, 0), (0, 0), (reach, reach)))
    band = jnp.stack([padded[..., GRID_W - 1 - qc:2 * GRID_W - 1 - qc] for qc in range(GRID_W)], axis=-2)
    masked = jnp.zeros(band.shape[:2] + band.shape[3:], F32)
    pats = []
    for p in range(row_off.shape[0]):
        qrows = []
        for qi in range(NA_QROWS):
            blocks = [masked if row_off[p, qi, kj] < 0 else band[:, :, row_off[p, qi, kj]] for kj in range(NA_KROWS)]
            qrows.append(jnp.concatenate(blocks, axis=-1))
        pats.append(jnp.concatenate(qrows, axis=-2))
    b = jnp.stack(pats, axis=1)
    return jnp.where(jnp.asarray(valid)[None, :, None], b, NEG)


def _rearranged_in_weights(w_in):
    z = lambda n: jnp.zeros((D_MODEL, n), F32)
    kr = w_in[:, 1664:1696]
    kr_e, kr_o = kr[:, 0::2], kr[:, 1::2]
    pad = MLA_HEAD_PAD - MLA_NOPE - MLA_ROPE
    krb = jnp.concatenate([z(MLA_NOPE), kr_e, kr_o, z(pad)], axis=1)
    krs = jnp.concatenate([z(MLA_NOPE), kr_o, kr_e, z(pad)], axis=1)
    wsm = jnp.concatenate([w_in[:, :1664], krb, krs], axis=1).astype(BF16)
    return wsm, w_in[:, 1696:].astype(BF16)


def _rearranged_mla_weights(w_uq, w_ukv):
    pad = MLA_HEAD_PAD - MLA_NOPE - MLA_ROPE
    wq = w_uq.reshape(MLA_Q_RANK, MLA_HEADS, MLA_NOPE + MLA_ROPE)
    nope, rope = wq[..., :MLA_NOPE], wq[..., MLA_NOPE:]
    r_e, r_o = rope[..., 0::2], rope[..., 1::2]
    zq = jnp.zeros((MLA_Q_RANK, MLA_HEADS, pad), F32)
    q_main = jnp.concatenate([nope, r_e, r_o, zq], axis=-1).reshape(MLA_Q_RANK, -1)
    q_swap = jnp.concatenate([nope, r_o, r_e, zq], axis=-1).reshape(MLA_Q_RANK, -1)
    wuq2 = jnp.concatenate([q_main, q_swap], axis=1).astype(BF16)
    wkv = w_ukv.reshape(MLA_KV_RANK, MLA_HEADS, MLA_NOPE + MLA_V)
    k_nope, v = wkv[..., :MLA_NOPE], wkv[..., MLA_NOPE:]
    zk = jnp.zeros((MLA_KV_RANK, MLA_HEADS, MLA_HEAD_PAD - MLA_NOPE), F32)
    k_main = jnp.concatenate([k_nope, zk], axis=-1).reshape(MLA_KV_RANK, -1)
    wukv2 = jnp.concatenate([k_main, v.reshape(MLA_KV_RANK, -1)], axis=1).astype(BF16)
    return wuq2, wukv2


def _route(aff, tri, with_ctx):
    B = aff.shape[0]
    a = aff.transpose(0, 2, 1).reshape(B * N_EXPERTS, -1)
    affs = [a[:, :SEQ]] + ([a[:, SEQ:]] if with_ctx else [])
    ranks = [_route_set(s, tri, cap) for s, cap in zip(affs, (CAP_LAT, CAP_CTX))]
    rank_tok = jnp.concatenate([r.reshape(B, N_EXPERTS, -1) for r in ranks], axis=-1).transpose(0, 2, 1)
    return affs, ranks, rank_tok


def kernel(x, c, ctx, c_ctx, w_mod, b_mod, w_in, na_rpb, pool_w, pool_scale, mla_q_norm, mla_w_uq, mla_kv_norm,
           mla_w_ukv, w_branch, w_out, ln1_g, ln1_b, w_router, w_gate, w_up, w_down, ln2_g, ln2_b):
    B = x.shape[0]
    assert x.shape == (B, SEQ, D_MODEL) and ctx.shape == (B, CTX_LEN, D_MODEL) and B + 1 <= 16

    cc = jnp.concatenate([c, c_ctx[None], jnp.zeros((16 - B - 1, D_MODEL), F32)], axis=0)
    mod = _modulation(cc, w_mod, b_mod).reshape(DEPTH, 16, 6, D_MODEL)
    cos_t, sin_t = _rope_tables()
    cmat, smat, bd = _dft_tables()
    na_bias = _na_bias(na_rpb)
    tri = jnp.triu(jnp.ones((SEQ, SEQ), BF16))
    hall = jnp.concatenate([x, ctx], axis=1)

    for l in range(DEPTH):
        with_ctx = l < DEPTH - 1
        n_tiles = N_TILES if with_ctx else N_LAT_TILES
        mod_all = jnp.stack([mod[l, :B], jnp.broadcast_to(mod[l, B], (B, 6, D_MODEL))], axis=1)
        wsm, wgates = _rearranged_in_weights(w_in[l])
        wuq2, wukv2 = _rearranged_mla_weights(mla_w_uq[l], mla_w_ukv[l])
        qa, ka, va, up, xcs, qm, km, vm = _inproj(
            hall, mod_all, wsm, mla_q_norm[l][None], mla_kv_norm[l][None], wuq2, wukv2, cos_t, sin_t, bd)
        ya = _na_attention(qa, ka, va, na_bias, l, n_tiles)
        pool_bd = jax.scipy.linalg.block_diag(*[pool_w[l, g] for g in range(POOL_GROUPS)]).astype(BF16)
        yb = _pool(up, pool_bd, pool_scale[l][None])
        yc = _dft_positions(cmat, smat, xcs)
        yd = _mla_attention(qm, km, vm, n_tiles)
        wr = jnp.pad(w_router[l], ((0, 0), (0, ROUTER_PAD - N_EXPERTS)))
        wr_hi = wr.astype(BF16)
        wr_cat = jnp.concatenate([wr_hi, (wr - wr_hi.astype(F32)).astype(BF16)], axis=1)
        h1, u2, aff = _merge(hall, mod_all, ya, yb, yc, yd, wgates, w_branch[l].astype(BF16), w_out[l].astype(BF16),
                             ln1_g[l][None], ln1_b[l][None], wr_cat, n_tiles)
        affs, ranks, rank_tok = _route(aff[:, :, :N_EXPERTS], tri, with_ctx)
        start, nwin = _slot_windows(ranks[0])
        xe, gcol = _gather_tokens(ranks, affs, u2, start, nwin)
        ye = _experts(xe, gcol, w_gate, w_up, w_down, l)
        hall = _scatter_ln(h1, mod_all, ye.reshape(N_EXPERTS, B, -1, D_MODEL), ln2_g[l][None], ln2_b[l][None], rank_tok,
                           start, nwin)
    return hall
```

```python
import functools

import numpy as np
import jax
import jax.numpy as jnp
from jax import lax
from jax.experimental import pallas as pl
from jax.experimental.pallas import tpu as pltpu

F32 = jnp.float32
BF16 = jnp.bfloat16

D_MODEL = 1024
DEPTH = 4
GRID_W = 64
SEQ = 2048
CTX_LEN = 256
T_ALL = SEQ + CTX_LEN
HEAD_DIM = 64
NA_HEADS = 4
NA_WIN_ROWS = 8
NA_WIN_COLS = 16
NA_SCALE = HEAD_DIM ** -0.5
POOL_GROUPS = 4
POOL_CH = 64
POOL_WINDOWS = (2, 4, 8, 16)
FFT_GROUPS = 4
FFT_CH = 64
MLA_HEADS = 4
MLA_Q_RANK = 256
MLA_KV_RANK = 128
MLA_NOPE = 64
MLA_ROPE = 32
MLA_V = 64
MLA_SCALE = (MLA_NOPE + MLA_ROPE) ** -0.5
ROPE_BASE = 10000.0
BRANCH_W = 256
N_BRANCH = 4
N_EXPERTS = 16
EXPERT_FF = 2048
EC_CAPACITY = 2
DN_ALPHA = (2 * DEPTH) ** 0.25
LN_EPS = 1e-5
NEG = -1e30

LANE = 128
TM = 256
N_TILES = T_ALL // TM
N_LAT_TILES = SEQ // TM
MLA_HEAD_PAD = 128
NA_QROWS = TM // GRID_W
NA_KROWS = 12
NA_KWIN = NA_KROWS * GRID_W
NA_KALL = NA_KWIN + CTX_LEN
CAP_LAT = EC_CAPACITY * SEQ // N_EXPERTS
CAP_CTX = EC_CAPACITY * CTX_LEN // N_EXPERTS
CAP_ALL = CAP_LAT + CAP_CTX
ROUTER_PAD = 128
ROUTE_EXP_BITS = 11
ROUTE_MANTISSA_STEPS = 52
FF_TILE = 512
XROW_CHUNK = 576
VMEM_LIMIT = 56 * 1024 * 1024

C_QA, C_KA, C_VA, C_UP, C_UF, C_CQ, C_CKV, C_KR, C_KRS, C_END = 0, 256, 512, 768, 1024, 1280, 1536, 1664, 1792, 1920


def _cparams(sem):
    return pltpu.CompilerParams(dimension_semantics=sem, vmem_limit_bytes=VMEM_LIMIT)


def _dot(a, b):
    return jnp.dot(a, b, preferred_element_type=F32)


def _dot_t(a, b):
    return lax.dot_general(a, b, (((1,), (1,)), ((), ())), preferred_element_type=F32)


def _layer_norm(x, g, b):
    mu = jnp.mean(x, axis=-1, keepdims=True)
    xc = x - mu
    var = jnp.mean(xc * xc, axis=-1, keepdims=True)
    return xc * lax.rsqrt(var + LN_EPS) * g + b


def _mod_kernel(c_ref, w_ref, b_ref, o_ref):
    c = c_ref[...]
    s = c * jax.nn.sigmoid(c)
    w = w_ref[0]
    s_hi = s.astype(BF16)
    s_lo = (s - s_hi.astype(F32)).astype(BF16)
    w_hi = w.astype(BF16)
    w_lo = (w - w_hi.astype(F32)).astype(BF16)
    o_ref[0] = _dot(s_hi, w_hi) + _dot(s_hi, w_lo) + _dot(s_lo, w_hi) + b_ref[0]


def _modulation(cc, w_mod, b_mod):
    rows = cc.shape[0]
    n6 = 6 * D_MODEL
    tn = 1536
    return pl.pallas_call(
        _mod_kernel,
        grid=(DEPTH, n6 // tn),
        in_specs=[
            pl.BlockSpec((rows, D_MODEL), lambda l, j: (0, 0)),
            pl.BlockSpec((1, D_MODEL, tn), lambda l, j: (l, 0, j)),
            pl.BlockSpec((1, 1, tn), lambda l, j: (l, 0, j)),
        ],
        out_specs=pl.BlockSpec((1, rows, tn), lambda l, j: (l, 0, j)),
        out_shape=jax.ShapeDtypeStruct((DEPTH, rows, n6), F32),
        compiler_params=_cparams(("arbitrary", "arbitrary")),
    )(cc, w_mod, b_mod.reshape(DEPTH, 1, n6))


def _stream_specs(stream, pb):
    if not isinstance(stream, tuple):
        return [pl.BlockSpec((pb, TM, D_MODEL), lambda b, t: (b, t, 0))], [stream]
    return [pl.BlockSpec((pb, TM, D_MODEL), lambda b, t: (b, jnp.minimum(t, N_LAT_TILES - 1), 0)),
            pl.BlockSpec((pb, TM, D_MODEL), lambda b, t: (b, 0, 0))], list(stream)


def _load_stream(h_refs):
    if len(h_refs) == 1:
        return h_refs[0][...]
    return jnp.where(pl.program_id(1) < N_LAT_TILES, h_refs[0][...], h_refs[1][...])


def _inproj_kernel(*refs, n_stream):
    h_refs = refs[:n_stream]
    (mod_ref, wsm_ref, qn_ref, kvn_ref, wuq_ref, wukv_ref, cos_ref, sin_ref, bd_ref,
     qa_ref, ka_ref, va_ref, up_ref, xcs_ref, qm_ref, km_ref, vm_ref) = refs[n_stream:]
    pb = mod_ref.shape[0]

    def put(ref, val):
        ref[...] = val.astype(ref.dtype).reshape(ref.shape)

    sh1 = mod_ref[:, 0, 0:1, :]
    sc1 = mod_ref[:, 0, 1:2, :]
    u = (_load_stream(h_refs) * (1.0 + sc1) + sh1).astype(BF16).reshape(pb * TM, D_MODEL)
    z = _dot(u, wsm_ref[...])
    put(qa_ref, z[:, C_QA:C_KA] * NA_SCALE)
    put(ka_ref, z[:, C_KA:C_VA])
    put(va_ref, z[:, C_VA:C_UP])
    put(up_ref, z[:, C_UP:C_UF])
    put(xcs_ref, _dot(z[:, C_UF:C_CQ].astype(BF16), bd_ref[...]))

    cos = jnp.concatenate([cos_ref[...]] * pb, axis=0)
    sin = jnp.concatenate([sin_ref[...]] * pb, axis=0)
    cos4 = jnp.concatenate([cos] * MLA_HEADS, axis=-1)
    sin4 = jnp.concatenate([sin] * MLA_HEADS, axis=-1)

    cq = z[:, C_CQ:C_CKV]
    nq = cq * lax.rsqrt(jnp.mean(cq * cq, axis=-1, keepdims=True) + LN_EPS) * qn_ref[...]
    q2 = _dot(nq.astype(BF16), wuq_ref[...])
    hw = MLA_HEADS * MLA_HEAD_PAD
    put(qm_ref, (q2[:, :hw] * cos4 + q2[:, hw:] * sin4) * MLA_SCALE)

    ckv = z[:, C_CKV:C_KR]
    nkv = ckv * lax.rsqrt(jnp.mean(ckv * ckv, axis=-1, keepdims=True) + LN_EPS) * kvn_ref[...]
    kv2 = _dot(nkv.astype(BF16), wukv_ref[...])
    kr = z[:, C_KR:C_KRS] * cos + z[:, C_KRS:C_END] * sin
    put(km_ref, kv2[:, :hw] + jnp.concatenate([kr] * MLA_HEADS, axis=-1))
    put(vm_ref, kv2[:, hw:])


def _samples_per_step(B):
    return 2 if B % 2 == 0 else 1


def _resident(shape):
    return pl.BlockSpec(shape, lambda *_: (0,) * len(shape), pipeline_mode=pl.Buffered(1))


def _inproj(hall, mod_all, wsm, qn, kvn, wuq2, wukv2, cos_t, sin_t, bd):
    B = mod_all.shape[0]
    pb = _samples_per_step(B)
    hw = MLA_HEADS * MLA_HEAD_PAD
    row = lambda b, t: (b, t, 0)
    outs = [
        (256, BF16), (256, BF16), (256, BF16), (256, F32), (512, BF16), (hw, BF16), (hw, BF16), (MLA_HEADS * MLA_V, BF16),
    ]
    stream_specs, stream_args = _stream_specs(hall, pb)
    return pl.pallas_call(
        functools.partial(_inproj_kernel, n_stream=len(stream_args)),
        grid=(B // pb, N_TILES),
        in_specs=stream_specs + [
            pl.BlockSpec((pb, 1, 6, D_MODEL), lambda b, t: (b, t // N_LAT_TILES, 0, 0)),
            _resident(wsm.shape),
            _resident(qn.shape),
            _resident(kvn.shape),
            _resident(wuq2.shape),
            _resident(wukv2.shape),
            pl.BlockSpec((TM, LANE), lambda b, t: (t, 0)),
            pl.BlockSpec((TM, LANE), lambda b, t: (t, 0)),
            _resident(bd.shape),
        ],
        out_specs=[pl.BlockSpec((pb, TM, w), row) for w, _ in outs],
        out_shape=[jax.ShapeDtypeStruct((B, T_ALL, w), dt) for w, dt in outs],
        compiler_params=_cparams(("arbitrary", "arbitrary")),
    )(*stream_args, mod_all, wsm, qn, kvn, wuq2, wukv2, cos_t, sin_t, bd)


def _softmax_pv(s, v):
    m = jnp.max(s, axis=-1, keepdims=True)
    p = jnp.exp(s - m)
    l = jnp.sum(p, axis=-1, keepdims=True)
    return _dot(p.astype(BF16), v) / l


def _na_kernel(q_ref, k_ref, v_ref, bias_ref, o_ref):
    t = pl.program_id(0)

    @pl.when(t < N_LAT_TILES)
    def _():
        ks = pl.multiple_of(jnp.clip(t * NA_QROWS - NA_WIN_ROWS // 2, 0, SEQ // GRID_W - NA_KROWS) * GRID_W, GRID_W)
        kk = jnp.concatenate([k_ref[0, pl.ds(ks, NA_KWIN), :], k_ref[0, SEQ:T_ALL, :]], axis=0)
        vv = jnp.concatenate([v_ref[0, pl.ds(ks, NA_KWIN), :], v_ref[0, SEQ:T_ALL, :]], axis=0)
        q = q_ref[0]
        sls = [slice(h * HEAD_DIM, (h + 1) * HEAD_DIM) for h in range(NA_HEADS)]
        ss = [_dot_t(q[:, sl], kk[:, sl]) for sl in sls]
        ss = [jnp.concatenate([s[:, :NA_KWIN] + bias_ref[0, 0, h], s[:, NA_KWIN:]], axis=-1) for h, s in enumerate(ss)]
        ms = [jnp.max(s, axis=-1, keepdims=True) for s in ss]
        ps = [jnp.exp(s - m).astype(BF16) for s, m in zip(ss, ms)]
        ones = _onehot(lax.broadcasted_iota(jnp.int32, (NA_KALL, HEAD_DIM), 1) == 0)
        oa = [_dot(p, jnp.concatenate([vv[:, sl], ones], axis=-1)) for p, sl in zip(ps, sls)]
        os_ = [o[:, :HEAD_DIM] / o[:, HEAD_DIM:HEAD_DIM + 1] for o in oa]
        o_ref[0] = jnp.concatenate(os_, axis=-1).astype(o_ref.dtype)

    @pl.when(t == N_LAT_TILES)
    def _():
        kk = k_ref[0, SEQ:T_ALL, :]
        vv = v_ref[0, SEQ:T_ALL, :]
        q = q_ref[0]
        for h in range(NA_HEADS):
            sl = slice(h * HEAD_DIM, (h + 1) * HEAD_DIM)
            o_ref[0, :, sl] = _softmax_pv(_dot_t(q[:, sl], kk[:, sl]), vv[:, sl]).astype(o_ref.dtype)


def _na_attention(qa, ka, va, bias, layer, n_tiles):
    B = qa.shape[0]

    def bias_idx(t, b):
        return (layer, jnp.where(t == 0, 0, jnp.where(t >= N_LAT_TILES - 1, 2, 1)), 0, 0, 0)

    return pl.pallas_call(
        _na_kernel,
        grid=(n_tiles, B),
        in_specs=[
            pl.BlockSpec((1, TM, 256), lambda t, b: (b, t, 0)),
            pl.BlockSpec((1, T_ALL, 256), lambda t, b: (b, 0, 0)),
            pl.BlockSpec((1, T_ALL, 256), lambda t, b: (b, 0, 0)),
            pl.BlockSpec((1, 1, NA_HEADS, TM, NA_KWIN), bias_idx),
        ],
        out_specs=pl.BlockSpec((1, TM, 256), lambda t, b: (b, t, 0)),
        out_shape=jax.ShapeDtypeStruct((B, n_tiles * TM, 256), BF16),
        compiler_params=_cparams(("arbitrary", "arbitrary")),
    )(qa, ka, va, bias)


def _mla_kernel(q_ref, k_ref, v_ref, o_ref):
    t = pl.program_id(1)

    def run(key_lo, n_keys):
        keys = slice(key_lo, key_lo + n_keys)
        ksls = [slice(h * MLA_HEAD_PAD, (h + 1) * MLA_HEAD_PAD) for h in range(MLA_HEADS)]
        vsls = [slice(h * MLA_V, (h + 1) * MLA_V) for h in range(MLA_HEADS)]
        ss = [_dot_t(q_ref[0, :, ksl], k_ref[0, keys, ksl]) for ksl in ksls]
        ms = [jnp.max(s, axis=-1, keepdims=True) for s in ss]
        ps = [jnp.exp(s - m).astype(BF16) for s, m in zip(ss, ms)]
        ones = _onehot(lax.broadcasted_iota(jnp.int32, (n_keys, MLA_V), 1) == 0)
        oa = [_dot(p, jnp.concatenate([v_ref[0, keys, vsl], ones], axis=-1)) for p, vsl in zip(ps, vsls)]
        os_ = [o[:, :MLA_V] / o[:, MLA_V:MLA_V + 1] for o in oa]
        o_ref[0] = jnp.concatenate(os_, axis=-1).astype(o_ref.dtype)

    @pl.when(t < N_LAT_TILES)
    def _():
        run(0, T_ALL)

    @pl.when(t == N_LAT_TILES)
    def _():
        run(SEQ, CTX_LEN)


def _mla_attention(qm, km, vm, n_tiles):
    B = qm.shape[0]
    hw = MLA_HEADS * MLA_HEAD_PAD
    vw = MLA_HEADS * MLA_V
    return pl.pallas_call(
        _mla_kernel,
        grid=(B, n_tiles),
        in_specs=[
            pl.BlockSpec((1, TM, hw), lambda b, t: (b, t, 0)),
            pl.BlockSpec((1, T_ALL, hw), lambda b, t: (b, 0, 0)),
            pl.BlockSpec((1, T_ALL, vw), lambda b, t: (b, 0, 0)),
        ],
        out_specs=pl.BlockSpec((1, TM, vw), lambda b, t: (b, t, 0)),
        out_shape=jax.ShapeDtypeStruct((B, n_tiles * TM, vw), BF16),
        compiler_params=_cparams(("arbitrary", "arbitrary")),
    )(qm, km, vm)


POOL_PAD = POOL_WINDOWS[-1] // 2


def _pool_kernel(u_ref, w_ref, scale_ref, o_ref, pad_ref):
    width = POOL_GROUPS * POOL_CH
    lane = lax.broadcasted_iota(jnp.int32, (1, width), 1)
    half = jnp.left_shift(1, lane // POOL_CH)
    zeros = jnp.zeros((POOL_PAD, width), F32)
    base = 0
    for lo, n in ((0, SEQ), (SEQ, CTX_LEN)):
        x = u_ref[0, lo:lo + n, :]
        pad_ref[base:base + POOL_PAD, :] = zeros
        pad_ref[base + POOL_PAD:base + POOL_PAD + n, :] = x
        pad_ref[base + POOL_PAD + n:base + 2 * POOL_PAD + n, :] = zeros

        xp = pad_ref[base:base + n + 2 * POOL_PAD, :]

        def ahead(v, d):
            return jnp.concatenate([v[d:], jnp.zeros((d, width), F32)], axis=0)

        runs, run = {}, xp
        for h in (1, 2, 4, 8):
            run = run + ahead(run, h)
            runs[h] = run[POOL_PAD - h:POOL_PAD - h + n]
        win = jnp.where(half == 1, runs[1], jnp.where(half == 2, runs[2], jnp.where(half == 4, runs[4], runs[8])))
        t = lax.broadcasted_iota(jnp.int32, (n, 1), 0)
        cnt = jnp.minimum(t + half, n) - jnp.maximum(t - half, 0)
        y = (win / cnt.astype(F32) - x).astype(BF16)
        o_ref[0, lo:lo + n, :] = (_dot(y, w_ref[...]) * scale_ref[...]).astype(o_ref.dtype)
        base += n + 2 * POOL_PAD


def _pool(up, w_bd, scale):
    B = up.shape[0]
    return pl.pallas_call(
        _pool_kernel,
        grid=(B,),
        in_specs=[
            pl.BlockSpec((1, T_ALL, 256), lambda b: (b, 0, 0)),
            pl.BlockSpec((256, 256), lambda b: (0, 0)),
            pl.BlockSpec((1, 256), lambda b: (0, 0)),
        ],
        out_specs=pl.BlockSpec((1, T_ALL, 256), lambda b: (b, 0, 0)),
        out_shape=jax.ShapeDtypeStruct((B, T_ALL, 256), BF16),
        scratch_shapes=[pltpu.VMEM((T_ALL + 4 * POOL_PAD, POOL_GROUPS * POOL_CH), F32)],
        compiler_params=_cparams(("arbitrary",)),
    )(up, w_bd, scale)


DFT_ROWS = 768


def _dft_kernel(c_ref, s_ref, x_ref, o_ref):
    x = x_ref[0]
    w = FFT_GROUPS * FFT_CH
    o_ref[0] = (_dot(c_ref[...], x[:, :w]) + _dot(s_ref[...], x[:, w:])).astype(o_ref.dtype)


def _dft_positions(cmat, smat, xcs):
    B = xcs.shape[0]
    return pl.pallas_call(
        _dft_kernel,
        grid=(T_ALL // DFT_ROWS, B),
        in_specs=[
            pl.BlockSpec((DFT_ROWS, T_ALL), lambda i, b: (i, 0)),
            pl.BlockSpec((DFT_ROWS, T_ALL), lambda i, b: (i, 0)),
            pl.BlockSpec((1, T_ALL, 512), lambda i, b: (b, 0, 0)),
        ],
        out_specs=pl.BlockSpec((1, DFT_ROWS, 256), lambda i, b: (b, i, 0)),
        out_shape=jax.ShapeDtypeStruct((B, T_ALL, 256), BF16),
        compiler_params=_cparams(("arbitrary", "arbitrary")),
    )(cmat, smat, xcs)


def _merge_kernel(*refs, n_stream):
    h_refs = refs[:n_stream]
    (mod_ref, ya_ref, yb_ref, yc_ref, yd_ref, wg_ref, wbr_ref, wout_ref, g_ref, b_ref, wr_ref,
     h1_ref, u2_ref, aff_ref) = refs[n_stream:]
    pb = mod_ref.shape[0]
    rows = pb * TM
    h = _load_stream(h_refs)
    sh1 = mod_ref[:, 0, 0:1, :]
    sc1 = mod_ref[:, 0, 1:2, :]
    g1 = mod_ref[:, 0, 2:3, :]
    sh2 = mod_ref[:, 0, 3:4, :]
    sc2 = mod_ref[:, 0, 4:5, :]
    u = (h * (1.0 + sc1) + sh1).astype(BF16).reshape(rows, D_MODEL)
    ys = (ya_ref, yb_ref, yc_ref, yd_ref)
    acc = jnp.zeros((rows, D_MODEL), F32)
    for n in range(N_BRANCH):
        gate = jax.nn.sigmoid(_dot(u, wg_ref[:, n * D_MODEL:(n + 1) * D_MODEL]))
        acc = acc + gate * _dot(ys[n][...].reshape(rows, BRANCH_W), wbr_ref[n])
    mix = _dot(acc.astype(BF16), wout_ref[...]).reshape(pb, TM, D_MODEL)
    h1 = _layer_norm(DN_ALPHA * h + g1 * mix, g_ref[...], b_ref[...])
    h1_ref[...] = h1
    u2 = h1 * (1.0 + sc2) + sh2
    u2_hi = u2.astype(BF16)
    u2_ref[...] = u2_hi
    u2_lo = (u2 - u2_hi.astype(F32)).astype(BF16).reshape(rows, D_MODEL)
    hi_terms = _dot(u2_hi.reshape(rows, D_MODEL), wr_ref[...])
    logits = hi_terms[:, :ROUTER_PAD] + hi_terms[:, ROUTER_PAD:] + _dot(u2_lo, wr_ref[:, :ROUTER_PAD])
    lane = lax.broadcasted_iota(jnp.int32, (1, ROUTER_PAD), 1)
    logits = jnp.where(lane < N_EXPERTS, logits, NEG)
    e = jnp.exp(logits - jnp.max(logits, axis=-1, keepdims=True))
    aff_ref[...] = (e / jnp.sum(e, axis=-1, keepdims=True)).reshape(pb, TM, ROUTER_PAD)


def _merge(hall, mod_all, ya, yb, yc, yd, wg, wbr, wout, ln_g, ln_b, wr_cat, n_tiles):
    B = mod_all.shape[0]
    pb = _samples_per_step(B)
    row = lambda b, t: (b, t, 0)
    t_rows = n_tiles * TM
    stream_specs, stream_args = _stream_specs(hall, pb)
    return pl.pallas_call(
        functools.partial(_merge_kernel, n_stream=len(stream_args)),
        grid=(B // pb, n_tiles),
        in_specs=stream_specs + [
            pl.BlockSpec((pb, 1, 6, D_MODEL), lambda b, t: (b, t // N_LAT_TILES, 0, 0)),
            pl.BlockSpec((pb, TM, BRANCH_W), row),
            pl.BlockSpec((pb, TM, BRANCH_W), row),
            pl.BlockSpec((pb, TM, BRANCH_W), row),
            pl.BlockSpec((pb, TM, BRANCH_W), row),
            _resident(wg.shape),
            _resident(wbr.shape),
            _resident(wout.shape),
            _resident((1, D_MODEL)),
            _resident((1, D_MODEL)),
            _resident(wr_cat.shape),
        ],
        out_specs=[
            pl.BlockSpec((pb, TM, D_MODEL), row),
            pl.BlockSpec((pb, TM, D_MODEL), row),
            pl.BlockSpec((pb, TM, ROUTER_PAD), row),
        ],
        out_shape=[
            jax.ShapeDtypeStruct((B, t_rows, D_MODEL), F32),
            jax.ShapeDtypeStruct((B, t_rows, D_MODEL), BF16),
            jax.ShapeDtypeStruct((B, t_rows, ROUTER_PAD), F32),
        ],
        compiler_params=_cparams(("arbitrary", "arbitrary")),
    )(*stream_args, mod_all, ya, yb, yc, yd, wg, wbr, wout, ln_g, ln_b, wr_cat)


def _onehot(match):
    return jnp.where(match, 1.0, 0.0).astype(BF16)


def _route_kernel(a_ref, tri_ref, rank_ref, *, cap):
    a = a_ref[...]
    n = a.shape[1]

    def count(mask):
        return jnp.sum(jnp.where(mask, 1.0, 0.0), axis=-1, keepdims=True)

    p = jnp.full((a.shape[0], 1), 2.0, a.dtype)
    for k in range(ROUTE_EXP_BITS - 1, -1, -1):
        cand = p * (2.0 ** -(2 ** k))
        p = jnp.where(count(a >= cand) < cap, cand, p)
    found = count(a >= 0.5 * p) >= cap
    lo = jnp.where(found, 0.5 * p, 0.0)

    def refine(_, carry):
        lo, step = carry
        cand = lo + step
        return jnp.where(count(a >= cand) >= cap, cand, lo), 0.5 * step

    thr, _ = lax.fori_loop(0, ROUTE_MANTISSA_STEPS, refine, (lo, 0.5 * lo))
    above = a > thr
    tie = a == thr
    tri = tri_ref[:n, :n]
    tie_f = jnp.where(tie, 1.0, 0.0)
    ties_before = _dot(tie_f.astype(BF16), tri) - tie_f
    keep = above | (tie & (ties_before < cap - count(above)))
    keep_f = jnp.where(keep, 1.0, 0.0)
    slot = _dot(keep_f.astype(BF16), tri) - keep_f
    rank_ref[...] = jnp.where(keep, slot, -1.0).astype(jnp.int32)


def _route_set(aff_t, tri, cap):
    return pl.pallas_call(
        functools.partial(_route_kernel, cap=cap),
        out_shape=jax.ShapeDtypeStruct(aff_t.shape, jnp.int32),
        compiler_params=pltpu.CompilerParams(vmem_limit_bytes=VMEM_LIMIT),
    )(aff_t, tri[:aff_t.shape[1], :aff_t.shape[1]])


def _gather_kernel(start_ref, nwin_ref, rank_ref, aff_ref, *rest):
    *ctx_refs, u_ref, o_ref, g_ref = rest
    b = pl.program_id(0)
    t = pl.program_id(1)

    @pl.when(t == 0)
    def _():
        o_ref[...] = jnp.zeros_like(o_ref)
        g_ref[...] = jnp.zeros_like(g_ref)

    def latent():
        rank = rank_ref[...]
        aff = aff_ref[...]
        u = u_ref[0]
        offset = lax.broadcasted_iota(jnp.int32, (SLOT_WIN, TM), 0)

        def one_pass(k, carry):
            picks, los = [], []
            for e in range(N_EXPERTS):
                w = start_ref[b, t, e] + k * SLOT_WIN
                lo = pl.multiple_of(jnp.minimum(w, CAP_LAT - SLOT_WIN), 16)
                r = rank[e:e + 1, :]
                picks.append((r - lo == offset) & (r >= w))
                los.append(lo)
            rows = _dot(jnp.concatenate([_onehot(p) for p in picks], axis=0), u)
            for e in range(N_EXPERTS):
                sl = pl.ds(los[e], SLOT_WIN)
                new = o_ref[e, sl, :].astype(F32) + rows[e * SLOT_WIN:(e + 1) * SLOT_WIN]
                o_ref[e, sl, :] = new.astype(o_ref.dtype)
                g_ref[e, sl, :] += jnp.sum(jnp.where(picks[e], aff[e:e + 1, :], 0.0), axis=-1, keepdims=True)
            return carry

        lax.fori_loop(0, nwin_ref[b, t], one_pass, 0)

    def context():
        rank = ctx_refs[0][...]
        aff = ctx_refs[1][...]
        slot = lax.broadcasted_iota(jnp.int32, (CAP_CTX, CTX_LEN), 0)
        picks = [slot == rank[e:e + 1, :] for e in range(N_EXPERTS)]
        rows = _dot(jnp.concatenate([_onehot(p) for p in picks], axis=0), u_ref[0])
        for e in range(N_EXPERTS):
            o_ref[e, CAP_LAT:CAP_ALL, :] = rows[e * CAP_CTX:(e + 1) * CAP_CTX].astype(o_ref.dtype)
            g_ref[e, CAP_LAT:CAP_ALL, :] = jnp.sum(jnp.where(picks[e], aff[e:e + 1, :], 0.0), axis=-1, keepdims=True)

    if ctx_refs:
        pl.when(t < N_LAT_TILES)(latent)
        pl.when(t == N_LAT_TILES)(context)
    else:
        latent()


def _gather_tokens(ranks, affs, u2, start, nwin):
    B = u2.shape[0]
    with_ctx = len(ranks) > 1
    cap = CAP_ALL if with_ctx else CAP_LAT
    lat_spec = pl.BlockSpec((N_EXPERTS, TM), lambda b, t, *_: (b, jnp.minimum(t, N_LAT_TILES - 1)))
    ctx_spec = pl.BlockSpec((N_EXPERTS, CTX_LEN), lambda b, t, *_: (b, 0))
    grid_spec = pltpu.PrefetchScalarGridSpec(
        num_scalar_prefetch=2,
        grid=(B, N_TILES if with_ctx else N_LAT_TILES),
        in_specs=[lat_spec, lat_spec] + ([ctx_spec, ctx_spec] if with_ctx else [])
        + [pl.BlockSpec((1, TM, D_MODEL), lambda b, t, *_: (b, t, 0))],
        out_specs=[
            pl.BlockSpec((N_EXPERTS, cap, D_MODEL), lambda b, t, *_: (0, b, 0)),
            pl.BlockSpec((N_EXPERTS, cap, 1), lambda b, t, *_: (0, b, 0)),
        ],
    )
    args = [ranks[0], affs[0]] + ([ranks[1], affs[1]] if with_ctx else [])
    return pl.pallas_call(
        _gather_kernel,
        grid_spec=grid_spec,
        out_shape=[
            jax.ShapeDtypeStruct((N_EXPERTS, B * cap, D_MODEL), BF16),
            jax.ShapeDtypeStruct((N_EXPERTS, B * cap, 1), F32),
        ],
        compiler_params=_cparams(("arbitrary", "arbitrary")),
    )(start, nwin, *args, u2)


def _expert_kernel(x_ref, g_ref, wg_ref, wu_ref, wd_ref, o_ref, acc_ref, *, chunk):
    f = pl.program_id(1)

    @pl.when(f == 0)
    def _():
        acc_ref[...] = jnp.zeros_like(acc_ref)

    wg = wg_ref[0, 0].astype(BF16)
    wu = wu_ref[0, 0].astype(BF16)
    wd = wd_ref[0, 0].astype(BF16)
    for c in range(x_ref.shape[1] // chunk):
        rows = pl.ds(c * chunk, chunk)
        x = x_ref[0, rows, :]
        a = _dot(x, wg)
        u = _dot(x, wu)
        acc_ref[rows, :] += _dot((a * jax.nn.sigmoid(a) * u).astype(BF16), wd)

    @pl.when(f == pl.num_programs(1) - 1)
    def _():
        o_ref[0] = (acc_ref[...] * g_ref[0]).astype(o_ref.dtype)


def _experts(xe, gcol, w_gate, w_up, w_down, layer):
    rows = xe.shape[1]
    chunk = next(c for c in (XROW_CHUNK, 512, CAP_ALL, CAP_LAT) if rows % c == 0)
    return pl.pallas_call(
        functools.partial(_expert_kernel, chunk=chunk),
        grid=(N_EXPERTS, EXPERT_FF // FF_TILE),
        in_specs=[
            pl.BlockSpec((1, rows, D_MODEL), lambda e, f: (e, 0, 0)),
            pl.BlockSpec((1, rows, 1), lambda e, f: (e, 0, 0)),
            pl.BlockSpec((1, 1, D_MODEL, FF_TILE), lambda e, f: (layer, e, 0, f)),
            pl.BlockSpec((1, 1, D_MODEL, FF_TILE), lambda e, f: (layer, e, 0, f)),
            pl.BlockSpec((1, 1, FF_TILE, D_MODEL), lambda e, f: (layer, e, f, 0)),
        ],
        out_specs=pl.BlockSpec((1, rows, D_MODEL), lambda e, f: (e, 0, 0)),
        out_shape=jax.ShapeDtypeStruct((N_EXPERTS, rows, D_MODEL), BF16),
        scratch_shapes=[pltpu.VMEM((rows, D_MODEL), F32)],
        compiler_params=_cparams(("arbitrary", "arbitrary")),
    )(xe, gcol, w_gate, w_up, w_down)


SLOT_WIN = 64


def _scatter_kernel(start_ref, nwin_ref, h_ref, mod_ref, y_ref, g_ref, b_ref, rank_ref, o_ref, fx_ref):
    b = pl.program_id(0)
    t = pl.program_id(1)
    rank = rank_ref[0]

    def spread_ranks(width):
        n = N_EXPERTS * width
        col = lax.broadcasted_iota(jnp.int32, (N_EXPERTS, n), 1)
        spread = _onehot(col // width == lax.broadcasted_iota(jnp.int32, (N_EXPERTS, n), 0))
        return _dot(rank.astype(F32).astype(BF16), spread)

    def finish(fx):
        g2 = mod_ref[0, 0, 5:6, :]
        o_ref[0] = _layer_norm(DN_ALPHA * h_ref[0] + g2 * fx, g_ref[...], b_ref[...])

    def latent():
        n = N_EXPERTS * SLOT_WIN
        rank_cols = spread_ranks(SLOT_WIN)
        lane = lax.broadcasted_iota(jnp.int32, (1, n), 1)
        block = lane // SLOT_WIN
        offset = (lane % SLOT_WIN).astype(F32)
        fx_ref[...] = jnp.zeros_like(fx_ref)

        def one_pass(k, carry):
            want = jnp.zeros((1, n), F32)
            have = jnp.zeros((1, n), F32)
            rows = []
            for e in range(N_EXPERTS):
                w = start_ref[b, t, e] + k * SLOT_WIN
                lo = pl.multiple_of(jnp.minimum(w, CAP_LAT - SLOT_WIN), 16)
                rows.append(y_ref[e, 0, pl.ds(lo, SLOT_WIN), :])
                want = jnp.where(block == e, w.astype(F32), want)
                have = jnp.where(block == e, lo.astype(F32), have)
            hit = (rank_cols - have == offset) & (rank_cols >= want)
            fx_ref[...] += _dot(_onehot(hit), jnp.concatenate(rows, axis=0))
            return carry

        lax.fori_loop(0, nwin_ref[b, t], one_pass, 0)
        finish(fx_ref[...])

    def context():
        n = N_EXPERTS * CAP_CTX
        slot = (lax.broadcasted_iota(jnp.int32, (TM, n), 1) % CAP_CTX).astype(F32)
        y = y_ref[:, 0, CAP_LAT:CAP_ALL, :].reshape(n, D_MODEL)
        finish(_dot(_onehot(spread_ranks(CAP_CTX) == slot), y))

    if y_ref.shape[2] == CAP_LAT:
        latent()
    else:
        pl.when(t < N_LAT_TILES)(latent)
        pl.when(t == N_LAT_TILES)(context)


def _slot_windows(rank_lat):
    B = rank_lat.shape[0] // N_EXPERTS
    cnt = jnp.sum((rank_lat >= 0).reshape(B, N_EXPERTS, N_LAT_TILES, TM), axis=-1, dtype=jnp.int32)
    first = jnp.cumsum(cnt, axis=-1) - cnt
    start = (first // 16) * 16
    nwin = jnp.max((first - start + cnt + SLOT_WIN - 1) // SLOT_WIN, axis=1)
    return start.transpose(0, 2, 1), nwin


def _scatter_ln(h1, mod_all, ye, ln_g, ln_b, rank_tok, start, nwin):
    B = h1.shape[0]
    cap = ye.shape[2]
    n_tiles = rank_tok.shape[1] // TM
    grid_spec = pltpu.PrefetchScalarGridSpec(
        num_scalar_prefetch=2,
        grid=(B, n_tiles),
        in_specs=[
            pl.BlockSpec((1, TM, D_MODEL), lambda b, t, *_: (b, t, 0)),
            pl.BlockSpec((1, 1, 6, D_MODEL), lambda b, t, *_: (b, t // N_LAT_TILES, 0, 0)),
            pl.BlockSpec((N_EXPERTS, 1, cap, D_MODEL), lambda b, t, *_: (0, b, 0, 0)),
            pl.BlockSpec((1, D_MODEL), lambda b, t, *_: (0, 0)),
            pl.BlockSpec((1, D_MODEL), lambda b, t, *_: (0, 0)),
            pl.BlockSpec((1, TM, N_EXPERTS), lambda b, t, *_: (b, t, 0)),
        ],
        out_specs=pl.BlockSpec((1, TM, D_MODEL), lambda b, t, *_: (b, t, 0)),
        scratch_shapes=[pltpu.VMEM((TM, D_MODEL), F32)],
    )
    return pl.pallas_call(
        _scatter_kernel,
        grid_spec=grid_spec,
        out_shape=jax.ShapeDtypeStruct((B, n_tiles * TM, D_MODEL), F32),
        compiler_params=_cparams(("arbitrary", "arbitrary")),
    )(start, nwin, h1, mod_all, ye, ln_g, ln_b, rank_tok)


def _rope_tables():
    n_freq = MLA_ROPE // 4
    inv = ROPE_BASE ** (-jnp.arange(n_freq, dtype=F32) / n_freq)
    t = jnp.arange(SEQ)
    row = (t // GRID_W).astype(F32)
    col = (t % GRID_W).astype(F32)
    ang = jnp.concatenate([row[:, None] * inv, col[:, None] * inv], axis=-1)
    cos, sin = jnp.cos(ang), jnp.sin(ang)
    ones = jnp.ones((SEQ, MLA_NOPE), F32)
    pad1 = jnp.ones((SEQ, MLA_HEAD_PAD - MLA_NOPE - MLA_ROPE), F32)
    cos_t = jnp.concatenate([ones, cos, cos, pad1], axis=-1)
    sin_t = jnp.concatenate([0 * ones, -sin, sin, 0 * pad1], axis=-1)
    cos_t = jnp.concatenate([cos_t, jnp.ones((CTX_LEN, MLA_HEAD_PAD), F32)], axis=0)
    sin_t = jnp.concatenate([sin_t, jnp.zeros((CTX_LEN, MLA_HEAD_PAD), F32)], axis=0)
    return cos_t, sin_t


def _dft_tables():
    def block(n):
        split = 64
        k = jnp.arange(n, dtype=jnp.int32)[None, :]
        jh = jnp.arange(n // split, dtype=jnp.int32)[:, None]
        jl = jnp.arange(split, dtype=jnp.int32)[:, None]
        ang_h = ((jh * split * k) % n).astype(F32) * (2.0 * np.pi / n)
        ang_l = ((jl * k) % n).astype(F32) * (2.0 * np.pi / n)
        ch, sh = jnp.cos(ang_h)[:, None, :], jnp.sin(ang_h)[:, None, :]
        cl_, sl_ = jnp.cos(ang_l)[None, :, :], jnp.sin(ang_l)[None, :, :]
        sc = 1.0 / np.sqrt(n * FFT_CH)
        return ((ch * cl_ - sh * sl_) * sc).reshape(n, n), (-(sh * cl_ + ch * sl_) * sc).reshape(n, n)

    cl, sl = block(SEQ)
    cc, sc_ = block(CTX_LEN)

    def diag(a, b):
        top = jnp.concatenate([a, jnp.zeros((SEQ, CTX_LEN), F32)], axis=1)
        bot = jnp.concatenate([jnp.zeros((CTX_LEN, SEQ), F32), b], axis=1)
        return jnp.concatenate([top, bot], axis=0).astype(BF16)

    j = jnp.arange(FFT_CH, dtype=jnp.int32)
    ang = ((j[:, None] * j[None, :]) % FFT_CH).astype(F32) * (2.0 * np.pi / FFT_CH)
    eye = jnp.eye(FFT_GROUPS, dtype=F32)
    bd = jnp.concatenate([jnp.kron(eye, jnp.cos(ang)), jnp.kron(eye, jnp.sin(ang))], axis=1).astype(BF16)
    return diag(cl, cc), diag(sl, sc_), bd


def _na_bias_index():
    R = SEQ // GRID_W
    pats = [(0, 0), (NA_QROWS, 0), (R - NA_QROWS, R - NA_KROWS)]
    q = np.arange(TM)
    k = np.arange(NA_KWIN)
    qi, qc = q // GRID_W, q % GRID_W
    kj, kc = k // GRID_W, k % GRID_W
    c0 = np.clip(qc - NA_WIN_COLS // 2, 0, GRID_W - NA_WIN_COLS)
    col_in = (kc[None, :] >= c0[:, None]) & (kc[None, :] < c0[:, None] + NA_WIN_COLS)
    col_off = np.clip(kc[None, :] - qc[:, None], -(NA_WIN_COLS - 1), NA_WIN_COLS - 1) + (NA_WIN_COLS - 1)
    del col_off
    row_offs, valids = [], []
    for rb, ks in pats:
        qr = rb + np.arange(NA_QROWS)
        kr = ks + np.arange(NA_KROWS)
        r0 = np.clip(qr - NA_WIN_ROWS // 2, 0, R - NA_WIN_ROWS)
        row_in = (kr[None, :] >= r0[:, None]) & (kr[None, :] < r0[:, None] + NA_WIN_ROWS)
        row_offs.append(np.where(row_in, kr[None, :] - qr[:, None] + (NA_WIN_ROWS - 1), -1))
        valids.append(row_in[qi][:, kj] & col_in)
    return np.stack(row_offs), np.stack(valids)


def _na_bias(rpb):
    row_off, valid = _na_bias_index()
    reach = GRID_W - NA_WIN_COLS
    padded = jnp.pad(rpb, ((0, 0), (0, 0), (0, 0), (reach, reach)))
    band = jnp.stack([padded[..., GRID_W - 1 - qc:2 * GRID_W - 1 - qc] for qc in range(GRID_W)], axis=-2)
    masked = jnp.zeros(band.shape[:2] + band.shape[3:], F32)
    pats = []
    for p in range(row_off.shape[0]):
        qrows = []
        for qi in range(NA_QROWS):
            blocks = [masked if row_off[p, qi, kj] < 0 else band[:, :, row_off[p, qi, kj]] for kj in range(NA_KROWS)]
            qrows.append(jnp.concatenate(blocks, axis=-1))
        pats.append(jnp.concatenate(qrows, axis=-2))
    b = jnp.stack(pats, axis=1)
    return jnp.where(jnp.asarray(valid)[None, :, None], b, NEG)


def _rearranged_in_weights(w_in):
    z = lambda n: jnp.zeros((D_MODEL, n), F32)
    kr = w_in[:, 1664:1696]
    kr_e, kr_o = kr[:, 0::2], kr[:, 1::2]
    pad = MLA_HEAD_PAD - MLA_NOPE - MLA_ROPE
    krb = jnp.concatenate([z(MLA_NOPE), kr_e, kr_o, z(pad)], axis=1)
    krs = jnp.concatenate([z(MLA_NOPE), kr_o, kr_e, z(pad)], axis=1)
    wsm = jnp.concatenate([w_in[:, :1664], krb, krs], axis=1).astype(BF16)
    return wsm, w_in[:, 1696:].astype(BF16)


def _rearranged_mla_weights(w_uq, w_ukv):
    pad = MLA_HEAD_PAD - MLA_NOPE - MLA_ROPE
    wq = w_uq.reshape(MLA_Q_RANK, MLA_HEADS, MLA_NOPE + MLA_ROPE)
    nope, rope = wq[..., :MLA_NOPE], wq[..., MLA_NOPE:]
    r_e, r_o = rope[..., 0::2], rope[..., 1::2]
    zq = jnp.zeros((MLA_Q_RANK, MLA_HEADS, pad), F32)
    q_main = jnp.concatenate([nope, r_e, r_o, zq], axis=-1).reshape(MLA_Q_RANK, -1)
    q_swap = jnp.concatenate([nope, r_o, r_e, zq], axis=-1).reshape(MLA_Q_RANK, -1)
    wuq2 = jnp.concatenate([q_main, q_swap], axis=1).astype(BF16)
    wkv = w_ukv.reshape(MLA_KV_RANK, MLA_HEADS, MLA_NOPE + MLA_V)
    k_nope, v = wkv[..., :MLA_NOPE], wkv[..., MLA_NOPE:]
    zk = jnp.zeros((MLA_KV_RANK, MLA_HEADS, MLA_HEAD_PAD - MLA_NOPE), F32)
    k_main = jnp.concatenate([k_nope, zk], axis=-1).reshape(MLA_KV_RANK, -1)
    wukv2 = jnp.concatenate([k_main, v.reshape(MLA_KV_RANK, -1)], axis=1).astype(BF16)
    return wuq2, wukv2


def _route(aff, tri, with_ctx):
    B = aff.shape[0]
    a = aff.transpose(0, 2, 1).reshape(B * N_EXPERTS, -1)
    affs = [a[:, :SEQ]] + ([a[:, SEQ:]] if with_ctx else [])
    ranks = [_route_set(s, tri, cap) for s, cap in zip(affs, (CAP_LAT, CAP_CTX))]
    rank_tok = jnp.concatenate([r.reshape(B, N_EXPERTS, -1) for r in ranks], axis=-1).transpose(0, 2, 1)
    return affs, ranks, rank_tok


def kernel(x, c, ctx, c_ctx, w_mod, b_mod, w_in, na_rpb, pool_w, pool_scale, mla_q_norm, mla_w_uq, mla_kv_norm,
           mla_w_ukv, w_branch, w_out, ln1_g, ln1_b, w_router, w_gate, w_up, w_down, ln2_g, ln2_b):
    B = x.shape[0]
    assert x.shape == (B, SEQ, D_MODEL) and ctx.shape == (B, CTX_LEN, D_MODEL) and B + 1 <= 16

    cc = jnp.concatenate([c, c_ctx[None], jnp.zeros((16 - B - 1, D_MODEL), F32)], axis=0)
    mod = _modulation(cc, w_mod, b_mod).reshape(DEPTH, 16, 6, D_MODEL)
    cos_t, sin_t = _rope_tables()
    cmat, smat, bd = _dft_tables()
    na_bias = _na_bias(na_rpb)
    tri = jnp.triu(jnp.ones((SEQ, SEQ), BF16))
    hall = (x, ctx)

    for l in range(DEPTH):
        with_ctx = l < DEPTH - 1
        n_tiles = N_TILES if with_ctx else N_LAT_TILES
        mod_all = jnp.stack([mod[l, :B], jnp.broadcast_to(mod[l, B], (B, 6, D_MODEL))], axis=1)
        wsm, wgates = _rearranged_in_weights(w_in[l])
        wuq2, wukv2 = _rearranged_mla_weights(mla_w_uq[l], mla_w_ukv[l])
        qa, ka, va, up, xcs, qm, km, vm = _inproj(
            hall, mod_all, wsm, mla_q_norm[l][None], mla_kv_norm[l][None], wuq2, wukv2, cos_t, sin_t, bd)
        ya = _na_attention(qa, ka, va, na_bias, l, n_tiles)
        pool_bd = jax.scipy.linalg.block_diag(*[pool_w[l, g] for g in range(POOL_GROUPS)]).astype(BF16)
        yb = _pool(up, pool_bd, pool_scale[l][None])
        yc = _dft_positions(cmat, smat, xcs)
        yd = _mla_attention(qm, km, vm, n_tiles)
        wr = jnp.pad(w_router[l], ((0, 0), (0, ROUTER_PAD - N_EXPERTS)))
        wr_hi = wr.astype(BF16)
        wr_cat = jnp.concatenate([wr_hi, (wr - wr_hi.astype(F32)).astype(BF16)], axis=1)
        h1, u2, aff = _merge(hall, mod_all, ya, yb, yc, yd, wgates, w_branch[l].astype(BF16), w_out[l].astype(BF16),
                             ln1_g[l][None], ln1_b[l][None], wr_cat, n_tiles)
        affs, ranks, rank_tok = _route(aff[:, :, :N_EXPERTS], tri, with_ctx)
        start, nwin = _slot_windows(ranks[0])
        xe, gcol = _gather_tokens(ranks, affs, u2, start, nwin)
        ye = _experts(xe, gcol, w_gate, w_up, w_down, l)
        hall = _scatter_ln(h1, mod_all, ye.reshape(N_EXPERTS, B, -1, D_MODEL), ln2_g[l][None], ln2_b[l][None], rank_tok,
                           start, nwin)
    return hall
```

```python
import functools

import numpy as np
import jax
import jax.numpy as jnp
from jax import lax
from jax.experimental import pallas as pl
from jax.experimental.pallas import tpu as pltpu

F32 = jnp.float32
BF16 = jnp.bfloat16

D_MODEL = 1024
DEPTH = 4
GRID_W = 64
SEQ = 2048
CTX_LEN = 256
T_ALL = SEQ + CTX_LEN
HEAD_DIM = 64
NA_HEADS = 4
NA_WIN_ROWS = 8
NA_WIN_COLS = 16
NA_SCALE = HEAD_DIM ** -0.5
POOL_GROUPS = 4
POOL_CH = 64
POOL_WINDOWS = (2, 4, 8, 16)
FFT_GROUPS = 4
FFT_CH = 64
MLA_HEADS = 4
MLA_Q_RANK = 256
MLA_KV_RANK = 128
MLA_NOPE = 64
MLA_ROPE = 32
MLA_V = 64
MLA_SCALE = (MLA_NOPE + MLA_ROPE) ** -0.5
ROPE_BASE = 10000.0
BRANCH_W = 256
N_BRANCH = 4
N_EXPERTS = 16
EXPERT_FF = 2048
EC_CAPACITY = 2
DN_ALPHA = (2 * DEPTH) ** 0.25
LN_EPS = 1e-5
NEG = -1e30

LANE = 128
TM = 256
N_TILES = T_ALL // TM
N_LAT_TILES = SEQ // TM
MLA_HEAD_PAD = 128
NA_QROWS = TM // GRID_W
NA_KROWS = 12
NA_KWIN = NA_KROWS * GRID_W
NA_KALL = NA_KWIN + CTX_LEN
CAP_LAT = EC_CAPACITY * SEQ // N_EXPERTS
CAP_CTX = EC_CAPACITY * CTX_LEN // N_EXPERTS
CAP_ALL = CAP_LAT + CAP_CTX
ROUTER_PAD = 128
ROUTE_EXP_BITS = 11
ROUTE_MANTISSA_STEPS = 52
FF_TILE = 512
XROW_CHUNK = 576
VMEM_LIMIT = 56 * 1024 * 1024

C_QA, C_KA, C_VA, C_UP, C_UF, C_CQ, C_CKV, C_KR, C_KRS, C_END = 0, 256, 512, 768, 1024, 1280, 1536, 1664, 1792, 1920


def _cparams(sem):
    return pltpu.CompilerParams(dimension_semantics=sem, vmem_limit_bytes=VMEM_LIMIT)


def _dot(a, b):
    return jnp.dot(a, b, preferred_element_type=F32)


def _dot_t(a, b):
    return lax.dot_general(a, b, (((1,), (1,)), ((), ())), preferred_element_type=F32)


def _layer_norm(x, g, b):
    mu = jnp.mean(x, axis=-1, keepdims=True)
    xc = x - mu
    var = jnp.mean(xc * xc, axis=-1, keepdims=True)
    return xc * lax.rsqrt(var + LN_EPS) * g + b


def _mod_kernel(c_ref, w_ref, b_ref, o_ref):
    c = c_ref[...]
    s = c * jax.nn.sigmoid(c)
    w = w_ref[0]
    s_hi = s.astype(BF16)
    s_lo = (s - s_hi.astype(F32)).astype(BF16)
    w_hi = w.astype(BF16)
    w_lo = (w - w_hi.astype(F32)).astype(BF16)
    o_ref[0] = _dot(s_hi, w_hi) + _dot(s_hi, w_lo) + _dot(s_lo, w_hi) + b_ref[0]


def _modulation(cc, w_mod, b_mod):
    rows = cc.shape[0]
    n6 = 6 * D_MODEL
    tn = 1536
    return pl.pallas_call(
        _mod_kernel,
        grid=(DEPTH, n6 // tn),
        in_specs=[
            pl.BlockSpec((rows, D_MODEL), lambda l, j: (0, 0)),
            pl.BlockSpec((1, D_MODEL, tn), lambda l, j: (l, 0, j)),
            pl.BlockSpec((1, 1, tn), lambda l, j: (l, 0, j)),
        ],
        out_specs=pl.BlockSpec((1, rows, tn), lambda l, j: (l, 0, j)),
        out_shape=jax.ShapeDtypeStruct((DEPTH, rows, n6), F32),
        compiler_params=_cparams(("arbitrary", "arbitrary")),
    )(cc, w_mod, b_mod.reshape(DEPTH, 1, n6))


def _stream_specs(stream, pb):
    if not isinstance(stream, tuple):
        return [pl.BlockSpec((pb, TM, D_MODEL), lambda b, t: (b, t, 0))], [stream]
    return [pl.BlockSpec((pb, TM, D_MODEL), lambda b, t: (b, jnp.minimum(t, N_LAT_TILES - 1), 0)),
            pl.BlockSpec((pb, TM, D_MODEL), lambda b, t: (b, 0, 0))], list(stream)


def _load_stream(h_refs):
    if len(h_refs) == 1:
        return h_refs[0][...]
    return jnp.where(pl.program_id(1) < N_LAT_TILES, h_refs[0][...], h_refs[1][...])


def _inproj_kernel(*refs, n_stream):
    h_refs = refs[:n_stream]
    (mod_ref, wsm_ref, qn_ref, kvn_ref, wuq_ref, wukv_ref, cos_ref, sin_ref, bd_ref,
     qa_ref, ka_ref, va_ref, up_ref, xcs_ref, qm_ref, km_ref, vm_ref) = refs[n_stream:]
    pb = mod_ref.shape[0]

    def put(ref, val):
        ref[...] = val.astype(ref.dtype).reshape(ref.shape)

    sh1 = mod_ref[:, 0, 0:1, :]
    sc1 = mod_ref[:, 0, 1:2, :]
    u = (_load_stream(h_refs) * (1.0 + sc1) + sh1).astype(BF16).reshape(pb * TM, D_MODEL)
    z = _dot(u, wsm_ref[...])
    put(qa_ref, z[:, C_QA:C_KA] * NA_SCALE)
    put(ka_ref, z[:, C_KA:C_VA])
    put(va_ref, z[:, C_VA:C_UP])
    put(up_ref, z[:, C_UP:C_UF])
    put(xcs_ref, _dot(z[:, C_UF:C_CQ].astype(BF16), bd_ref[...]))

    cos = jnp.concatenate([cos_ref[...]] * pb, axis=0)
    sin = jnp.concatenate([sin_ref[...]] * pb, axis=0)
    cos4 = jnp.concatenate([cos] * MLA_HEADS, axis=-1)
    sin4 = jnp.concatenate([sin] * MLA_HEADS, axis=-1)

    cq = z[:, C_CQ:C_CKV]
    nq = cq * lax.rsqrt(jnp.mean(cq * cq, axis=-1, keepdims=True) + LN_EPS) * qn_ref[...]
    q2 = _dot(nq.astype(BF16), wuq_ref[...])
    hw = MLA_HEADS * MLA_HEAD_PAD
    put(qm_ref, (q2[:, :hw] * cos4 + q2[:, hw:] * sin4) * MLA_SCALE)

    ckv = z[:, C_CKV:C_KR]
    nkv = ckv * lax.rsqrt(jnp.mean(ckv * ckv, axis=-1, keepdims=True) + LN_EPS) * kvn_ref[...]
    kv2 = _dot(nkv.astype(BF16), wukv_ref[...])
    kr = z[:, C_KR:C_KRS] * cos + z[:, C_KRS:C_END] * sin
    put(km_ref, kv2[:, :hw] + jnp.concatenate([kr] * MLA_HEADS, axis=-1))
    put(vm_ref, kv2[:, hw:])


def _samples_per_step(B):
    return 2 if B % 2 == 0 else 1


def _resident(shape):
    return pl.BlockSpec(shape, lambda *_: (0,) * len(shape), pipeline_mode=pl.Buffered(1))


def _inproj(hall, mod_all, wsm, qn, kvn, wuq2, wukv2, cos_t, sin_t, bd):
    B = mod_all.shape[0]
    pb = _samples_per_step(B)
    hw = MLA_HEADS * MLA_HEAD_PAD
    row = lambda b, t: (b, t, 0)
    outs = [
        (256, BF16), (256, BF16), (256, BF16), (256, F32), (512, BF16), (hw, BF16), (hw, BF16), (MLA_HEADS * MLA_V, BF16),
    ]
    stream_specs, stream_args = _stream_specs(hall, pb)
    return pl.pallas_call(
        functools.partial(_inproj_kernel, n_stream=len(stream_args)),
        grid=(B // pb, N_TILES),
        in_specs=stream_specs + [
            pl.BlockSpec((pb, 1, 6, D_MODEL), lambda b, t: (b, t // N_LAT_TILES, 0, 0)),
            _resident(wsm.shape),
            _resident(qn.shape),
            _resident(kvn.shape),
            _resident(wuq2.shape),
            _resident(wukv2.shape),
            pl.BlockSpec((TM, LANE), lambda b, t: (t, 0)),
            pl.BlockSpec((TM, LANE), lambda b, t: (t, 0)),
            _resident(bd.shape),
        ],
        out_specs=[pl.BlockSpec((pb, TM, w), row) for w, _ in outs],
        out_shape=[jax.ShapeDtypeStruct((B, T_ALL, w), dt) for w, dt in outs],
        compiler_params=_cparams(("arbitrary", "arbitrary")),
    )(*stream_args, mod_all, wsm, qn, kvn, wuq2, wukv2, cos_t, sin_t, bd)


def _softmax_pv(s, v):
    m = jnp.max(s, axis=-1, keepdims=True)
    p = jnp.exp(s - m)
    l = jnp.sum(p, axis=-1, keepdims=True)
    return _dot(p.astype(BF16), v) / l


def _na_kernel(q_ref, k_ref, v_ref, bias_ref, o_ref):
    t = pl.program_id(0)

    @pl.when(t < N_LAT_TILES)
    def _():
        ks = pl.multiple_of(jnp.clip(t * NA_QROWS - NA_WIN_ROWS // 2, 0, SEQ // GRID_W - NA_KROWS) * GRID_W, GRID_W)
        kk = jnp.concatenate([k_ref[0, pl.ds(ks, NA_KWIN), :], k_ref[0, SEQ:T_ALL, :]], axis=0)
        vv = jnp.concatenate([v_ref[0, pl.ds(ks, NA_KWIN), :], v_ref[0, SEQ:T_ALL, :]], axis=0)
        q = q_ref[0]
        sls = [slice(h * HEAD_DIM, (h + 1) * HEAD_DIM) for h in range(NA_HEADS)]
        ss = [_dot_t(q[:, sl], kk[:, sl]) for sl in sls]
        ss = [jnp.concatenate([s[:, :NA_KWIN] + bias_ref[0, 0, h], s[:, NA_KWIN:]], axis=-1) for h, s in enumerate(ss)]
        ms = [jnp.max(s, axis=-1, keepdims=True) for s in ss]
        ps = [jnp.exp(s - m).astype(BF16) for s, m in zip(ss, ms)]
        ones = _onehot(lax.broadcasted_iota(jnp.int32, (NA_KALL, HEAD_DIM), 1) == 0)
        oa = [_dot(p, jnp.concatenate([vv[:, sl], ones], axis=-1)) for p, sl in zip(ps, sls)]
        os_ = [o[:, :HEAD_DIM] / o[:, HEAD_DIM:HEAD_DIM + 1] for o in oa]
        o_ref[0] = jnp.concatenate(os_, axis=-1).astype(o_ref.dtype)

    @pl.when(t == N_LAT_TILES)
    def _():
        kk = k_ref[0, SEQ:T_ALL, :]
        vv = v_ref[0, SEQ:T_ALL, :]
        q = q_ref[0]
        for h in range(NA_HEADS):
            sl = slice(h * HEAD_DIM, (h + 1) * HEAD_DIM)
            o_ref[0, :, sl] = _softmax_pv(_dot_t(q[:, sl], kk[:, sl]), vv[:, sl]).astype(o_ref.dtype)


def _na_attention(qa, ka, va, bias, layer, n_tiles):
    B = qa.shape[0]

    def bias_idx(t, b):
        return (layer, jnp.where(t == 0, 0, jnp.where(t >= N_LAT_TILES - 1, 2, 1)), 0, 0, 0)

    return pl.pallas_call(
        _na_kernel,
        grid=(n_tiles, B),
        in_specs=[
            pl.BlockSpec((1, TM, 256), lambda t, b: (b, t, 0)),
            pl.BlockSpec((1, T_ALL, 256), lambda t, b: (b, 0, 0)),
            pl.BlockSpec((1, T_ALL, 256), lambda t, b: (b, 0, 0)),
            pl.BlockSpec((1, 1, NA_HEADS, TM, NA_KWIN), bias_idx),
        ],
        out_specs=pl.BlockSpec((1, TM, 256), lambda t, b: (b, t, 0)),
        out_shape=jax.ShapeDtypeStruct((B, n_tiles * TM, 256), BF16),
        compiler_params=_cparams(("arbitrary", "arbitrary")),
    )(qa, ka, va, bias)


def _mla_kernel(q_ref, k_ref, v_ref, o_ref):
    t = pl.program_id(1)

    def run(key_lo, n_keys):
        keys = slice(key_lo, key_lo + n_keys)
        ksls = [slice(h * MLA_HEAD_PAD, (h + 1) * MLA_HEAD_PAD) for h in range(MLA_HEADS)]
        vsls = [slice(h * MLA_V, (h + 1) * MLA_V) for h in range(MLA_HEADS)]
        ss = [_dot_t(q_ref[0, :, ksl], k_ref[0, keys, ksl]) for ksl in ksls]
        ms = [jnp.max(s, axis=-1, keepdims=True) for s in ss]
        ps = [jnp.exp(s - m).astype(BF16) for s, m in zip(ss, ms)]
        ones = _onehot(lax.broadcasted_iota(jnp.int32, (n_keys, MLA_V), 1) == 0)
        oa = [_dot(p, jnp.concatenate([v_ref[0, keys, vsl], ones], axis=-1)) for p, vsl in zip(ps, vsls)]
        os_ = [o[:, :MLA_V] / o[:, MLA_V:MLA_V + 1] for o in oa]
        o_ref[0] = jnp.concatenate(os_, axis=-1).astype(o_ref.dtype)

    @pl.when(t < N_LAT_TILES)
    def _():
        run(0, T_ALL)

    @pl.when(t == N_LAT_TILES)
    def _():
        run(SEQ, CTX_LEN)


def _mla_attention(qm, km, vm, n_tiles):
    B = qm.shape[0]
    hw = MLA_HEADS * MLA_HEAD_PAD
    vw = MLA_HEADS * MLA_V
    return pl.pallas_call(
        _mla_kernel,
        grid=(B, n_tiles),
        in_specs=[
            pl.BlockSpec((1, TM, hw), lambda b, t: (b, t, 0)),
            pl.BlockSpec((1, T_ALL, hw), lambda b, t: (b, 0, 0)),
            pl.BlockSpec((1, T_ALL, vw), lambda b, t: (b, 0, 0)),
        ],
        out_specs=pl.BlockSpec((1, TM, vw), lambda b, t: (b, t, 0)),
        out_shape=jax.ShapeDtypeStruct((B, n_tiles * TM, vw), BF16),
        compiler_params=_cparams(("arbitrary", "arbitrary")),
    )(qm, km, vm)


POOL_PAD = POOL_WINDOWS[-1] // 2


def _pool_kernel(u_ref, w_ref, scale_ref, o_ref, pad_ref):
    width = POOL_GROUPS * POOL_CH
    lane = lax.broadcasted_iota(jnp.int32, (1, width), 1)
    half = jnp.left_shift(1, lane // POOL_CH)
    zeros = jnp.zeros((POOL_PAD, width), F32)
    base = 0
    for lo, n in ((0, SEQ), (SEQ, CTX_LEN)):
        x = u_ref[0, lo:lo + n, :]
        pad_ref[base:base + POOL_PAD, :] = zeros
        pad_ref[base + POOL_PAD:base + POOL_PAD + n, :] = x
        pad_ref[base + POOL_PAD + n:base + 2 * POOL_PAD + n, :] = zeros

        xp = pad_ref[base:base + n + 2 * POOL_PAD, :]

        def ahead(v, d):
            return jnp.concatenate([v[d:], jnp.zeros((d, width), F32)], axis=0)

        runs, run = {}, xp
        for h in (1, 2, 4, 8):
            run = run + ahead(run, h)
            runs[h] = run[POOL_PAD - h:POOL_PAD - h + n]
        win = jnp.where(half == 1, runs[1], jnp.where(half == 2, runs[2], jnp.where(half == 4, runs[4], runs[8])))
        t = lax.broadcasted_iota(jnp.int32, (n, 1), 0)
        cnt = jnp.minimum(t + half, n) - jnp.maximum(t - half, 0)
        y = (win / cnt.astype(F32) - x).astype(BF16)
        o_ref[0, lo:lo + n, :] = (_dot(y, w_ref[...]) * scale_ref[...]).astype(o_ref.dtype)
        base += n + 2 * POOL_PAD


def _pool(up, w_bd, scale):
    B = up.shape[0]
    return pl.pallas_call(
        _pool_kernel,
        grid=(B,),
        in_specs=[
            pl.BlockSpec((1, T_ALL, 256), lambda b: (b, 0, 0)),
            pl.BlockSpec((256, 256), lambda b: (0, 0)),
            pl.BlockSpec((1, 256), lambda b: (0, 0)),
        ],
        out_specs=pl.BlockSpec((1, T_ALL, 256), lambda b: (b, 0, 0)),
        out_shape=jax.ShapeDtypeStruct((B, T_ALL, 256), BF16),
        scratch_shapes=[pltpu.VMEM((T_ALL + 4 * POOL_PAD, POOL_GROUPS * POOL_CH), F32)],
        compiler_params=_cparams(("arbitrary",)),
    )(up, w_bd, scale)


DFT_ROWS = 768


def _dft_kernel(c_ref, s_ref, x_ref, o_ref):
    x = x_ref[0]
    w = FFT_GROUPS * FFT_CH
    o_ref[0] = (_dot(c_ref[...], x[:, :w]) + _dot(s_ref[...], x[:, w:])).astype(o_ref.dtype)


def _dft_positions(cmat, smat, xcs):
    B = xcs.shape[0]
    return pl.pallas_call(
        _dft_kernel,
        grid=(T_ALL // DFT_ROWS, B),
        in_specs=[
            pl.BlockSpec((DFT_ROWS, T_ALL), lambda i, b: (i, 0)),
            pl.BlockSpec((DFT_ROWS, T_ALL), lambda i, b: (i, 0)),
            pl.BlockSpec((1, T_ALL, 512), lambda i, b: (b, 0, 0)),
        ],
        out_specs=pl.BlockSpec((1, DFT_ROWS, 256), lambda i, b: (b, i, 0)),
        out_shape=jax.ShapeDtypeStruct((B, T_ALL, 256), BF16),
        compiler_params=_cparams(("arbitrary", "arbitrary")),
    )(cmat, smat, xcs)


def _merge_kernel(*refs, n_stream):
    h_refs = refs[:n_stream]
    (mod_ref, ya_ref, yb_ref, yc_ref, yd_ref, wg_ref, wbr_ref, wout_ref, g_ref, b_ref, wr_ref,
     h1_ref, u2_ref, aff_ref) = refs[n_stream:]
    pb = mod_ref.shape[0]
    rows = pb * TM
    h = _load_stream(h_refs)
    sh1 = mod_ref[:, 0, 0:1, :]
    sc1 = mod_ref[:, 0, 1:2, :]
    g1 = mod_ref[:, 0, 2:3, :]
    sh2 = mod_ref[:, 0, 3:4, :]
    sc2 = mod_ref[:, 0, 4:5, :]
    u = (h * (1.0 + sc1) + sh1).astype(BF16).reshape(rows, D_MODEL)
    ys = (ya_ref, yb_ref, yc_ref, yd_ref)
    acc = jnp.zeros((rows, D_MODEL), F32)
    for n in range(N_BRANCH):
        gate = jax.nn.sigmoid(_dot(u, wg_ref[:, n * D_MODEL:(n + 1) * D_MODEL]))
        acc = acc + gate * _dot(ys[n][...].reshape(rows, BRANCH_W), wbr_ref[n])
    mix = _dot(acc.astype(BF16), wout_ref[...]).reshape(pb, TM, D_MODEL)
    h1 = _layer_norm(DN_ALPHA * h + g1 * mix, g_ref[...], b_ref[...])
    h1_ref[...] = h1
    u2 = h1 * (1.0 + sc2) + sh2
    u2_hi = u2.astype(BF16)
    u2_ref[...] = u2_hi
    u2_lo = (u2 - u2_hi.astype(F32)).astype(BF16).reshape(rows, D_MODEL)
    hi_terms = _dot(u2_hi.reshape(rows, D_MODEL), wr_ref[...])
    logits = hi_terms[:, :ROUTER_PAD] + hi_terms[:, ROUTER_PAD:] + _dot(u2_lo, wr_ref[:, :ROUTER_PAD])
    lane = lax.broadcasted_iota(jnp.int32, (1, ROUTER_PAD), 1)
    logits = jnp.where(lane < N_EXPERTS, logits, NEG)
    e = jnp.exp(logits - jnp.max(logits, axis=-1, keepdims=True))
    aff_ref[...] = (e / jnp.sum(e, axis=-1, keepdims=True)).reshape(pb, TM, ROUTER_PAD)


def _merge(hall, mod_all, ya, yb, yc, yd, wg, wbr, wout, ln_g, ln_b, wr_cat, n_tiles):
    B = mod_all.shape[0]
    pb = _samples_per_step(B)
    row = lambda b, t: (b, t, 0)
    t_rows = n_tiles * TM
    stream_specs, stream_args = _stream_specs(hall, pb)
    return pl.pallas_call(
        functools.partial(_merge_kernel, n_stream=len(stream_args)),
        grid=(B // pb, n_tiles),
        in_specs=stream_specs + [
            pl.BlockSpec((pb, 1, 6, D_MODEL), lambda b, t: (b, t // N_LAT_TILES, 0, 0)),
            pl.BlockSpec((pb, TM, BRANCH_W), row),
            pl.BlockSpec((pb, TM, BRANCH_W), row),
            pl.BlockSpec((pb, TM, BRANCH_W), row),
            pl.BlockSpec((pb, TM, BRANCH_W), row),
            _resident(wg.shape),
            _resident(wbr.shape),
            _resident(wout.shape),
            _resident((1, D_MODEL)),
            _resident((1, D_MODEL)),
            _resident(wr_cat.shape),
        ],
        out_specs=[
            pl.BlockSpec((pb, TM, D_MODEL), row),
            pl.BlockSpec((pb, TM, D_MODEL), row),
            pl.BlockSpec((pb, TM, ROUTER_PAD), row),
        ],
        out_shape=[
            jax.ShapeDtypeStruct((B, t_rows, D_MODEL), F32),
            jax.ShapeDtypeStruct((B, t_rows, D_MODEL), BF16),
            jax.ShapeDtypeStruct((B, t_rows, ROUTER_PAD), F32),
        ],
        compiler_params=_cparams(("arbitrary", "arbitrary")),
    )(*stream_args, mod_all, ya, yb, yc, yd, wg, wbr, wout, ln_g, ln_b, wr_cat)


def _onehot(match):
    return jnp.where(match, 1.0, 0.0).astype(BF16)


def _route_kernel(a_ref, tri_ref, rank_ref, *, cap):
    a = a_ref[...]
    n = a.shape[1]

    def count(mask):
        return jnp.sum(jnp.where(mask, 1.0, 0.0), axis=-1, keepdims=True)

    p = jnp.full((a.shape[0], 1), 2.0, a.dtype)
    for k in range(ROUTE_EXP_BITS - 1, -1, -1):
        cand = p * (2.0 ** -(2 ** k))
        p = jnp.where(count(a >= cand) < cap, cand, p)
    found = count(a >= 0.5 * p) >= cap
    lo = jnp.where(found, 0.5 * p, 0.0)

    def refine(_, carry):
        lo, step = carry
        cand = lo + step
        return jnp.where(count(a >= cand) >= cap, cand, lo), 0.5 * step

    thr, _ = lax.fori_loop(0, ROUTE_MANTISSA_STEPS, refine, (lo, 0.5 * lo))
    above = a > thr
    tie = a == thr
    tri = tri_ref[:n, :n]
    tie_f = jnp.where(tie, 1.0, 0.0)
    ties_before = _dot(tie_f.astype(BF16), tri) - tie_f
    keep = above | (tie & (ties_before < cap - count(above)))
    keep_f = jnp.where(keep, 1.0, 0.0)
    slot = _dot(keep_f.astype(BF16), tri) - keep_f
    rank_ref[...] = jnp.where(keep, slot, -1.0).astype(jnp.int32)


def _route_set(aff_t, tri, cap):
    return pl.pallas_call(
        functools.partial(_route_kernel, cap=cap),
        out_shape=jax.ShapeDtypeStruct(aff_t.shape, jnp.int32),
        compiler_params=pltpu.CompilerParams(vmem_limit_bytes=VMEM_LIMIT),
    )(aff_t, tri[:aff_t.shape[1], :aff_t.shape[1]])


def _gather_kernel(start_ref, nwin_ref, rank_ref, aff_ref, *rest):
    *ctx_refs, u_ref, o_ref, g_ref = rest
    b = pl.program_id(0)
    t = pl.program_id(1)

    @pl.when(t == 0)
    def _():
        o_ref[...] = jnp.zeros_like(o_ref)
        g_ref[...] = jnp.zeros_like(g_ref)

    def latent():
        rank = rank_ref[...]
        aff = aff_ref[...]
        u = u_ref[0]
        offset = lax.broadcasted_iota(jnp.int32, (SLOT_WIN, TM), 0)

        def one_pass(k, carry):
            picks, los = [], []
            for e in range(N_EXPERTS):
                w = start_ref[b, t, e] + k * SLOT_WIN
                lo = pl.multiple_of(jnp.minimum(w, CAP_LAT - SLOT_WIN), 16)
                r = rank[e:e + 1, :]
                picks.append((r - lo == offset) & (r >= w))
                los.append(lo)
            rows = _dot(jnp.concatenate([_onehot(p) for p in picks], axis=0), u)
            for e in range(N_EXPERTS):
                sl = pl.ds(los[e], SLOT_WIN)
                new = o_ref[e, sl, :].astype(F32) + rows[e * SLOT_WIN:(e + 1) * SLOT_WIN]
                o_ref[e, sl, :] = new.astype(o_ref.dtype)
                g_ref[e, sl, :] += jnp.sum(jnp.where(picks[e], aff[e:e + 1, :], 0.0), axis=-1, keepdims=True)
            return carry

        lax.fori_loop(0, nwin_ref[b, t], one_pass, 0)

    def context():
        rank = ctx_refs[0][...]
        aff = ctx_refs[1][...]
        slot = lax.broadcasted_iota(jnp.int32, (CAP_CTX, CTX_LEN), 0)
        picks = [slot == rank[e:e + 1, :] for e in range(N_EXPERTS)]
        rows = _dot(jnp.concatenate([_onehot(p) for p in picks], axis=0), u_ref[0])
        for e in range(N_EXPERTS):
            o_ref[e, CAP_LAT:CAP_ALL, :] = rows[e * CAP_CTX:(e + 1) * CAP_CTX].astype(o_ref.dtype)
            g_ref[e, CAP_LAT:CAP_ALL, :] = jnp.sum(jnp.where(picks[e], aff[e:e + 1, :], 0.0), axis=-1, keepdims=True)

    if ctx_refs:
        pl.when(t < N_LAT_TILES)(latent)
        pl.when(t == N_LAT_TILES)(context)
    else:
        latent()


def _gather_tokens(ranks, affs, u2, start, nwin):
    B = u2.shape[0]
    with_ctx = len(ranks) > 1
    cap = CAP_ALL if with_ctx else CAP_LAT
    lat_spec = pl.BlockSpec((N_EXPERTS, TM), lambda b, t, *_: (b, jnp.minimum(t, N_LAT_TILES - 1)))
    ctx_spec = pl.BlockSpec((N_EXPERTS, CTX_LEN), lambda b, t, *_: (b, 0))
    grid_spec = pltpu.PrefetchScalarGridSpec(
        num_scalar_prefetch=2,
        grid=(B, N_TILES if with_ctx else N_LAT_TILES),
        in_specs=[lat_spec, lat_spec] + ([ctx_spec, ctx_spec] if with_ctx else [])
        + [pl.BlockSpec((1, TM, D_MODEL), lambda b, t, *_: (b, t, 0))],
        out_specs=[
            pl.BlockSpec((N_EXPERTS, cap, D_MODEL), lambda b, t, *_: (0, b, 0)),
            pl.BlockSpec((N_EXPERTS, cap, 1), lambda b, t, *_: (0, b, 0)),
        ],
    )
    args = [ranks[0], affs[0]] + ([ranks[1], affs[1]] if with_ctx else [])
    return pl.pallas_call(
        _gather_kernel,
        grid_spec=grid_spec,
        out_shape=[
            jax.ShapeDtypeStruct((N_EXPERTS, B * cap, D_MODEL), BF16),
            jax.ShapeDtypeStruct((N_EXPERTS, B * cap, 1), F32),
        ],
        compiler_params=_cparams(("arbitrary", "arbitrary")),
    )(start, nwin, *args, u2)


def _expert_kernel(x_ref, g_ref, wg_ref, wu_ref, wd_ref, o_ref, acc_ref, *, chunk):
    f = pl.program_id(1)

    def ff_tile(first):
        wg = wg_ref[0, 0].astype(BF16)
        wu = wu_ref[0, 0].astype(BF16)
        wd = wd_ref[0, 0].astype(BF16)
        for c in range(x_ref.shape[1] // chunk):
            rows = pl.ds(c * chunk, chunk)
            x = x_ref[0, rows, :]
            a = _dot(x, wg)
            u = _dot(x, wu)
            y = _dot((a * jax.nn.sigmoid(a) * u).astype(BF16), wd)
            acc_ref[rows, :] = y if first else acc_ref[rows, :] + y

    pl.when(f == 0)(lambda: ff_tile(True))
    pl.when(f > 0)(lambda: ff_tile(False))

    @pl.when(f == pl.num_programs(1) - 1)
    def _():
        o_ref[0] = (acc_ref[...] * g_ref[0]).astype(o_ref.dtype)


def _experts(xe, gcol, w_gate, w_up, w_down, layer):
    rows = xe.shape[1]
    chunk = next(c for c in (XROW_CHUNK, 512, CAP_ALL, CAP_LAT) if rows % c == 0)
    return pl.pallas_call(
        functools.partial(_expert_kernel, chunk=chunk),
        grid=(N_EXPERTS, EXPERT_FF // FF_TILE),
        in_specs=[
            pl.BlockSpec((1, rows, D_MODEL), lambda e, f: (e, 0, 0)),
            pl.BlockSpec((1, rows, 1), lambda e, f: (e, 0, 0)),
            pl.BlockSpec((1, 1, D_MODEL, FF_TILE), lambda e, f: (layer, e, 0, f)),
            pl.BlockSpec((1, 1, D_MODEL, FF_TILE), lambda e, f: (layer, e, 0, f)),
            pl.BlockSpec((1, 1, FF_TILE, D_MODEL), lambda e, f: (layer, e, f, 0)),
        ],
        out_specs=pl.BlockSpec((1, rows, D_MODEL), lambda e, f: (e, 0, 0)),
        out_shape=jax.ShapeDtypeStruct((N_EXPERTS, rows, D_MODEL), BF16),
        scratch_shapes=[pltpu.VMEM((rows, D_MODEL), F32)],
        compiler_params=_cparams(("arbitrary", "arbitrary")),
    )(xe, gcol, w_gate, w_up, w_down)


SLOT_WIN = 64


def _scatter_kernel(start_ref, nwin_ref, h_ref, mod_ref, y_ref, g_ref, b_ref, rank_ref, o_ref, fx_ref):
    b = pl.program_id(0)
    t = pl.program_id(1)
    rank = rank_ref[0]

    def spread_ranks(width):
        n = N_EXPERTS * width
        col = lax.broadcasted_iota(jnp.int32, (N_EXPERTS, n), 1)
        spread = _onehot(col // width == lax.broadcasted_iota(jnp.int32, (N_EXPERTS, n), 0))
        return _dot(rank.astype(F32).astype(BF16), spread)

    def finish(fx):
        g2 = mod_ref[0, 0, 5:6, :]
        o_ref[0] = _layer_norm(DN_ALPHA * h_ref[0] + g2 * fx, g_ref[...], b_ref[...])

    def latent():
        n = N_EXPERTS * SLOT_WIN
        rank_cols = spread_ranks(SLOT_WIN)
        lane = lax.broadcasted_iota(jnp.int32, (1, n), 1)
        block = lane // SLOT_WIN
        offset = (lane % SLOT_WIN).astype(F32)
        fx_ref[...] = jnp.zeros_like(fx_ref)

        def one_pass(k, carry):
            want = jnp.zeros((1, n), F32)
            have = jnp.zeros((1, n), F32)
            rows = []
            for e in range(N_EXPERTS):
                w = start_ref[b, t, e] + k * SLOT_WIN
                lo = pl.multiple_of(jnp.minimum(w, CAP_LAT - SLOT_WIN), 16)
                rows.append(y_ref[e, 0, pl.ds(lo, SLOT_WIN), :])
                want = jnp.where(block == e, w.astype(F32), want)
                have = jnp.where(block == e, lo.astype(F32), have)
            hit = (rank_cols - have == offset) & (rank_cols >= want)
            fx_ref[...] += _dot(_onehot(hit), jnp.concatenate(rows, axis=0))
            return carry

        lax.fori_loop(0, nwin_ref[b, t], one_pass, 0)
        finish(fx_ref[...])

    def context():
        n = N_EXPERTS * CAP_CTX
        slot = (lax.broadcasted_iota(jnp.int32, (TM, n), 1) % CAP_CTX).astype(F32)
        y = y_ref[:, 0, CAP_LAT:CAP_ALL, :].reshape(n, D_MODEL)
        finish(_dot(_onehot(spread_ranks(CAP_CTX) == slot), y))

    if y_ref.shape[2] == CAP_LAT:
        latent()
    else:
        pl.when(t < N_LAT_TILES)(latent)
        pl.when(t == N_LAT_TILES)(context)


def _slot_windows(rank_lat):
    B = rank_lat.shape[0] // N_EXPERTS
    cnt = jnp.sum((rank_lat >= 0).reshape(B, N_EXPERTS, N_LAT_TILES, TM), axis=-1, dtype=jnp.int32)
    first = jnp.cumsum(cnt, axis=-1) - cnt
    start = (first // 16) * 16
    nwin = jnp.max((first - start + cnt + SLOT_WIN - 1) // SLOT_WIN, axis=1)
    return start.transpose(0, 2, 1), nwin


def _scatter_ln(h1, mod_all, ye, ln_g, ln_b, rank_tok, start, nwin):
    B = h1.shape[0]
    cap = ye.shape[2]
    n_tiles = rank_tok.shape[1] // TM
    grid_spec = pltpu.PrefetchScalarGridSpec(
        num_scalar_prefetch=2,
        grid=(B, n_tiles),
        in_specs=[
            pl.BlockSpec((1, TM, D_MODEL), lambda b, t, *_: (b, t, 0)),
            pl.BlockSpec((1, 1, 6, D_MODEL), lambda b, t, *_: (b, t // N_LAT_TILES, 0, 0)),
            pl.BlockSpec((N_EXPERTS, 1, cap, D_MODEL), lambda b, t, *_: (0, b, 0, 0)),
            pl.BlockSpec((1, D_MODEL), lambda b, t, *_: (0, 0)),
            pl.BlockSpec((1, D_MODEL), lambda b, t, *_: (0, 0)),
            pl.BlockSpec((1, TM, N_EXPERTS), lambda b, t, *_: (b, t, 0)),
        ],
        out_specs=pl.BlockSpec((1, TM, D_MODEL), lambda b, t, *_: (b, t, 0)),
        scratch_shapes=[pltpu.VMEM((TM, D_MODEL), F32)],
    )
    return pl.pallas_call(
        _scatter_kernel,
        grid_spec=grid_spec,
        out_shape=jax.ShapeDtypeStruct((B, n_tiles * TM, D_MODEL), F32),
        compiler_params=_cparams(("arbitrary", "arbitrary")),
    )(start, nwin, h1, mod_all, ye, ln_g, ln_b, rank_tok)


def _rope_tables():
    n_freq = MLA_ROPE // 4
    inv = ROPE_BASE ** (-jnp.arange(n_freq, dtype=F32) / n_freq)
    t = jnp.arange(SEQ)
    row = (t // GRID_W).astype(F32)
    col = (t % GRID_W).astype(F32)
    ang = jnp.concatenate([row[:, None] * inv, col[:, None] * inv], axis=-1)
    cos, sin = jnp.cos(ang), jnp.sin(ang)
    ones = jnp.ones((SEQ, MLA_NOPE), F32)
    pad1 = jnp.ones((SEQ, MLA_HEAD_PAD - MLA_NOPE - MLA_ROPE), F32)
    cos_t = jnp.concatenate([ones, cos, cos, pad1], axis=-1)
    sin_t = jnp.concatenate([0 * ones, -sin, sin, 0 * pad1], axis=-1)
    cos_t = jnp.concatenate([cos_t, jnp.ones((CTX_LEN, MLA_HEAD_PAD), F32)], axis=0)
    sin_t = jnp.concatenate([sin_t, jnp.zeros((CTX_LEN, MLA_HEAD_PAD), F32)], axis=0)
    return cos_t, sin_t


def _dft_tables():
    def block(n):
        split = 64
        k = jnp.arange(n, dtype=jnp.int32)[None, :]
        jh = jnp.arange(n // split, dtype=jnp.int32)[:, None]
        jl = jnp.arange(split, dtype=jnp.int32)[:, None]
        ang_h = ((jh * split * k) % n).astype(F32) * (2.0 * np.pi / n)
        ang_l = ((jl * k) % n).astype(F32) * (2.0 * np.pi / n)
        ch, sh = jnp.cos(ang_h)[:, None, :], jnp.sin(ang_h)[:, None, :]
        cl_, sl_ = jnp.cos(ang_l)[None, :, :], jnp.sin(ang_l)[None, :, :]
        sc = 1.0 / np.sqrt(n * FFT_CH)
        return ((ch * cl_ - sh * sl_) * sc).reshape(n, n), (-(sh * cl_ + ch * sl_) * sc).reshape(n, n)

    cl, sl = block(SEQ)
    cc, sc_ = block(CTX_LEN)

    def diag(a, b):
        top = jnp.concatenate([a, jnp.zeros((SEQ, CTX_LEN), F32)], axis=1)
        bot = jnp.concatenate([jnp.zeros((CTX_LEN, SEQ), F32), b], axis=1)
        return jnp.concatenate([top, bot], axis=0).astype(BF16)

    j = jnp.arange(FFT_CH, dtype=jnp.int32)
    ang = ((j[:, None] * j[None, :]) % FFT_CH).astype(F32) * (2.0 * np.pi / FFT_CH)
    eye = jnp.eye(FFT_GROUPS, dtype=F32)
    bd = jnp.concatenate([jnp.kron(eye, jnp.cos(ang)), jnp.kron(eye, jnp.sin(ang))], axis=1).astype(BF16)
    return diag(cl, cc), diag(sl, sc_), bd


def _na_bias_index():
    R = SEQ // GRID_W
    pats = [(0, 0), (NA_QROWS, 0), (R - NA_QROWS, R - NA_KROWS)]
    q = np.arange(TM)
    k = np.arange(NA_KWIN)
    qi, qc = q // GRID_W, q % GRID_W
    kj, kc = k // GRID_W, k % GRID_W
    c0 = np.clip(qc - NA_WIN_COLS // 2, 0, GRID_W - NA_WIN_COLS)
    col_in = (kc[None, :] >= c0[:, None]) & (kc[None, :] < c0[:, None] + NA_WIN_COLS)
    col_off = np.clip(kc[None, :] - qc[:, None], -(NA_WIN_COLS - 1), NA_WIN_COLS - 1) + (NA_WIN_COLS - 1)
    del col_off
    row_offs, valids = [], []
    for rb, ks in pats:
        qr = rb + np.arange(NA_QROWS)
        kr = ks + np.arange(NA_KROWS)
        r0 = np.clip(qr - NA_WIN_ROWS // 2, 0, R - NA_WIN_ROWS)
        row_in = (kr[None, :] >= r0[:, None]) & (kr[None, :] < r0[:, None] + NA_WIN_ROWS)
        row_offs.append(np.where(row_in, kr[None, :] - qr[:, None] + (NA_WIN_ROWS - 1), -1))
        valids.append(row_in[qi][:, kj] & col_in)
    return np.stack(row_offs), np.stack(valids)


def _na_bias(rpb):
    row_off, valid = _na_bias_index()
    reach = GRID_W - NA_WIN_COLS
    padded = jnp.pad(rpb, ((0, 0), (0, 0), (0, 0), (reach, reach)))
    band = jnp.stack([padded[..., GRID_W - 1 - qc:2 * GRID_W - 1 - qc] for qc in range(GRID_W)], axis=-2)
    masked = jnp.zeros(band.shape[:2] + band.shape[3:], F32)
    pats = []
    for p in range(row_off.shape[0]):
        qrows = []
        for qi in range(NA_QROWS):
            blocks = [masked if row_off[p, qi, kj] < 0 else band[:, :, row_off[p, qi, kj]] for kj in range(NA_KROWS)]
            qrows.append(jnp.concatenate(blocks, axis=-1))
        pats.append(jnp.concatenate(qrows, axis=-2))
    b = jnp.stack(pats, axis=1)
    return jnp.where(jnp.asarray(valid)[None, :, None], b, NEG)


def _rearranged_in_weights(w_in):
    z = lambda n: jnp.zeros((D_MODEL, n), F32)
    kr = w_in[:, 1664:1696]
    kr_e, kr_o = kr[:, 0::2], kr[:, 1::2]
    pad = MLA_HEAD_PAD - MLA_NOPE - MLA_ROPE
    krb = jnp.concatenate([z(MLA_NOPE), kr_e, kr_o, z(pad)], axis=1)
    krs = jnp.concatenate([z(MLA_NOPE), kr_o, kr_e, z(pad)], axis=1)
    wsm = jnp.concatenate([w_in[:, :1664], krb, krs], axis=1).astype(BF16)
    return wsm, w_in[:, 1696:].astype(BF16)


def _rearranged_mla_weights(w_uq, w_ukv):
    pad = MLA_HEAD_PAD - MLA_NOPE - MLA_ROPE
    wq = w_uq.reshape(MLA_Q_RANK, MLA_HEADS, MLA_NOPE + MLA_ROPE)
    nope, rope = wq[..., :MLA_NOPE], wq[..., MLA_NOPE:]
    r_e, r_o = rope[..., 0::2], rope[..., 1::2]
    zq = jnp.zeros((MLA_Q_RANK, MLA_HEADS, pad), F32)
    q_main = jnp.concatenate([nope, r_e, r_o, zq], axis=-1).reshape(MLA_Q_RANK, -1)
    q_swap = jnp.concatenate([nope, r_o, r_e, zq], axis=-1).reshape(MLA_Q_RANK, -1)
    wuq2 = jnp.concatenate([q_main, q_swap], axis=1).astype(BF16)
    wkv = w_ukv.reshape(MLA_KV_RANK, MLA_HEADS, MLA_NOPE + MLA_V)
    k_nope, v = wkv[..., :MLA_NOPE], wkv[..., MLA_NOPE:]
    zk = jnp.zeros((MLA_KV_RANK, MLA_HEADS, MLA_HEAD_PAD - MLA_NOPE), F32)
    k_main = jnp.concatenate([k_nope, zk], axis=-1).reshape(MLA_KV_RANK, -1)
    wukv2 = jnp.concatenate([k_main, v.reshape(MLA_KV_RANK, -1)], axis=1).astype(BF16)
    return wuq2, wukv2


def _route(aff, tri, with_ctx):
    B = aff.shape[0]
    a = aff.transpose(0, 2, 1).reshape(B * N_EXPERTS, -1)
    affs = [a[:, :SEQ]] + ([a[:, SEQ:]] if with_ctx else [])
    ranks = [_route_set(s, tri, cap) for s, cap in zip(affs, (CAP_LAT, CAP_CTX))]
    rank_tok = jnp.concatenate([r.reshape(B, N_EXPERTS, -1) for r in ranks], axis=-1).transpose(0, 2, 1)
    return affs, ranks, rank_tok


def kernel(x, c, ctx, c_ctx, w_mod, b_mod, w_in, na_rpb, pool_w, pool_scale, mla_q_norm, mla_w_uq, mla_kv_norm,
           mla_w_ukv, w_branch, w_out, ln1_g, ln1_b, w_router, w_gate, w_up, w_down, ln2_g, ln2_b):
    B = x.shape[0]
    assert x.shape == (B, SEQ, D_MODEL) and ctx.shape == (B, CTX_LEN, D_MODEL) and B + 1 <= 16

    cc = jnp.concatenate([c, c_ctx[None], jnp.zeros((16 - B - 1, D_MODEL), F32)], axis=0)
    mod = _modulation(cc, w_mod, b_mod).reshape(DEPTH, 16, 6, D_MODEL)
    cos_t, sin_t = _rope_tables()
    cmat, smat, bd = _dft_tables()
    na_bias = _na_bias(na_rpb)
    tri = jnp.triu(jnp.ones((SEQ, SEQ), BF16))
    hall = (x, ctx)

    for l in range(DEPTH):
        with_ctx = l < DEPTH - 1
        n_tiles = N_TILES if with_ctx else N_LAT_TILES
        mod_all = jnp.stack([mod[l, :B], jnp.broadcast_to(mod[l, B], (B, 6, D_MODEL))], axis=1)
        wsm, wgates = _rearranged_in_weights(w_in[l])
        wuq2, wukv2 = _rearranged_mla_weights(mla_w_uq[l], mla_w_ukv[l])
        qa, ka, va, up, xcs, qm, km, vm = _inproj(
            hall, mod_all, wsm, mla_q_norm[l][None], mla_kv_norm[l][None], wuq2, wukv2, cos_t, sin_t, bd)
        ya = _na_attention(qa, ka, va, na_bias, l, n_tiles)
        pool_bd = jax.scipy.linalg.block_diag(*[pool_w[l, g] for g in range(POOL_GROUPS)]).astype(BF16)
        yb = _pool(up, pool_bd, pool_scale[l][None])
        yc = _dft_positions(cmat, smat, xcs)
        yd = _mla_attention(qm, km, vm, n_tiles)
        wr = jnp.pad(w_router[l], ((0, 0), (0, ROUTER_PAD - N_EXPERTS)))
        wr_hi = wr.astype(BF16)
        wr_cat = jnp.concatenate([wr_hi, (wr - wr_hi.astype(F32)).astype(BF16)], axis=1)
        h1, u2, aff = _merge(hall, mod_all, ya, yb, yc, yd, wgates, w_branch[l].astype(BF16), w_out[l].astype(BF16),
                             ln1_g[l][None], ln1_b[l][None], wr_cat, n_tiles)
        affs, ranks, rank_tok = _route(aff[:, :, :N_EXPERTS], tri, with_ctx)
        start, nwin = _slot_windows(ranks[0])
        xe, gcol = _gather_tokens(ranks, affs, u2, start, nwin)
        ye = _experts(xe, gcol, w_gate, w_up, w_down, l)
        hall = _scatter_ln(h1, mod_all, ye.reshape(N_EXPERTS, B, -1, D_MODEL), ln2_g[l][None], ln2_b[l][None], rank_tok,
                           start, nwin)
    return hall
```
